```python
import math
import jax, jax.numpy as jnp
from jax import lax
import numpy as np

D_MODEL = 1024
BATCH = 4
SEQ = 4096
DEPTH = 1
DEC_BATCH = 128
DEC_SEQ = 8
PAST_LEN = 8192
PAGE_SIZE = 128

D_FF = 2816
CHUNK = 128
A_GROUPS = 4
A_GROUP_DIM = 128
A_WIDTH = A_GROUPS * A_GROUP_DIM
N_HEADS = 8
QK_NOPE = 64
QK_ROPE = 32
QK_HEAD = QK_NOPE + QK_ROPE
V_DIM = 64
Q_LORA = 384
KV_LORA = 256
ROPE_THETA = 10000.0
Q_BLOCK = 128
PLE_DIM = 256
EPS = 1e-6
IN_COLS = 2 * A_WIDTH + Q_LORA + KV_LORA + QK_ROPE + 2 * D_MODEL

kernel_name = 'hybrid_gmlp_mla_macaron_decode_step'


def rmsnorm(x, g):
    xf = x.astype(jnp.float32)
    y = xf * lax.rsqrt(jnp.mean(xf * xf, axis=-1, keepdims=True) + EPS)
    return (y * g.astype(jnp.float32)).astype(x.dtype)


def layernorm(x, g, b):
    xf = x.astype(jnp.float32)
    xc = xf - jnp.mean(xf, axis=-1, keepdims=True)
    y = xc * lax.rsqrt(jnp.mean(xc * xc, axis=-1, keepdims=True) + EPS)
    return (y * g.astype(jnp.float32) + b.astype(jnp.float32)).astype(x.dtype)


def swiglu(x, w_in, w_out):
    a, b = jnp.split(x @ w_in, 2, axis=-1)
    return (jax.nn.silu(a) * b) @ w_out


def rope_cos_sin(n_pos, offset, dtype):
    pos = jnp.arange(n_pos, dtype=jnp.float32) + offset
    inv_freq = ROPE_THETA ** (-jnp.arange(0, QK_ROPE, 2, dtype=jnp.float32) / QK_ROPE)
    ang = pos[:, None] * inv_freq[None, :]
    return jnp.cos(ang).astype(dtype), jnp.sin(ang).astype(dtype)


def apply_rope(x, cos, sin):
    x1, x2 = jnp.split(x, 2, axis=-1)
    return jnp.concatenate([x1 * cos - x2 * sin, x1 * sin + x2 * cos], axis=-1)


def chunk_spatial_mix(v, w_s, b_s):
    B, T, G, C = v.shape
    n_chunks = -(-T // CHUNK)
    pad = n_chunks * CHUNK - T
    vp = jnp.pad(v, ((0, 0), (0, pad), (0, 0), (0, 0))).reshape(B, n_chunks, CHUNK, G, C)
    causal = jnp.tril(jnp.ones((CHUNK, CHUNK), dtype=bool))
    w = jnp.where(causal[None], w_s, jnp.zeros_like(w_s))
    z = jnp.einsum('gts,bnsgc->bntgc', w, vp) + b_s.T[:, :, None]
    return z.reshape(B, n_chunks * CHUNK, G, C)[:, :T]


def expand_keys(c_kv, k_rope, w_ukv, g_k):
    kv = (c_kv @ w_ukv).reshape(c_kv.shape[:-1] + (N_HEADS, QK_NOPE + V_DIM))
    k_nope, v = kv[..., :QK_NOPE], kv[..., QK_NOPE:]
    k_pe = jnp.broadcast_to(k_rope[..., None, :], k_nope.shape[:-1] + (QK_ROPE,))
    k = rmsnorm(jnp.concatenate([k_nope, k_pe], axis=-1), g_k)
    return k, v


def attention(q, k, v, mask):
    s = jnp.einsum('...thd,...shd->...hts', q, k).astype(jnp.float32) * (QK_HEAD ** -0.5)
    s = jnp.where(mask, s, jnp.finfo(jnp.float32).min)
    p = jax.nn.softmax(s, axis=-1).astype(v.dtype)
    return jnp.einsum('...hts,...shd->...thd', p, v)


def prompt_attend(q, c_kv, k_rope, w_ukv, g_k):
    B, S = q.shape[0], q.shape[1]
    k, v = expand_keys(c_kv, k_rope, w_ukv, g_k)
    n_blocks = S // Q_BLOCK
    qb = q.reshape(B, n_blocks, Q_BLOCK, N_HEADS, QK_HEAD).transpose(1, 0, 2, 3, 4)
    k_pos = jnp.arange(S)

    def one_block(args):
        q_blk, j = args
        q_pos = j * Q_BLOCK + jnp.arange(Q_BLOCK)
        return attention(q_blk, k, v, k_pos[None, :] <= q_pos[:, None])

    o = lax.map(one_block, (qb, jnp.arange(n_blocks)))
    return o.transpose(1, 0, 2, 3, 4).reshape(B, S, N_HEADS, V_DIM)


def make_sample_attend(cache_ckv, cache_krope, page_table, layer):
    def attend(q, c_new, kr_new, w_ukv, g_k):
        def one_seq(args):
            q1, c1, kr1, pt = args
            past_c = cache_ckv[layer, pt].reshape(-1, KV_LORA).astype(c1.dtype)
            past_kr = cache_krope[layer, pt].reshape(-1, QK_ROPE).astype(kr1.dtype)
            c_all = jnp.concatenate([past_c, c1], axis=0)
            kr_all = jnp.concatenate([past_kr, kr1], axis=0)
            k, v = expand_keys(c_all, kr_all, w_ukv, g_k)
            n_past = past_c.shape[0]
            t_new = q1.shape[0]
            mask = jnp.arange(n_past + t_new)[None, :] <= (n_past + jnp.arange(t_new))[:, None]
            return attention(q1, k, v, mask)
        return lax.map(one_seq, (q, c_new, kr_new, page_table))
    return attend


def trunk_layer(x, p, offset, attend,
                ffn1_norm, ffn1_w_in, ffn1_w_out, mix_norm, w_in, qa_norm, kva_norm,
                w_uq, w_ukv, q_norm, k_norm, sgu_ln_g, sgu_ln_b, sgu_w_s, sgu_b_s,
                w_a_out, w_b_out, w_o, ffn2_norm, ffn2_w_in, ffn2_w_out,
                ple_norm, ple_w_gate, ple_w_proj):
    B, T, _ = x.shape
    h = x + 0.5 * swiglu(rmsnorm(x, ffn1_norm), ffn1_w_in, ffn1_w_out)
    n = rmsnorm(h, mix_norm)
    proj = n @ w_in
    s1 = 2 * A_WIDTH
    s2 = s1 + Q_LORA
    s3 = s2 + KV_LORA
    s4 = s3 + QK_ROPE
    uv, cq, ckv, kr, gates = jnp.split(proj, [s1, s2, s3, s4], axis=-1)
    u, v = jnp.split(jax.nn.gelu(uv), 2, axis=-1)
    vn = layernorm(v, sgu_ln_g, sgu_ln_b)
    z = chunk_spatial_mix(vn.reshape(B, T, A_GROUPS, A_GROUP_DIM), sgu_w_s, sgu_b_s)
    ya = (u * z.reshape(B, T, A_WIDTH)) @ w_a_out
    cos, sin = rope_cos_sin(T, offset, x.dtype)
    q = (rmsnorm(cq, qa_norm) @ w_uq).reshape(B, T, N_HEADS, QK_HEAD)
    q = jnp.concatenate([q[..., :QK_NOPE],
                         apply_rope(q[..., QK_NOPE:], cos[:, None, :], sin[:, None, :])], axis=-1)
    q = rmsnorm(q, q_norm)
    c_kv = rmsnorm(ckv, kva_norm)
    k_rope = apply_rope(kr, cos, sin)
    o = attend(q, c_kv, k_rope, w_ukv, k_norm)
    yb = o.reshape(B, T, N_HEADS * V_DIM) @ w_b_out
    g_a, g_b = jnp.split(gates, 2, axis=-1)
    h = h + (jax.nn.sigmoid(g_a) * ya + jax.nn.sigmoid(g_b) * yb) @ w_o
    h = h + 0.5 * swiglu(rmsnorm(h, ffn2_norm), ffn2_w_in, ffn2_w_out)
    h = h + jax.nn.sigmoid(rmsnorm(h, ple_norm) @ ple_w_gate) * (p @ ple_w_proj)
    return h, c_kv, k_rope, vn


def setup_inputs(seed: int = 0) -> dict:
    key = jax.random.key(seed)
    ks = jax.random.split(key, 32)
    f32 = jnp.float32
    n_pages = PAST_LEN // PAGE_SIZE
    n_used = DEC_BATCH * n_pages
    n_pool = n_used + n_used // 4

    def nrm(k, shape, scale=1.0):
        return jax.random.normal(k, shape, f32) * scale

    def gain(k, n):
        return 1.0 + 0.01 * jax.random.normal(k, (DEPTH, n), f32)

    page_table = jax.random.permutation(ks[4], n_pool)[:n_used].reshape(DEC_BATCH, n_pages).astype(jnp.int32)
    return {
        'x_prompt': nrm(ks[0], (BATCH, SEQ, D_MODEL)),
        'x_sample': nrm(ks[1], (DEC_BATCH, DEC_SEQ, D_MODEL)),
        'cache_ckv': nrm(ks[2], (DEPTH, n_pool, PAGE_SIZE, KV_LORA)),
        'cache_krope': nrm(ks[3], (DEPTH, n_pool, PAGE_SIZE, QK_ROPE)),
        'page_table': page_table,
        'p_prompt': nrm(ks[5], (DEPTH, BATCH, SEQ, PLE_DIM)),
        'p_sample': nrm(ks[6], (DEPTH, DEC_BATCH, DEC_SEQ, PLE_DIM)),
        'ffn1_norm': gain(ks[7], D_MODEL),
        'ffn1_w_in': nrm(ks[8], (DEPTH, D_MODEL, 2 * D_FF), D_MODEL ** -0.5),
        'ffn1_w_out': nrm(ks[9], (DEPTH, D_FF, D_MODEL), D_FF ** -0.5),
        'mix_norm': gain(ks[10], D_MODEL),
        'w_in': nrm(ks[11], (DEPTH, D_MODEL, IN_COLS), D_MODEL ** -0.5),
        'qa_norm': gain(ks[12], Q_LORA),
        'kva_norm': gain(ks[13], KV_LORA),
        'w_uq': nrm(ks[14], (DEPTH, Q_LORA, N_HEADS * QK_HEAD), Q_LORA ** -0.5),
        'w_ukv': nrm(ks[15], (DEPTH, KV_LORA, N_HEADS * (QK_NOPE + V_DIM)), KV_LORA ** -0.5),
        'q_norm': gain(ks[16], QK_HEAD),
        'k_norm': gain(ks[17], QK_HEAD),
        'sgu_ln_g': gain(ks[18], A_WIDTH),
        'sgu_ln_b': nrm(ks[19], (DEPTH, A_WIDTH), 0.01),
        'sgu_w_s': nrm(ks[20], (DEPTH, A_GROUPS, CHUNK, CHUNK), CHUNK ** -0.5),
        'sgu_b_s': 1.0 + nrm(ks[21], (DEPTH, A_GROUPS, CHUNK), 0.1),
        'w_a_out': nrm(ks[22], (DEPTH, A_WIDTH, D_MODEL), A_WIDTH ** -0.5),
        'w_b_out': nrm(ks[23], (DEPTH, N_HEADS * V_DIM, D_MODEL), (N_HEADS * V_DIM) ** -0.5),
        'w_o': nrm(ks[24], (DEPTH, D_MODEL, D_MODEL), D_MODEL ** -0.5),
        'ffn2_norm': gain(ks[25], D_MODEL),
        'ffn2_w_in': nrm(ks[26], (DEPTH, D_MODEL, 2 * D_FF), D_MODEL ** -0.5),
        'ffn2_w_out': nrm(ks[27], (DEPTH, D_FF, D_MODEL), D_FF ** -0.5),
        'ple_norm': gain(ks[28], D_MODEL),
        'ple_w_gate': nrm(ks[29], (DEPTH, D_MODEL, D_MODEL), D_MODEL ** -0.5),
        'ple_w_proj': nrm(ks[30], (DEPTH, PLE_DIM, D_MODEL), PLE_DIM ** -0.5),
    }


def reference(x_prompt, x_sample, cache_ckv, cache_krope, page_table, p_prompt, p_sample,
              ffn1_norm, ffn1_w_in, ffn1_w_out, mix_norm, w_in, qa_norm, kva_norm,
              w_uq, w_ukv, q_norm, k_norm, sgu_ln_g, sgu_ln_b, sgu_w_s, sgu_b_s,
              w_a_out, w_b_out, w_o, ffn2_norm, ffn2_w_in, ffn2_w_out,
              ple_norm, ple_w_gate, ple_w_proj):
    y_prompt = x_prompt
    y_sample = x_sample
    ckv_p, kr_p, ckv_s, kr_s, v_s = [], [], [], [], []
    for i in range(DEPTH):
        lw = (ffn1_norm[i], ffn1_w_in[i], ffn1_w_out[i], mix_norm[i], w_in[i], qa_norm[i],
              kva_norm[i], w_uq[i], w_ukv[i], q_norm[i], k_norm[i], sgu_ln_g[i], sgu_ln_b[i],
              sgu_w_s[i], sgu_b_s[i], w_a_out[i], w_b_out[i], w_o[i], ffn2_norm[i],
              ffn2_w_in[i], ffn2_w_out[i], ple_norm[i], ple_w_gate[i], ple_w_proj[i])
        y_prompt, c_p, k_p, _ = trunk_layer(y_prompt, p_prompt[i], 0, prompt_attend, *lw)
        sample_attend = make_sample_attend(cache_ckv, cache_krope, page_table, i)
        y_sample, c_s, k_s, vn_s = trunk_layer(y_sample, p_sample[i], PAST_LEN, sample_attend, *lw)
        ckv_p.append(c_p)
        kr_p.append(k_p)
        ckv_s.append(c_s)
        kr_s.append(k_s)
        v_s.append(vn_s)
    return (y_prompt, y_sample, jnp.stack(ckv_p), jnp.stack(kr_p),
            jnp.stack(ckv_s), jnp.stack(kr_s), jnp.stack(v_s))
```

```python
import functools

import jax
import jax.numpy as jnp
from jax import lax
from jax.experimental import pallas as pl
from jax.experimental.pallas import tpu as pltpu

F32 = jnp.float32
BF16 = jnp.bfloat16

EPS = 1e-6
N_HEADS = 8
QK_NOPE = 64
QK_ROPE = 32
QK_HEAD = QK_NOPE + QK_ROPE
V_DIM = 64
HEAD_PAD = 128
A_GROUPS = 4
CHUNK = 128
ROPE_THETA = 10000.0
PAGE = 128

FF_CHUNK = 256
TOKEN_TILE = 512
VMEM_LIMIT = 56 * 1024 * 1024

_NT = (((1,), (1,)), ((), ()))


def _dot(a, b):
    return jnp.dot(a, b, preferred_element_type=F32)


def _dot_nt(a, b):
    return lax.dot_general(a, b, _NT, preferred_element_type=F32)


def _rms(x, g):
    return x * lax.rsqrt(jnp.mean(x * x, axis=-1, keepdims=True) + EPS) * g


def _ffn_half(x, g_ref, win_ref, wout_ref, act_ref):
    d_ff = wout_ref.shape[0]
    xn = _rms(x, g_ref[...]).astype(BF16)
    for c in range(d_ff // FF_CHUNK):
        lo = c * FF_CHUNK
        a = _dot(xn, win_ref[:, lo:lo + FF_CHUNK])
        b = _dot(xn, win_ref[:, d_ff + lo:d_ff + lo + FF_CHUNK])
        act_ref[:, lo:lo + FF_CHUNK] = (a * jax.nn.sigmoid(a) * b).astype(BF16)
    return x + 0.5 * _dot(act_ref[...], wout_ref[...])


def _head_norm(xh, gain):
    ss = jnp.sum(xh * xh, axis=-1, keepdims=True)
    return xh * lax.rsqrt(ss * (1.0 / QK_HEAD) + EPS) * gain


def _pre_kernel(expand_kv, write_vn, *refs):
    (x_ref, cos_ref, sin_ref, f1g_ref, f1in_ref, f1out_ref, mixg_ref, wpre_ref, qag_ref, kvag_ref,
     wuq_ref, qg_ref, lng_ref, lnb_ref, mixw_ref, mixb_ref) = refs[:16]
    pos = 16
    if expand_kv:
        wukv_ref, kg_ref = refs[pos:pos + 2]
        pos += 2
    h1_ref, ain_ref, q_ref, ckv_ref, kr_ref = refs[pos:pos + 5]
    pos += 5
    if expand_kv:
        k_ref, v_ref = refs[pos:pos + 2]
        pos += 2
    if write_vn:
        vn_ref = refs[pos]
        pos += 1
    act_ref = refs[pos]

    a_width = lng_ref.shape[1]
    q_lora = qag_ref.shape[1]
    kv_lora = kvag_ref.shape[1]
    rows = x_ref.shape[0]

    h1 = _ffn_half(x_ref[...], f1g_ref, f1in_ref, f1out_ref, act_ref)
    h1_ref[...] = h1
    n = _rms(h1, mixg_ref[...]).astype(BF16)

    c0 = 2 * a_width
    uv = jax.nn.gelu(_dot(n, wpre_ref[:, :c0]))
    u, v = uv[:, :a_width], uv[:, a_width:]
    vc = v - jnp.mean(v, axis=-1, keepdims=True)
    vn = vc * lax.rsqrt(jnp.mean(vc * vc, axis=-1, keepdims=True) + EPS) * lng_ref[...] + lnb_ref[...]
    if write_vn:
        vn_ref[...] = vn
    vnb = vn.astype(BF16)
    gd = a_width // A_GROUPS
    for r in range(rows // CHUNK):
        rs = slice(r * CHUNK, (r + 1) * CHUNK)
        for g in range(A_GROUPS):
            gs = slice(g * gd, (g + 1) * gd)
            z = _dot(mixw_ref[g], vnb[rs, gs]) + mixb_ref[:, gs]
            ain_ref[rs, gs] = (u[rs, gs] * z).astype(ain_ref.dtype)

    cos = cos_ref[...]
    sin = sin_ref[...]
    c1 = c0 + q_lora
    c2 = c1 + kv_lora
    cq = _rms(_dot(n, wpre_ref[:, c0:c1]), qag_ref[...]).astype(BF16)
    nq = N_HEADS * HEAD_PAD
    qq = _dot(cq, wuq_ref[...])
    scale = QK_HEAD ** -0.5
    for h in range(N_HEADS):
        hs = slice(h * HEAD_PAD, (h + 1) * HEAD_PAD)
        qh = qq[:, hs] * cos + qq[:, nq + h * HEAD_PAD:nq + (h + 1) * HEAD_PAD] * sin
        q_ref[:, hs] = (_head_norm(qh, qg_ref[...]) * scale).astype(q_ref.dtype)

    c_kv = _rms(_dot(n, wpre_ref[:, c1:c2]), kvag_ref[...])
    ckv_ref[...] = c_kv
    krp = _dot(n, wpre_ref[:, c2:c2 + 2 * HEAD_PAD])
    kr_blk = krp[:, :HEAD_PAD] * cos + krp[:, HEAD_PAD:] * sin
    kr_ref[...] = kr_blk[:, QK_NOPE:QK_HEAD]

    if expand_kv:
        kv = _dot(c_kv.astype(BF16), wukv_ref[...])
        for h in range(N_HEADS):
            hs = slice(h * HEAD_PAD, (h + 1) * HEAD_PAD)
            k_ref[:, hs] = _head_norm(kv[:, hs] + kr_blk, kg_ref[...]).astype(k_ref.dtype)
        v_ref[...] = kv[:, nq:].astype(v_ref.dtype)


def _const_spec(shape):
    nd = len(shape)
    return pl.BlockSpec(shape, lambda i, _nd=nd: (0,) * _nd, pipeline_mode=pl.Buffered(1))


def _pre_call(x, cos_tab, sin_tab, w, expand_kv, write_vn, q_dtype):
    n_tok, d_model = x.shape
    tm = TOKEN_TILE
    n_tab = cos_tab.shape[0] // tm
    a_width = w['lng'].shape[1]
    kv_lora = w['kvag'].shape[1]
    d_ff = w['f1out'].shape[0]
    nq = N_HEADS * HEAD_PAD

    def row_spec(cols):
        return pl.BlockSpec((tm, cols), lambda i: (i, 0))

    tab_spec = pl.BlockSpec((tm, HEAD_PAD), lambda i: (i % n_tab, 0))
    consts = [w['f1g'], w['f1in'], w['f1out'], w['mixg'], w['wpre'], w['qag'], w['kvag'], w['wuq'],
              w['qg'], w['lng'], w['lnb'], w['mixw'], w['mixb']]
    if expand_kv:
        consts += [w['wukv'], w['kg']]
    in_specs = [row_spec(d_model), tab_spec, tab_spec] + [_const_spec(c.shape) for c in consts]
    out_shape = [jax.ShapeDtypeStruct((n_tok, d_model), F32),
                 jax.ShapeDtypeStruct((n_tok, a_width), BF16),
                 jax.ShapeDtypeStruct((n_tok, nq), q_dtype),
                 jax.ShapeDtypeStruct((n_tok, kv_lora), F32),
                 jax.ShapeDtypeStruct((n_tok, QK_ROPE), F32)]
    out_specs = [row_spec(d_model), row_spec(a_width), row_spec(nq), row_spec(kv_lora), row_spec(QK_ROPE)]
    if expand_kv:
        out_shape += [jax.ShapeDtypeStruct((n_tok, nq), BF16),
                      jax.ShapeDtypeStruct((n_tok, N_HEADS * V_DIM), BF16)]
        out_specs += [row_spec(nq), row_spec(N_HEADS * V_DIM)]
    if write_vn:
        out_shape.append(jax.ShapeDtypeStruct((n_tok, a_width), F32))
        out_specs.append(row_spec(a_width))
    return pl.pallas_call(
        functools.partial(_pre_kernel, expand_kv, write_vn),
        grid=(n_tok // tm,),
        in_specs=in_specs,
        out_specs=out_specs,
        out_shape=out_shape,
        scratch_shapes=[pltpu.VMEM((tm, d_ff), BF16)],
        compiler_params=pltpu.CompilerParams(dimension_semantics=("arbitrary",),
                                             vmem_limit_bytes=VMEM_LIMIT),
        name="pre_expand" if expand_kv else "pre_latent",
    )(x, cos_tab, sin_tab, *consts)


def _post_kernel(h1_ref, ain_ref, o_ref, p_ref, mixg_ref, wg_ref, wa_ref, wb_ref, wo_ref,
                 f2g_ref, f2in_ref, f2out_ref, pleg_ref, plewg_ref, plewp_ref, y_ref, act_ref):
    d_model = h1_ref.shape[1]
    h1 = h1_ref[...]
    n = _rms(h1, mixg_ref[...]).astype(BF16)
    ga = jax.nn.sigmoid(_dot(n, wg_ref[:, :d_model]))
    gb = jax.nn.sigmoid(_dot(n, wg_ref[:, d_model:]))
    m = ga * _dot(ain_ref[...], wa_ref[...]) + gb * _dot(o_ref[...], wb_ref[...])
    h2 = h1 + _dot(m.astype(BF16), wo_ref[...])
    h3 = _ffn_half(h2, f2g_ref, f2in_ref, f2out_ref, act_ref)
    hn = _rms(h3, pleg_ref[...]).astype(BF16)
    gate = jax.nn.sigmoid(_dot(hn, plewg_ref[...]))
    y_ref[...] = h3 + gate * _dot(p_ref[...].astype(BF16), plewp_ref[...])


def _post_call(h1, ain, o, p, w):
    n_tok, d_model = h1.shape
    tm = TOKEN_TILE
    d_ff = w['f2out'].shape[0]

    def row_spec(cols):
        return pl.BlockSpec((tm, cols), lambda i: (i, 0))

    consts = [w['mixg'], w['wgates'], w['wa'], w['wb'], w['wo'], w['f2g'], w['f2in'], w['f2out'],
              w['pleg'], w['plewg'], w['plewp']]
    return pl.pallas_call(
        _post_kernel,
        grid=(n_tok // tm,),
        in_specs=[row_spec(d_model), row_spec(ain.shape[1]), row_spec(o.shape[1]), row_spec(p.shape[1])]
        + [_const_spec(c.shape) for c in consts],
        out_specs=row_spec(d_model),
        out_shape=jax.ShapeDtypeStruct((n_tok, d_model), F32),
        scratch_shapes=[pltpu.VMEM((tm, d_ff), BF16)],
        compiler_params=pltpu.CompilerParams(dimension_semantics=("arbitrary",),
                                             vmem_limit_bytes=VMEM_LIMIT),
        name="post",
    )(h1, ain, o, p, *consts)


ATT_BQ = 256
ATT_BK = 256


def _prompt_attn_kernel(q_ref, k_ref, v_ref, o_ref):
    i = pl.program_id(2)
    bq = q_ref.shape[0]
    halves = []
    for hh in range(2):
        hs = slice(hh * HEAD_PAD, (hh + 1) * HEAD_PAD)
        q = q_ref[:, hs]

        def step(j, carry, masked, q=q, hs=hs):
            m, l, acc = carry
            start = pl.multiple_of(j * ATT_BK, ATT_BK)
            s = _dot_nt(q, k_ref[pl.ds(start, ATT_BK), hs])
            if masked:
                row = lax.broadcasted_iota(jnp.int32, s.shape, 0)
                col = lax.broadcasted_iota(jnp.int32, s.shape, 1)
                s = jnp.where(col <= row, s, -1e30)
            m_new = jnp.maximum(m, jnp.max(s, axis=-1, keepdims=True))
            alpha = jnp.exp(m - m_new)
            p = jnp.exp(s - m_new)
            l = alpha * l + jnp.sum(p, axis=-1, keepdims=True)
            acc = alpha * acc + _dot(p.astype(BF16), v_ref[pl.ds(start, ATT_BK), :])
            return m_new, l, acc

        init = (jnp.full((bq, 1), -1e30, F32), jnp.zeros((bq, 1), F32), jnp.zeros((bq, 2 * V_DIM), F32))
        carry = lax.fori_loop(0, i, functools.partial(step, masked=False), init)
        m, l, acc = step(i, carry, True)
        halves.append(acc / l)
    lane = lax.broadcasted_iota(jnp.int32, halves[0].shape, 1)
    o_ref[...] = jnp.where(lane < V_DIM, halves[0], halves[1]).astype(o_ref.dtype)


def _prompt_attn(q, k, v, batch, seq):
    assert ATT_BQ == ATT_BK
    nqb = seq // ATT_BQ
    n_tok = q.shape[0]
    return pl.pallas_call(
        _prompt_attn_kernel,
        grid=(batch, N_HEADS // 2, nqb),
        in_specs=[pl.BlockSpec((ATT_BQ, 2 * HEAD_PAD), lambda b, hp, i: (b * nqb + i, hp)),
                  pl.BlockSpec((seq, 2 * HEAD_PAD), lambda b, hp, i: (b, hp)),
                  pl.BlockSpec((seq, 2 * V_DIM), lambda b, hp, i: (b, hp))],
        out_specs=pl.BlockSpec((ATT_BQ, 2 * V_DIM), lambda b, hp, i: (b * nqb + i, hp)),
        out_shape=jax.ShapeDtypeStruct((n_tok, N_HEADS * V_DIM), BF16),
        compiler_params=pltpu.CompilerParams(dimension_semantics=("arbitrary",) * 3,
                                             vmem_limit_bytes=VMEM_LIMIT),
        name="prompt_attn",
    )(q, k, v)


PAGES_PER_CHUNK = 8


def _qabs_kernel(q_ref, gk_ref, wuk_ref, qlat_ref, qr_ref):
    for h in range(N_HEADS):
        qh = q_ref[:, h * HEAD_PAD:(h + 1) * HEAD_PAD] * gk_ref[...]
        qlat_ref[h] = _dot(qh[:, :QK_NOPE].astype(BF16), wuk_ref[h])
        qr_ref[h] = qh[:, QK_NOPE:QK_HEAD]


def _qabs_call(q, gk, wuk):
    n_tok = q.shape[0]
    kv_lora = wuk.shape[2]
    return pl.pallas_call(
        _qabs_kernel,
        out_shape=[jax.ShapeDtypeStruct((N_HEADS, n_tok, kv_lora), F32),
                   jax.ShapeDtypeStruct((N_HEADS, n_tok, QK_ROPE), F32)],
        compiler_params=pltpu.CompilerParams(vmem_limit_bytes=VMEM_LIMIT),
        name="q_absorb",
    )(q, gk, wuk)


def _sample_attn_kernel(pt_ref, qlat_ref, qr_ref, cnew_ref, krnew_ref, wukt_ref, ckv_hbm, kr_hbm,
                        olat_ref, cbuf, kbuf, csem, ksem, lhs_ref, m_ref, l_ref, acc_ref):
    s = pl.program_id(0)
    c = pl.program_id(1)
    n_seq = pl.num_programs(0)
    n_chunk = pl.num_programs(1)
    n_pages = n_chunk * PAGES_PER_CHUNK
    t_new = cnew_ref.shape[0]
    kv_lora = cnew_ref.shape[1]
    n_rows = N_HEADS * t_new
    n_kn = N_HEADS * QK_NOPE
    step = s * n_chunk + c
    slot = step % 2

    def copies(seq, chunk, slot_):
        out = []
        for p in range(PAGES_PER_CHUNK):
            page = pt_ref[seq * n_pages + chunk * PAGES_PER_CHUNK + p]
            rows = pl.ds(p * PAGE, PAGE)
            out.append(pltpu.make_async_copy(ckv_hbm.at[page], cbuf.at[slot_, rows, :], csem.at[slot_]))
            out.append(pltpu.make_async_copy(kr_hbm.at[page], kbuf.at[slot_, rows, :], ksem.at[slot_]))
        return out

    @pl.when(step == 0)
    def _():
        lhs_ref[:n_kn, :] = wukt_ref[...]
        for cp in copies(s, c, slot):
            cp.start()

    @pl.when(step + 1 < n_seq * n_chunk)
    def _():
        nxt = step + 1
        for cp in copies(nxt // n_chunk, nxt % n_chunk, 1 - slot):
            cp.start()

    @pl.when(c == 0)
    def _():
        lhs_ref[n_kn:, :] = qlat_ref[...].reshape(n_rows, kv_lora).astype(BF16)
        m_ref[...] = jnp.full(m_ref.shape, -1e30, F32)
        l_ref[...] = jnp.zeros(l_ref.shape, F32)
        acc_ref[...] = jnp.zeros(acc_ref.shape, F32)

    qr = qr_ref[...].reshape(n_rows, QK_ROPE).astype(BF16)
    ones = jnp.ones((8, QK_ROPE), BF16)

    def attend(cb, kr, mask):
        n = cb.shape[0]
        big = _dot_nt(lhs_ref[...], cb)
        kn_sq = big[:n_kn] * big[:n_kn]
        ss = jnp.sum(kn_sq.reshape(N_HEADS, QK_NOPE, n), axis=1)
        kr2 = kr * kr
        kr2_hi = kr2.astype(BF16)
        kr2_lo = (kr2 - kr2_hi.astype(F32)).astype(BF16)
        r2 = _dot_nt(ones, kr2_hi) + _dot_nt(ones, kr2_lo)
        rs = lax.rsqrt((ss + r2) * (1.0 / QK_HEAD) + EPS)
        sc = big[n_kn:] + _dot_nt(qr, kr.astype(BF16))
        sc = (sc.reshape(N_HEADS, t_new, n) * rs[:, None, :]).reshape(n_rows, n)
        if mask is not None:
            sc = jnp.where(mask, sc, -1e30)
        m_old = m_ref[...]
        m_new = jnp.maximum(m_old, jnp.max(sc, axis=-1, keepdims=True))
        alpha = jnp.exp(m_old - m_new)
        p = jnp.exp(sc - m_new)
        l_ref[...] = alpha * l_ref[...] + jnp.sum(p, axis=-1, keepdims=True)
        acc_ref[...] = alpha * acc_ref[...] + _dot(p.astype(BF16), cb)
        m_ref[...] = m_new

    for cp in copies(s, c, slot):
        cp.wait()
    attend(cbuf[slot].astype(BF16), kbuf[slot], None)

    @pl.when(c == n_chunk - 1)
    def _():
        pad = PAGE - t_new
        cb = jnp.concatenate([cnew_ref[...], jnp.zeros((pad, kv_lora), F32)], axis=0).astype(BF16)
        kr = jnp.concatenate([krnew_ref[...], jnp.zeros((pad, QK_ROPE), F32)], axis=0)
        row = lax.broadcasted_iota(jnp.int32, (n_rows, PAGE), 0)
        col = lax.broadcasted_iota(jnp.int32, (n_rows, PAGE), 1)
        attend(cb, kr, col <= row % t_new)
        olat_ref[...] = acc_ref[...] / l_ref[...]


def _sample_attn(page_table, qlat, qr, c_new, kr_new, wukt, cache_ckv, cache_kr):
    n_seq, n_pages = page_table.shape
    t_new = c_new.shape[1]
    kv_lora = c_new.shape[2]
    n_rows = N_HEADS * t_new
    n_chunk = n_pages // PAGES_PER_CHUNK
    ch = PAGES_PER_CHUNK * PAGE
    grid_spec = pltpu.PrefetchScalarGridSpec(
        num_scalar_prefetch=1,
        grid=(n_seq, n_chunk),
        in_specs=[pl.BlockSpec((N_HEADS, t_new, kv_lora), lambda s, c, pt: (0, s, 0)),
                  pl.BlockSpec((N_HEADS, t_new, QK_ROPE), lambda s, c, pt: (0, s, 0)),
                  pl.BlockSpec((None, t_new, kv_lora), lambda s, c, pt: (s, 0, 0)),
                  pl.BlockSpec((None, t_new, QK_ROPE), lambda s, c, pt: (s, 0, 0)),
                  pl.BlockSpec(wukt.shape, lambda s, c, pt: (0, 0)),
                  pl.BlockSpec(memory_space=pl.ANY),
                  pl.BlockSpec(memory_space=pl.ANY)],
        out_specs=pl.BlockSpec((None, n_rows, kv_lora), lambda s, c, pt: (s, 0, 0)),
        scratch_shapes=[pltpu.VMEM((2, ch, kv_lora), F32),
                        pltpu.VMEM((2, ch, QK_ROPE), F32),
                        pltpu.SemaphoreType.DMA((2,)),
                        pltpu.SemaphoreType.DMA((2,)),
                        pltpu.VMEM((N_HEADS * QK_NOPE + n_rows, kv_lora), BF16),
                        pltpu.VMEM((n_rows, 1), F32),
                        pltpu.VMEM((n_rows, 1), F32),
                        pltpu.VMEM((n_rows, kv_lora), F32)],
    )
    return pl.pallas_call(
        _sample_attn_kernel,
        grid_spec=grid_spec,
        out_shape=jax.ShapeDtypeStruct((n_seq, n_rows, kv_lora), F32),
        compiler_params=pltpu.CompilerParams(dimension_semantics=("arbitrary", "arbitrary"),
                                             vmem_limit_bytes=VMEM_LIMIT),
        name="sample_attn",
    )(page_table.reshape(-1), qlat, qr, c_new, kr_new, wukt, cache_ckv, cache_kr)


def _oproj_kernel(olat_ref, wuv_ref, o_ref):
    n_seq, n_rows, kv_lora = olat_ref.shape
    t_new = n_rows // N_HEADS
    acc = None
    for h in range(N_HEADS):
        x = olat_ref[:, h * t_new:(h + 1) * t_new, :].reshape(n_seq * t_new, kv_lora).astype(BF16)
        part = _dot(x, wuv_ref[h])
        acc = part if acc is None else acc + part
    o_ref[...] = acc.astype(o_ref.dtype)


def _oproj_call(olat, wuv):
    n_seq, n_rows, _ = olat.shape
    return pl.pallas_call(
        _oproj_kernel,
        out_shape=jax.ShapeDtypeStruct((n_seq * n_rows // N_HEADS, N_HEADS * V_DIM), BF16),
        compiler_params=pltpu.CompilerParams(vmem_limit_bytes=VMEM_LIMIT),
        name="o_proj",
    )(olat, wuv)


def _pad_heads(w, width):
    k = w.shape[0]
    w = w.reshape(k, N_HEADS, width)
    return jnp.pad(w, ((0, 0), (0, 0), (0, HEAD_PAD - width))).reshape(k, N_HEADS * HEAD_PAD)


def _rot_cols(w):
    half = w.shape[-1] // 2
    return jnp.concatenate([-w[..., half:], w[..., :half]], axis=-1)


def _rope_block(w):
    return jnp.pad(w, ((0, 0), (QK_NOPE, HEAD_PAD - QK_HEAD)))


def _head_gain(g):
    return jnp.pad(g, (0, HEAD_PAD - QK_HEAD)).reshape(1, HEAD_PAD)


def _rope_tables(n_pos, offset):
    pos = jnp.arange(n_pos, dtype=F32) + offset
    inv_freq = ROPE_THETA ** (-jnp.arange(0, QK_ROPE, 2, dtype=F32) / QK_ROPE)
    ang = pos[:, None] * inv_freq[None, :]
    cos, sin = jnp.cos(ang), jnp.sin(ang)
    ones = jnp.ones((n_pos, QK_NOPE), F32)
    tail = jnp.ones((n_pos, HEAD_PAD - QK_HEAD), F32)
    cos_tab = jnp.concatenate([ones, cos, cos, tail], axis=1)
    sin_tab = jnp.concatenate([0 * ones, sin, sin, 0 * tail], axis=1)
    return cos_tab, sin_tab


def _layer_weights(i, ffn1_norm, ffn1_w_in, ffn1_w_out, mix_norm, w_in, qa_norm, kva_norm, w_uq, w_ukv,
                   q_norm, k_norm, sgu_ln_g, sgu_ln_b, sgu_w_s, sgu_b_s, w_a_out, w_b_out, w_o,
                   ffn2_norm, ffn2_w_in, ffn2_w_out, ple_norm, ple_w_gate, ple_w_proj):
    a_width = sgu_ln_g.shape[1]
    q_lora = qa_norm.shape[1]
    kv_lora = kva_norm.shape[1]
    s1 = 2 * a_width
    s2 = s1 + q_lora
    s3 = s2 + kv_lora
    s4 = s3 + QK_ROPE
    wi = w_in[i]
    w_kr = wi[:, s3:s4]
    wpre = jnp.concatenate([wi[:, :s3], _rope_block(w_kr), _rope_block(_rot_cols(w_kr))], axis=1)

    wq = w_uq[i].reshape(q_lora, N_HEADS, QK_HEAD)
    wq_rot = jnp.concatenate([jnp.zeros_like(wq[..., :QK_NOPE]), _rot_cols(wq[..., QK_NOPE:])], axis=-1)
    wuq = jnp.concatenate([_pad_heads(wq.reshape(q_lora, -1), QK_HEAD),
                           _pad_heads(wq_rot.reshape(q_lora, -1), QK_HEAD)], axis=1)

    wkv = w_ukv[i].reshape(kv_lora, N_HEADS, QK_NOPE + V_DIM)
    w_uk = wkv[..., :QK_NOPE]
    w_uv = wkv[..., QK_NOPE:]
    wukv = jnp.concatenate([_pad_heads(w_uk.reshape(kv_lora, -1), QK_NOPE),
                            w_uv.reshape(kv_lora, -1)], axis=1)
    wuk_t = jnp.transpose(w_uk, (1, 2, 0))
    eye = jnp.eye(N_HEADS, dtype=F32)
    wuv_blk = jnp.einsum('khd,hg->hkgd', w_uv, eye).reshape(N_HEADS, kv_lora, N_HEADS * V_DIM)

    row = lambda g: g[i].reshape(1, -1)
    return {
        'f1g': row(ffn1_norm), 'f1in': ffn1_w_in[i].astype(BF16), 'f1out': ffn1_w_out[i].astype(BF16),
        'mixg': row(mix_norm), 'wpre': wpre.astype(BF16), 'wgates': wi[:, s4:].astype(BF16),
        'qag': row(qa_norm), 'kvag': row(kva_norm), 'wuq': wuq.astype(BF16), 'wukv': wukv.astype(BF16),
        'qg': _head_gain(q_norm[i]), 'kg': _head_gain(k_norm[i]),
        'wuk_t': wuk_t.astype(BF16), 'wuv_blk': wuv_blk.astype(BF16),
        'lng': row(sgu_ln_g), 'lnb': row(sgu_ln_b), 'w_s': sgu_w_s[i], 'b_s': sgu_b_s[i],
        'wa': w_a_out[i].astype(BF16), 'wb': w_b_out[i].astype(BF16), 'wo': w_o[i].astype(BF16),
        'f2g': row(ffn2_norm), 'f2in': ffn2_w_in[i].astype(BF16), 'f2out': ffn2_w_out[i].astype(BF16),
        'pleg': row(ple_norm), 'plewg': ple_w_gate[i].astype(BF16), 'plewp': ple_w_proj[i].astype(BF16),
    }


def _mix_tables(w_s, b_s, period):
    r = jnp.arange(CHUNK)
    same = (r[:, None] // period) == (r[None, :] // period)
    causal = (r[None, :] % period) <= (r[:, None] % period)
    t = r % period
    mixw = jnp.where((same & causal)[None], w_s[:, t[:, None], t[None, :]], 0.0)
    mixb = jnp.repeat(b_s[:, t].T, CHUNK, axis=1)
    return mixw.astype(BF16), mixb


def kernel(x_prompt, x_sample, cache_ckv, cache_krope, page_table, p_prompt, p_sample, ffn1_norm, ffn1_w_in, ffn1_w_out, mix_norm, w_in, qa_norm, kva_norm, w_uq, w_ukv, q_norm, k_norm, sgu_ln_g, sgu_ln_b, sgu_w_s, sgu_b_s, w_a_out, w_b_out, w_o, ffn2_norm, ffn2_w_in, ffn2_w_out, ple_norm, ple_w_gate, ple_w_proj):
    depth = w_in.shape[0]
    batch, seq, d_model = x_prompt.shape
    dec_batch, dec_seq, _ = x_sample.shape
    past_len = page_table.shape[1] * cache_ckv.shape[2]
    assert seq % TOKEN_TILE == 0 and TOKEN_TILE % dec_seq == 0 and CHUNK % dec_seq == 0
    assert cache_ckv.shape[2] == PAGE

    yp = x_prompt.reshape(batch * seq, d_model)
    ys = x_sample.reshape(dec_batch * dec_seq, d_model)
    cos_p, sin_p = _rope_tables(seq, 0)
    cos_s, sin_s = _rope_tables(dec_seq, past_len)
    reps = TOKEN_TILE // dec_seq
    cos_s, sin_s = jnp.tile(cos_s, (reps, 1)), jnp.tile(sin_s, (reps, 1))

    outs = [[] for _ in range(5)]
    for i in range(depth):
        w = _layer_weights(i, ffn1_norm, ffn1_w_in, ffn1_w_out, mix_norm, w_in, qa_norm, kva_norm, w_uq,
                           w_ukv, q_norm, k_norm, sgu_ln_g, sgu_ln_b, sgu_w_s, sgu_b_s, w_a_out, w_b_out,
                           w_o, ffn2_norm, ffn2_w_in, ffn2_w_out, ple_norm, ple_w_gate, ple_w_proj)

        w['mixw'], w['mixb'] = _mix_tables(w['w_s'], w['b_s'], CHUNK)
        h1, ain, q, ckv_p, kr_p, k, v = _pre_call(yp, cos_p, sin_p, w, True, False, BF16)
        o = _prompt_attn(q, k, v, batch, seq)
        yp = _post_call(h1, ain, o, p_prompt[i].reshape(batch * seq, -1), w)

        w['mixw'], w['mixb'] = _mix_tables(w['w_s'], w['b_s'], dec_seq)
        h1, ain, q, ckv_s, kr_s, vn_s = _pre_call(ys, cos_s, sin_s, w, False, True, F32)
        qlat, qr = _qabs_call(q, w['kg'], w['wuk_t'])
        kv_lora = ckv_s.shape[1]
        olat = _sample_attn(page_table, qlat, qr,
                            ckv_s.reshape(dec_batch, dec_seq, kv_lora),
                            kr_s.reshape(dec_batch, dec_seq, QK_ROPE),
                            w['wuk_t'].reshape(N_HEADS * QK_NOPE, kv_lora),
                            cache_ckv[i], cache_krope[i])
        o = _oproj_call(olat, w['wuv_blk'])
        ys = _post_call(h1, ain, o, p_sample[i].reshape(dec_batch * dec_seq, -1), w)

        outs[0].append(ckv_p.reshape(batch, seq, -1))
        outs[1].append(kr_p.reshape(batch, seq, -1))
        outs[2].append(ckv_s.reshape(dec_batch, dec_seq, -1))
        outs[3].append(kr_s.reshape(dec_batch, dec_seq, -1))
        outs[4].append(vn_s.reshape(dec_batch, dec_seq, -1))

    return (yp.reshape(batch, seq, d_model), ys.reshape(dec_batch, dec_seq, d_model),
            jnp.stack(outs[0]), jnp.stack(outs[1]), jnp.stack(outs[2]), jnp.stack(outs[3]),
            jnp.stack(outs[4]))
```

```python
import functools

import jax
import jax.numpy as jnp
from jax import lax
from jax.experimental import pallas as pl
from jax.experimental.pallas import tpu as pltpu

F32 = jnp.float32
BF16 = jnp.bfloat16

EPS = 1e-6
N_HEADS = 8
QK_NOPE = 64
QK_ROPE = 32
QK_HEAD = QK_NOPE + QK_ROPE
V_DIM = 64
HEAD_PAD = 128
A_GROUPS = 4
CHUNK = 128
ROPE_THETA = 10000.0
PAGE = 128
LOG2_E = 1.4426950408889634

FF_CHUNK = 256
TOKEN_TILE = 512
VMEM_LIMIT = 56 * 1024 * 1024

_NT = (((1,), (1,)), ((), ()))


def _dot(a, b):
    return jnp.dot(a, b, preferred_element_type=F32)


def _dot_nt(a, b):
    return lax.dot_general(a, b, _NT, preferred_element_type=F32)


def _rms(x, g):
    return x * lax.rsqrt(jnp.mean(x * x, axis=-1, keepdims=True) + EPS) * g


def _ffn_half(x, g_ref, win_ref, wout_ref, act_ref):
    d_ff = wout_ref.shape[0]
    xn = _rms(x, g_ref[...]).astype(BF16)
    for c in range(d_ff // FF_CHUNK):
        lo = c * FF_CHUNK
        a = _dot(xn, win_ref[:, lo:lo + FF_CHUNK])
        b = _dot(xn, win_ref[:, d_ff + lo:d_ff + lo + FF_CHUNK])
        act_ref[:, lo:lo + FF_CHUNK] = (a * jax.nn.sigmoid(a) * b).astype(BF16)
    return x + 0.5 * _dot(act_ref[...], wout_ref[...])


def _head_norm(xh, gain):
    ss = jnp.sum(xh * xh, axis=-1, keepdims=True)
    return xh * lax.rsqrt(ss * (1.0 / QK_HEAD) + EPS) * gain


def _pre_kernel(expand_kv, write_vn, *refs):
    (x_ref, cos_ref, sin_ref, f1g_ref, f1in_ref, f1out_ref, mixg_ref, wpre_ref, qag_ref, kvag_ref,
     wuq_ref, qg_ref, lng_ref, lnb_ref, mixw_ref, mixb_ref) = refs[:16]
    pos = 16
    if expand_kv:
        wukv_ref, kg_ref = refs[pos:pos + 2]
        pos += 2
    h1_ref, ain_ref, q_ref, ckv_ref, kr_ref = refs[pos:pos + 5]
    pos += 5
    if expand_kv:
        k_ref, v_ref = refs[pos:pos + 2]
        pos += 2
    if write_vn:
        vn_ref = refs[pos]
        pos += 1
    act_ref = refs[pos]

    a_width = lng_ref.shape[1]
    q_lora = qag_ref.shape[1]
    kv_lora = kvag_ref.shape[1]
    rows = x_ref.shape[0]

    h1 = _ffn_half(x_ref[...], f1g_ref, f1in_ref, f1out_ref, act_ref)
    h1_ref[...] = h1
    n = _rms(h1, mixg_ref[...]).astype(BF16)

    c0 = 2 * a_width
    uv = jax.nn.gelu(_dot(n, wpre_ref[:, :c0]))
    u, v = uv[:, :a_width], uv[:, a_width:]
    vc = v - jnp.mean(v, axis=-1, keepdims=True)
    vn = vc * lax.rsqrt(jnp.mean(vc * vc, axis=-1, keepdims=True) + EPS) * lng_ref[...] + lnb_ref[...]
    if write_vn:
        vn_ref[...] = vn
    vnb = vn.astype(BF16)
    gd = a_width // A_GROUPS
    for r in range(rows // CHUNK):
        rs = slice(r * CHUNK, (r + 1) * CHUNK)
        for g in range(A_GROUPS):
            gs = slice(g * gd, (g + 1) * gd)
            z = _dot(mixw_ref[g], vnb[rs, gs]) + mixb_ref[:, gs]
            ain_ref[rs, gs] = (u[rs, gs] * z).astype(ain_ref.dtype)

    cos = cos_ref[...]
    sin = sin_ref[...]
    c1 = c0 + q_lora
    c2 = c1 + kv_lora
    cq = _rms(_dot(n, wpre_ref[:, c0:c1]), qag_ref[...]).astype(BF16)
    nq = N_HEADS * HEAD_PAD
    qq = _dot(cq, wuq_ref[...])
    scale = QK_HEAD ** -0.5 * LOG2_E
    for h in range(N_HEADS):
        hs = slice(h * HEAD_PAD, (h + 1) * HEAD_PAD)
        qh = qq[:, hs] * cos + qq[:, nq + h * HEAD_PAD:nq + (h + 1) * HEAD_PAD] * sin
        q_ref[:, hs] = (_head_norm(qh, qg_ref[...]) * scale).astype(q_ref.dtype)

    c_kv = _rms(_dot(n, wpre_ref[:, c1:c2]), kvag_ref[...])
    ckv_ref[...] = c_kv
    krp = _dot(n, wpre_ref[:, c2:c2 + 2 * HEAD_PAD])
    kr_blk = krp[:, :HEAD_PAD] * cos + krp[:, HEAD_PAD:] * sin
    kr_ref[...] = kr_blk[:, QK_NOPE:QK_HEAD]

    if expand_kv:
        kv = _dot(c_kv.astype(BF16), wukv_ref[...])
        for h in range(N_HEADS):
            hs = slice(h * HEAD_PAD, (h + 1) * HEAD_PAD)
            k_ref[:, hs] = _head_norm(kv[:, hs] + kr_blk, kg_ref[...]).astype(k_ref.dtype)
        v_ref[...] = kv[:, nq:].astype(v_ref.dtype)


def _const_spec(shape):
    nd = len(shape)
    return pl.BlockSpec(shape, lambda i, _nd=nd: (0,) * _nd, pipeline_mode=pl.Buffered(1))


def _pre_call(x, cos_tab, sin_tab, w, expand_kv, write_vn, q_dtype):
    n_tok, d_model = x.shape
    tm = TOKEN_TILE
    n_tab = cos_tab.shape[0] // tm
    a_width = w['lng'].shape[1]
    kv_lora = w['kvag'].shape[1]
    d_ff = w['f1out'].shape[0]
    nq = N_HEADS * HEAD_PAD

    def row_spec(cols):
        return pl.BlockSpec((tm, cols), lambda i: (i, 0))

    tab_spec = pl.BlockSpec((tm, HEAD_PAD), lambda i: (i % n_tab, 0))
    consts = [w['f1g'], w['f1in'], w['f1out'], w['mixg'], w['wpre'], w['qag'], w['kvag'], w['wuq'],
              w['qg'], w['lng'], w['lnb'], w['mixw'], w['mixb']]
    if expand_kv:
        consts += [w['wukv'], w['kg']]
    in_specs = [row_spec(d_model), tab_spec, tab_spec] + [_const_spec(c.shape) for c in consts]
    out_shape = [jax.ShapeDtypeStruct((n_tok, d_model), F32),
                 jax.ShapeDtypeStruct((n_tok, a_width), BF16),
                 jax.ShapeDtypeStruct((n_tok, nq), q_dtype),
                 jax.ShapeDtypeStruct((n_tok, kv_lora), F32),
                 jax.ShapeDtypeStruct((n_tok, QK_ROPE), F32)]
    out_specs = [row_spec(d_model), row_spec(a_width), row_spec(nq), row_spec(kv_lora), row_spec(QK_ROPE)]
    if expand_kv:
        out_shape += [jax.ShapeDtypeStruct((n_tok, nq), BF16),
                      jax.ShapeDtypeStruct((n_tok, N_HEADS * V_DIM), BF16)]
        out_specs += [row_spec(nq), row_spec(N_HEADS * V_DIM)]
    if write_vn:
        out_shape.append(jax.ShapeDtypeStruct((n_tok, a_width), F32))
        out_specs.append(row_spec(a_width))
    return pl.pallas_call(
        functools.partial(_pre_kernel, expand_kv, write_vn),
        grid=(n_tok // tm,),
        in_specs=in_specs,
        out_specs=out_specs,
        out_shape=out_shape,
        scratch_shapes=[pltpu.VMEM((tm, d_ff), BF16)],
        compiler_params=pltpu.CompilerParams(dimension_semantics=("arbitrary",),
                                             vmem_limit_bytes=VMEM_LIMIT),
        name="pre_expand" if expand_kv else "pre_latent",
    )(x, cos_tab, sin_tab, *consts)


def _post_kernel(h1_ref, ain_ref, o_ref, p_ref, mixg_ref, wg_ref, wa_ref, wb_ref, wo_ref,
                 f2g_ref, f2in_ref, f2out_ref, pleg_ref, plewg_ref, plewp_ref, y_ref, act_ref):
    d_model = h1_ref.shape[1]
    h1 = h1_ref[...]
    n = _rms(h1, mixg_ref[...]).astype(BF16)
    ga = jax.nn.sigmoid(_dot(n, wg_ref[:, :d_model]))
    gb = jax.nn.sigmoid(_dot(n, wg_ref[:, d_model:]))
    m = ga * _dot(ain_ref[...], wa_ref[...]) + gb * _dot(o_ref[...], wb_ref[...])
    h2 = h1 + _dot(m.astype(BF16), wo_ref[...])
    h3 = _ffn_half(h2, f2g_ref, f2in_ref, f2out_ref, act_ref)
    hn = _rms(h3, pleg_ref[...]).astype(BF16)
    gate = jax.nn.sigmoid(_dot(hn, plewg_ref[...]))
    y_ref[...] = h3 + gate * _dot(p_ref[...].astype(BF16), plewp_ref[...])


def _post_call(h1, ain, o, p, w):
    n_tok, d_model = h1.shape
    tm = TOKEN_TILE
    d_ff = w['f2out'].shape[0]

    def row_spec(cols):
        return pl.BlockSpec((tm, cols), lambda i: (i, 0))

    consts = [w['mixg'], w['wgates'], w['wa'], w['wb'], w['wo'], w['f2g'], w['f2in'], w['f2out'],
              w['pleg'], w['plewg'], w['plewp']]
    return pl.pallas_call(
        _post_kernel,
        grid=(n_tok // tm,),
        in_specs=[row_spec(d_model), row_spec(ain.shape[1]), row_spec(o.shape[1]), row_spec(p.shape[1])]
        + [_const_spec(c.shape) for c in consts],
        out_specs=row_spec(d_model),
        out_shape=jax.ShapeDtypeStruct((n_tok, d_model), F32),
        scratch_shapes=[pltpu.VMEM((tm, d_ff), BF16)],
        compiler_params=pltpu.CompilerParams(dimension_semantics=("arbitrary",),
                                             vmem_limit_bytes=VMEM_LIMIT),
        name="post",
    )(h1, ain, o, p, *consts)


ATT_BQ = 512
ATT_BK = 256
ATT_VT_COLS = 512
ATT_ONES_ROWS = 16


def _prompt_attn_kernel(q_ref, k_ref, v_ref, o_ref, vt_ref, st_ref):
    i = pl.program_id(2)
    bq = q_ref.shape[0]
    seq = k_ref.shape[0]

    @pl.when(i == 0)
    def _():
        for r in range(seq // ATT_VT_COLS):
            rs = slice(r * ATT_VT_COLS, (r + 1) * ATT_VT_COLS)
            vt = v_ref[rs, :].astype(F32).T.astype(BF16)
            for hh in range(2):
                vt_ref[hh, :V_DIM, rs] = vt[hh * V_DIM:(hh + 1) * V_DIM]
                vt_ref[hh, V_DIM:, rs] = jnp.ones((ATT_ONES_ROWS, ATT_VT_COLS), BF16)

    qts = [q_ref[:, hh * HEAD_PAD:(hh + 1) * HEAD_PAD].astype(F32).T.astype(BF16) for hh in range(2)]

    def scores(j, slot):
        start = pl.multiple_of(j * ATT_BK, ATT_BK)
        for hh in range(2):
            st_ref[slot, hh] = _dot(k_ref[pl.ds(start, ATT_BK), hh * HEAD_PAD:(hh + 1) * HEAD_PAD], qts[hh])

    def update(j, slot, carry, masked):
        start = pl.multiple_of(j * ATT_BK, ATT_BK)
        out = []
        for hh in range(2):
            m, acc = carry[hh]
            st = st_ref[slot, hh]
            if masked:
                key = start + lax.broadcasted_iota(jnp.int32, st.shape, 0)
                qry = i * bq + lax.broadcasted_iota(jnp.int32, st.shape, 1)
                st = jnp.where(key <= qry, st, -1e30)
            m_new = jnp.maximum(m, jnp.max(st, axis=0, keepdims=True))
            p = jnp.exp2(st - m_new)
            acc = jnp.exp2(m - m_new) * acc + _dot(vt_ref[hh, :, pl.ds(start, ATT_BK)], p.astype(BF16))
            out.append((m_new, acc))
        return tuple(out)

    one = (jnp.full((1, bq), -1e30, F32), jnp.zeros((V_DIM + ATT_ONES_ROWS, bq), F32))

    def pair(t, carry):
        scores(2 * t + 1, 1)
        carry = update(2 * t, 0, carry, False)
        scores(2 * t + 2, 0)
        return update(2 * t + 1, 1, carry, False)

    scores(0, 0)
    carry = lax.fori_loop(0, i, pair, (one, one))
    scores(2 * i + 1, 1)
    carry = update(2 * i, 0, carry, True)
    carry = update(2 * i + 1, 1, carry, True)
    ot = jnp.concatenate([acc[:V_DIM] / acc[V_DIM:V_DIM + 1] for (_, acc) in carry], axis=0)
    o_ref[...] = ot.T.astype(o_ref.dtype)


def _prompt_attn(q, k, v, batch, seq):
    assert ATT_BQ == 2 * ATT_BK and seq % ATT_BQ == 0 and seq % ATT_VT_COLS == 0
    nqb = seq // ATT_BQ
    n_tok = q.shape[0]
    return pl.pallas_call(
        _prompt_attn_kernel,
        grid=(batch, N_HEADS // 2, nqb),
        in_specs=[pl.BlockSpec((ATT_BQ, 2 * HEAD_PAD), lambda b, hp, i: (b * nqb + i, hp)),
                  pl.BlockSpec((seq, 2 * HEAD_PAD), lambda b, hp, i: (b, hp)),
                  pl.BlockSpec((seq, 2 * V_DIM), lambda b, hp, i: (b, hp))],
        out_specs=pl.BlockSpec((ATT_BQ, 2 * V_DIM), lambda b, hp, i: (b * nqb + i, hp)),
        out_shape=jax.ShapeDtypeStruct((n_tok, N_HEADS * V_DIM), BF16),
        scratch_shapes=[pltpu.VMEM((2, V_DIM + ATT_ONES_ROWS, seq), BF16),
                        pltpu.VMEM((2, 2, ATT_BK, ATT_BQ), F32)],
        compiler_params=pltpu.CompilerParams(dimension_semantics=("arbitrary",) * 3,
                                             vmem_limit_bytes=VMEM_LIMIT),
        name="prompt_attn",
    )(q, k, v)


PAGES_PER_CHUNK = 8


def _qabs_kernel(q_ref, gk_ref, wuk_ref, qlat_ref, qr_ref):
    for h in range(N_HEADS):
        qh = q_ref[:, h * HEAD_PAD:(h + 1) * HEAD_PAD] * gk_ref[...]
        qlat_ref[h] = _dot(qh[:, :QK_NOPE].astype(BF16), wuk_ref[h])
        qr_ref[h] = qh[:, QK_NOPE:QK_HEAD]


def _qabs_call(q, gk, wuk):
    n_tok = q.shape[0]
    kv_lora = wuk.shape[2]
    return pl.pallas_call(
        _qabs_kernel,
        out_shape=[jax.ShapeDtypeStruct((N_HEADS, n_tok, kv_lora), F32),
                   jax.ShapeDtypeStruct((N_HEADS, n_tok, QK_ROPE), F32)],
        compiler_params=pltpu.CompilerParams(vmem_limit_bytes=VMEM_LIMIT),
        name="q_absorb",
    )(q, gk, wuk)


def _sample_attn_kernel(layer, pt_ref, qlat_ref, qr_ref, cnew_ref, krnew_ref, wukt_ref, ckv_hbm, kr_hbm,
                        olat_ref, cbuf, kbuf, csem, ksem, lhs_ref, m_ref, l_ref, acc_ref):
    s = pl.program_id(0)
    c = pl.program_id(1)
    n_seq = pl.num_programs(0)
    n_chunk = pl.num_programs(1)
    n_pages = n_chunk * PAGES_PER_CHUNK
    t_new = cnew_ref.shape[0]
    kv_lora = cnew_ref.shape[1]
    n_rows = N_HEADS * t_new
    n_kn = N_HEADS * QK_NOPE
    step = s * n_chunk + c
    slot = step % 2

    def copies(seq, chunk, slot_):
        out = []
        for p in range(PAGES_PER_CHUNK):
            page = pt_ref[seq * n_pages + chunk * PAGES_PER_CHUNK + p]
            rows = pl.ds(p * PAGE, PAGE)
            out.append(pltpu.make_async_copy(ckv_hbm.at[layer, page], cbuf.at[slot_, rows, :], csem.at[slot_]))
            out.append(pltpu.make_async_copy(kr_hbm.at[layer, page], kbuf.at[slot_, rows, :], ksem.at[slot_]))
        return out

    @pl.when(step == 0)
    def _():
        lhs_ref[:n_kn, :] = wukt_ref[...]
        for cp in copies(s, c, slot):
            cp.start()

    @pl.when(step + 1 < n_seq * n_chunk)
    def _():
        nxt = step + 1
        for cp in copies(nxt // n_chunk, nxt % n_chunk, 1 - slot):
            cp.start()

    @pl.when(c == 0)
    def _():
        lhs_ref[n_kn:, :] = qlat_ref[...].reshape(n_rows, kv_lora).astype(BF16)
        m_ref[...] = jnp.full(m_ref.shape, -1e30, F32)
        l_ref[...] = jnp.zeros(l_ref.shape, F32)
        acc_ref[...] = jnp.zeros(acc_ref.shape, F32)

    qr = qr_ref[...].reshape(n_rows, QK_ROPE).astype(BF16)
    ones = jnp.ones((8, QK_ROPE), BF16)

    def attend(cb, kr, mask):
        n = cb.shape[0]
        big = _dot_nt(lhs_ref[...], cb)
        kn_sq = big[:n_kn] * big[:n_kn]
        ss = jnp.sum(kn_sq.reshape(N_HEADS, QK_NOPE, n), axis=1)
        kr2 = kr * kr
        kr2_hi = kr2.astype(BF16)
        kr2_lo = (kr2 - kr2_hi.astype(F32)).astype(BF16)
        r2 = _dot_nt(ones, kr2_hi) + _dot_nt(ones, kr2_lo)
        rs = lax.rsqrt((ss + r2) * (1.0 / QK_HEAD) + EPS)
        sc = big[n_kn:] + _dot_nt(qr, kr.astype(BF16))
        sc = (sc.reshape(N_HEADS, t_new, n) * rs[:, None, :]).reshape(n_rows, n)
        if mask is not None:
            sc = jnp.where(mask, sc, -1e30)
        m_old = m_ref[...]
        m_new = jnp.maximum(m_old, jnp.max(sc, axis=-1, keepdims=True))
        alpha = jnp.exp2(m_old - m_new)
        p = jnp.exp2(sc - m_new)
        l_ref[...] = alpha * l_ref[...] + jnp.sum(p, axis=-1, keepdims=True)
        acc_ref[...] = alpha * acc_ref[...] + _dot(p.astype(BF16), cb)
        m_ref[...] = m_new

    for cp in copies(s, c, slot):
        cp.wait()
    attend(cbuf[slot].astype(BF16), kbuf[slot], None)

    @pl.when(c == n_chunk - 1)
    def _():
        pad = PAGE - t_new
        cb = jnp.concatenate([cnew_ref[...], jnp.zeros((pad, kv_lora), F32)], axis=0).astype(BF16)
        kr = jnp.concatenate([krnew_ref[...], jnp.zeros((pad, QK_ROPE), F32)], axis=0)
        row = lax.broadcasted_iota(jnp.int32, (n_rows, PAGE), 0)
        col = lax.broadcasted_iota(jnp.int32, (n_rows, PAGE), 1)
        attend(cb, kr, col <= row % t_new)
        olat_ref[...] = acc_ref[...] / l_ref[...]


def _sample_attn(layer, page_table, qlat, qr, c_new, kr_new, wukt, cache_ckv, cache_kr):
    n_seq, n_pages = page_table.shape
    t_new = c_new.shape[1]
    kv_lora = c_new.shape[2]
    n_rows = N_HEADS * t_new
    n_chunk = n_pages // PAGES_PER_CHUNK
    ch = PAGES_PER_CHUNK * PAGE
    grid_spec = pltpu.PrefetchScalarGridSpec(
        num_scalar_prefetch=1,
        grid=(n_seq, n_chunk),
        in_specs=[pl.BlockSpec((N_HEADS, t_new, kv_lora), lambda s, c, pt: (0, s, 0)),
                  pl.BlockSpec((N_HEADS, t_new, QK_ROPE), lambda s, c, pt: (0, s, 0)),
                  pl.BlockSpec((None, t_new, kv_lora), lambda s, c, pt: (s, 0, 0)),
                  pl.BlockSpec((None, t_new, QK_ROPE), lambda s, c, pt: (s, 0, 0)),
                  pl.BlockSpec(wukt.shape, lambda s, c, pt: (0, 0)),
                  pl.BlockSpec(memory_space=pl.ANY),
                  pl.BlockSpec(memory_space=pl.ANY)],
        out_specs=pl.BlockSpec((None, n_rows, kv_lora), lambda s, c, pt: (s, 0, 0)),
        scratch_shapes=[pltpu.VMEM((2, ch, kv_lora), F32),
                        pltpu.VMEM((2, ch, QK_ROPE), F32),
                        pltpu.SemaphoreType.DMA((2,)),
                        pltpu.SemaphoreType.DMA((2,)),
                        pltpu.VMEM((N_HEADS * QK_NOPE + n_rows, kv_lora), BF16),
                        pltpu.VMEM((n_rows, 1), F32),
                        pltpu.VMEM((n_rows, 1), F32),
                        pltpu.VMEM((n_rows, kv_lora), F32)],
    )
    return pl.pallas_call(
        functools.partial(_sample_attn_kernel, layer),
        grid_spec=grid_spec,
        out_shape=jax.ShapeDtypeStruct((n_seq, n_rows, kv_lora), F32),
        compiler_params=pltpu.CompilerParams(dimension_semantics=("arbitrary", "arbitrary"),
                                             vmem_limit_bytes=VMEM_LIMIT),
        name="sample_attn",
    )(page_table.reshape(-1), qlat, qr, c_new, kr_new, wukt, cache_ckv, cache_kr)


def _oproj_kernel(olat_ref, wuv_ref, o_ref):
    n_seq, n_rows, kv_lora = olat_ref.shape
    t_new = n_rows // N_HEADS
    acc = None
    for h in range(N_HEADS):
        x = olat_ref[:, h * t_new:(h + 1) * t_new, :].reshape(n_seq * t_new, kv_lora).astype(BF16)
        part = _dot(x, wuv_ref[h])
        acc = part if acc is None else acc + part
    o_ref[...] = acc.astype(o_ref.dtype)


def _oproj_call(olat, wuv):
    n_seq, n_rows, _ = olat.shape
    return pl.pallas_call(
        _oproj_kernel,
        out_shape=jax.ShapeDtypeStruct((n_seq * n_rows // N_HEADS, N_HEADS * V_DIM), BF16),
        compiler_params=pltpu.CompilerParams(vmem_limit_bytes=VMEM_LIMIT),
        name="o_proj",
    )(olat, wuv)


def _pad_heads(w, width):
    k = w.shape[0]
    w = w.reshape(k, N_HEADS, width)
    return jnp.pad(w, ((0, 0), (0, 0), (0, HEAD_PAD - width))).reshape(k, N_HEADS * HEAD_PAD)


def _rot_cols(w):
    half = w.shape[-1] // 2
    return jnp.concatenate([-w[..., half:], w[..., :half]], axis=-1)


def _rope_block(w):
    return jnp.pad(w, ((0, 0), (QK_NOPE, HEAD_PAD - QK_HEAD)))


def _head_gain(g):
    return jnp.pad(g, (0, HEAD_PAD - QK_HEAD)).reshape(1, HEAD_PAD)


def _rope_tables(n_pos, offset):
    pos = jnp.arange(n_pos, dtype=F32) + offset
    inv_freq = ROPE_THETA ** (-jnp.arange(0, QK_ROPE, 2, dtype=F32) / QK_ROPE)
    ang = pos[:, None] * inv_freq[None, :]
    cos, sin = jnp.cos(ang), jnp.sin(ang)
    ones = jnp.ones((n_pos, QK_NOPE), F32)
    tail = jnp.ones((n_pos, HEAD_PAD - QK_HEAD), F32)
    cos_tab = jnp.concatenate([ones, cos, cos, tail], axis=1)
    sin_tab = jnp.concatenate([0 * ones, sin, sin, 0 * tail], axis=1)
    return cos_tab, sin_tab


def _layer_weights(i, ffn1_norm, ffn1_w_in, ffn1_w_out, mix_norm, w_in, qa_norm, kva_norm, w_uq, w_ukv,
                   q_norm, k_norm, sgu_ln_g, sgu_ln_b, sgu_w_s, sgu_b_s, w_a_out, w_b_out, w_o,
                   ffn2_norm, ffn2_w_in, ffn2_w_out, ple_norm, ple_w_gate, ple_w_proj):
    a_width = sgu_ln_g.shape[1]
    q_lora = qa_norm.shape[1]
    kv_lora = kva_norm.shape[1]
    s1 = 2 * a_width
    s2 = s1 + q_lora
    s3 = s2 + kv_lora
    s4 = s3 + QK_ROPE
    wi = w_in[i]
    w_kr = wi[:, s3:s4]
    wpre = jnp.concatenate([wi[:, :s3], _rope_block(w_kr), _rope_block(_rot_cols(w_kr))], axis=1)

    wq = w_uq[i].reshape(q_lora, N_HEADS, QK_HEAD)
    wq_rot = jnp.concatenate([jnp.zeros_like(wq[..., :QK_NOPE]), _rot_cols(wq[..., QK_NOPE:])], axis=-1)
    wuq = jnp.concatenate([_pad_heads(wq.reshape(q_lora, -1), QK_HEAD),
                           _pad_heads(wq_rot.reshape(q_lora, -1), QK_HEAD)], axis=1)

    wkv = w_ukv[i].reshape(kv_lora, N_HEADS, QK_NOPE + V_DIM)
    w_uk = wkv[..., :QK_NOPE]
    w_uv = wkv[..., QK_NOPE:]
    wukv = jnp.concatenate([_pad_heads(w_uk.reshape(kv_lora, -1), QK_NOPE),
                            w_uv.reshape(kv_lora, -1)], axis=1)
    wuk_t = jnp.transpose(w_uk, (1, 2, 0))
    eye = jnp.eye(N_HEADS, dtype=F32)
    wuv_blk = jnp.einsum('khd,hg->hkgd', w_uv, eye).reshape(N_HEADS, kv_lora, N_HEADS * V_DIM)

    row = lambda g: g[i].reshape(1, -1)
    return {
        'f1g': row(ffn1_norm), 'f1in': ffn1_w_in[i].astype(BF16), 'f1out': ffn1_w_out[i].astype(BF16),
        'mixg': row(mix_norm), 'wpre': wpre.astype(BF16), 'wgates': wi[:, s4:].astype(BF16),
        'qag': row(qa_norm), 'kvag': row(kva_norm), 'wuq': wuq.astype(BF16), 'wukv': wukv.astype(BF16),
        'qg': _head_gain(q_norm[i]), 'kg': _head_gain(k_norm[i]),
        'wuk_t': wuk_t.astype(BF16), 'wuv_blk': wuv_blk.astype(BF16),
        'lng': row(sgu_ln_g), 'lnb': row(sgu_ln_b), 'w_s': sgu_w_s[i], 'b_s': sgu_b_s[i],
        'wa': w_a_out[i].astype(BF16), 'wb': w_b_out[i].astype(BF16), 'wo': w_o[i].astype(BF16),
        'f2g': row(ffn2_norm), 'f2in': ffn2_w_in[i].astype(BF16), 'f2out': ffn2_w_out[i].astype(BF16),
        'pleg': row(ple_norm), 'plewg': ple_w_gate[i].astype(BF16), 'plewp': ple_w_proj[i].astype(BF16),
    }


def _mix_tables(w_s, b_s, period, group_dim):
    r = jnp.arange(CHUNK)
    same = (r[:, None] // period) == (r[None, :] // period)
    causal = (r[None, :] % period) <= (r[:, None] % period)
    reps = CHUNK // period
    mixw = jnp.where((same & causal)[None], jnp.tile(w_s[:, :period, :period], (1, reps, reps)), 0.0)
    bias = jnp.tile(b_s[:, :period], (1, reps))
    mixb = jnp.repeat(bias.T, group_dim, axis=1)
    return mixw.astype(BF16), mixb


def kernel(x_prompt, x_sample, cache_ckv, cache_krope, page_table, p_prompt, p_sample, ffn1_norm, ffn1_w_in, ffn1_w_out, mix_norm, w_in, qa_norm, kva_norm, w_uq, w_ukv, q_norm, k_norm, sgu_ln_g, sgu_ln_b, sgu_w_s, sgu_b_s, w_a_out, w_b_out, w_o, ffn2_norm, ffn2_w_in, ffn2_w_out, ple_norm, ple_w_gate, ple_w_proj):
    depth = w_in.shape[0]
    batch, seq, d_model = x_prompt.shape
    dec_batch, dec_seq, _ = x_sample.shape
    past_len = page_table.shape[1] * cache_ckv.shape[2]
    assert seq % TOKEN_TILE == 0 and TOKEN_TILE % dec_seq == 0 and CHUNK % dec_seq == 0
    assert cache_ckv.shape[2] == PAGE

    yp = x_prompt.reshape(batch * seq, d_model)
    ys = x_sample.reshape(dec_batch * dec_seq, d_model)
    cos_p, sin_p = _rope_tables(seq, 0)
    cos_s, sin_s = _rope_tables(dec_seq, past_len)
    reps = TOKEN_TILE // dec_seq
    cos_s, sin_s = jnp.tile(cos_s, (reps, 1)), jnp.tile(sin_s, (reps, 1))

    outs = [[] for _ in range(5)]
    for i in range(depth):
        w = _layer_weights(i, ffn1_norm, ffn1_w_in, ffn1_w_out, mix_norm, w_in, qa_norm, kva_norm, w_uq,
                           w_ukv, q_norm, k_norm, sgu_ln_g, sgu_ln_b, sgu_w_s, sgu_b_s, w_a_out, w_b_out,
                           w_o, ffn2_norm, ffn2_w_in, ffn2_w_out, ple_norm, ple_w_gate, ple_w_proj)

        group_dim = w['lng'].shape[1] // A_GROUPS
        w['mixw'], w['mixb'] = _mix_tables(w['w_s'], w['b_s'], CHUNK, group_dim)
        h1, ain, q, ckv_p, kr_p, k, v = _pre_call(yp, cos_p, sin_p, w, True, False, BF16)
        o = _prompt_attn(q, k, v, batch, seq)
        yp = _post_call(h1, ain, o, p_prompt[i].reshape(batch * seq, -1), w)

        w['mixw'], w['mixb'] = _mix_tables(w['w_s'], w['b_s'], dec_seq, group_dim)
        h1, ain, q, ckv_s, kr_s, vn_s = _pre_call(ys, cos_s, sin_s, w, False, True, F32)
        qlat, qr = _qabs_call(q, w['kg'], w['wuk_t'])
        kv_lora = ckv_s.shape[1]
        olat = _sample_attn(i, page_table, qlat, qr,
                            ckv_s.reshape(dec_batch, dec_seq, kv_lora),
                            kr_s.reshape(dec_batch, dec_seq, QK_ROPE),
                            w['wuk_t'].reshape(N_HEADS * QK_NOPE, kv_lora),
                            cache_ckv, cache_krope)
        o = _oproj_call(olat, w['wuv_blk'])
        ys = _post_call(h1, ain, o, p_sample[i].reshape(dec_batch * dec_seq, -1), w)

        outs[0].append(ckv_p.reshape(batch, seq, -1))
        outs[1].append(kr_p.reshape(batch, seq, -1))
        outs[2].append(ckv_s.reshape(dec_batch, dec_seq, -1))
        outs[3].append(kr_s.reshape(dec_batch, dec_seq, -1))
        outs[4].append(vn_s.reshape(dec_batch, dec_seq, -1))

    return (yp.reshape(batch, seq, d_model), ys.reshape(dec_batch, dec_seq, d_model),
            jnp.stack(outs[0]), jnp.stack(outs[1]), jnp.stack(outs[2]), jnp.stack(outs[3]),
            jnp.stack(outs[4]))
```

```python
import functools

import jax
import jax.numpy as jnp
from jax import lax
from jax.experimental import pallas as pl
from jax.experimental.pallas import tpu as pltpu

F32 = jnp.float32
BF16 = jnp.bfloat16

EPS = 1e-6
N_HEADS = 8
QK_NOPE = 64
QK_ROPE = 32
QK_HEAD = QK_NOPE + QK_ROPE
V_DIM = 64
HEAD_PAD = 128
A_GROUPS = 4
CHUNK = 128
ROPE_THETA = 10000.0
PAGE = 128
LOG2_E = 1.4426950408889634

FF_CHUNK = 256
TOKEN_TILE = 512
VMEM_LIMIT = 56 * 1024 * 1024

_NT = (((1,), (1,)), ((), ()))


def _dot(a, b):
    return jnp.dot(a, b, preferred_element_type=F32)


def _dot_nt(a, b):
    return lax.dot_general(a, b, _NT, preferred_element_type=F32)


def _rms(x, g):
    return x * lax.rsqrt(jnp.mean(x * x, axis=-1, keepdims=True) + EPS) * g


def _ffn_half(x, g_ref, win_ref, wout_ref, act_ref):
    d_ff = wout_ref.shape[0]
    xn = _rms(x, g_ref[...]).astype(BF16)
    for c in range(d_ff // FF_CHUNK):
        lo = c * FF_CHUNK
        a = _dot(xn, win_ref[:, lo:lo + FF_CHUNK])
        b = _dot(xn, win_ref[:, d_ff + lo:d_ff + lo + FF_CHUNK])
        act_ref[:, lo:lo + FF_CHUNK] = (a * jax.nn.sigmoid(a) * b).astype(BF16)
    return x + 0.5 * _dot(act_ref[...], wout_ref[...])


def _head_norm(xh, gain):
    ss = jnp.sum(xh * xh, axis=-1, keepdims=True)
    return xh * lax.rsqrt(ss * (1.0 / QK_HEAD) + EPS) * gain


def _pre_kernel(expand_kv, write_vn, *refs):
    (x_ref, cos_ref, sin_ref, f1g_ref, f1in_ref, f1out_ref, mixg_ref, wpre_ref, qag_ref, kvag_ref,
     wuq_ref, qg_ref, lng_ref, lnb_ref, mixw_ref, mixb_ref) = refs[:16]
    pos = 16
    if expand_kv:
        wukv_ref, kg_ref = refs[pos:pos + 2]
        pos += 2
    h1_ref, ain_ref, q_ref, ckv_ref, kr_ref = refs[pos:pos + 5]
    pos += 5
    if expand_kv:
        k_ref, v_ref = refs[pos:pos + 2]
        pos += 2
    if write_vn:
        vn_ref = refs[pos]
        pos += 1
    act_ref = refs[pos]

    a_width = lng_ref.shape[1]
    q_lora = qag_ref.shape[1]
    kv_lora = kvag_ref.shape[1]
    rows = x_ref.shape[0]

    h1 = _ffn_half(x_ref[...], f1g_ref, f1in_ref, f1out_ref, act_ref)
    h1_ref[...] = h1
    n = _rms(h1, mixg_ref[...]).astype(BF16)

    c0 = 2 * a_width
    uv = jax.nn.gelu(_dot(n, wpre_ref[:, :c0]))
    u, v = uv[:, :a_width], uv[:, a_width:]
    vc = v - jnp.mean(v, axis=-1, keepdims=True)
    vn = vc * lax.rsqrt(jnp.mean(vc * vc, axis=-1, keepdims=True) + EPS) * lng_ref[...] + lnb_ref[...]
    if write_vn:
        vn_ref[...] = vn
    vnb = vn.astype(BF16)
    gd = a_width // A_GROUPS
    for r in range(rows // CHUNK):
        rs = slice(r * CHUNK, (r + 1) * CHUNK)
        for g in range(A_GROUPS):
            gs = slice(g * gd, (g + 1) * gd)
            z = _dot(mixw_ref[g], vnb[rs, gs]) + mixb_ref[:, gs]
            ain_ref[rs, gs] = (u[rs, gs] * z).astype(ain_ref.dtype)

    cos = cos_ref[...]
    sin = sin_ref[...]
    c1 = c0 + q_lora
    c2 = c1 + kv_lora
    cq = _rms(_dot(n, wpre_ref[:, c0:c1]), qag_ref[...]).astype(BF16)
    nq = N_HEADS * HEAD_PAD
    qq = _dot(cq, wuq_ref[...])
    scale = QK_HEAD ** -0.5 * LOG2_E
    for h in range(N_HEADS):
        hs = slice(h * HEAD_PAD, (h + 1) * HEAD_PAD)
        qh = qq[:, hs] * cos + qq[:, nq + h * HEAD_PAD:nq + (h + 1) * HEAD_PAD] * sin
        q_ref[:, hs] = (_head_norm(qh, qg_ref[...]) * scale).astype(q_ref.dtype)

    c_kv = _rms(_dot(n, wpre_ref[:, c1:c2]), kvag_ref[...])
    ckv_ref[...] = c_kv
    krp = _dot(n, wpre_ref[:, c2:c2 + 2 * HEAD_PAD])
    kr_blk = krp[:, :HEAD_PAD] * cos + krp[:, HEAD_PAD:] * sin
    kr_ref[...] = kr_blk[:, QK_NOPE:QK_HEAD]

    if expand_kv:
        kv = _dot(c_kv.astype(BF16), wukv_ref[...])
        for h in range(N_HEADS):
            hs = slice(h * HEAD_PAD, (h + 1) * HEAD_PAD)
            k_ref[:, hs] = _head_norm(kv[:, hs] + kr_blk, kg_ref[...]).astype(k_ref.dtype)
        v_ref[...] = kv[:, nq:].astype(v_ref.dtype)


def _const_spec(shape):
    nd = len(shape)
    return pl.BlockSpec(shape, lambda i, _nd=nd: (0,) * _nd, pipeline_mode=pl.Buffered(1))


def _pre_call(x, cos_tab, sin_tab, w, expand_kv, write_vn, q_dtype):
    n_tok, d_model = x.shape
    tm = TOKEN_TILE
    n_tab = cos_tab.shape[0] // tm
    a_width = w['lng'].shape[1]
    kv_lora = w['kvag'].shape[1]
    d_ff = w['f1out'].shape[0]
    nq = N_HEADS * HEAD_PAD

    def row_spec(cols):
        return pl.BlockSpec((tm, cols), lambda i: (i, 0))

    tab_spec = pl.BlockSpec((tm, HEAD_PAD), lambda i: (i % n_tab, 0))
    consts = [w['f1g'], w['f1in'], w['f1out'], w['mixg'], w['wpre'], w['qag'], w['kvag'], w['wuq'],
              w['qg'], w['lng'], w['lnb'], w['mixw'], w['mixb']]
    if expand_kv:
        consts += [w['wukv'], w['kg']]
    in_specs = [row_spec(d_model), tab_spec, tab_spec] + [_const_spec(c.shape) for c in consts]
    out_shape = [jax.ShapeDtypeStruct((n_tok, d_model), F32),
                 jax.ShapeDtypeStruct((n_tok, a_width), BF16),
                 jax.ShapeDtypeStruct((n_tok, nq), q_dtype),
                 jax.ShapeDtypeStruct((n_tok, kv_lora), F32),
                 jax.ShapeDtypeStruct((n_tok, QK_ROPE), F32)]
    out_specs = [row_spec(d_model), row_spec(a_width), row_spec(nq), row_spec(kv_lora), row_spec(QK_ROPE)]
    if expand_kv:
        out_shape += [jax.ShapeDtypeStruct((n_tok, nq), BF16),
                      jax.ShapeDtypeStruct((n_tok, N_HEADS * V_DIM), BF16)]
        out_specs += [row_spec(nq), row_spec(N_HEADS * V_DIM)]
    if write_vn:
        out_shape.append(jax.ShapeDtypeStruct((n_tok, a_width), F32))
        out_specs.append(row_spec(a_width))
    return pl.pallas_call(
        functools.partial(_pre_kernel, expand_kv, write_vn),
        grid=(n_tok // tm,),
        in_specs=in_specs,
        out_specs=out_specs,
        out_shape=out_shape,
        scratch_shapes=[pltpu.VMEM((tm, d_ff), BF16)],
        compiler_params=pltpu.CompilerParams(dimension_semantics=("arbitrary",),
                                             vmem_limit_bytes=VMEM_LIMIT),
        name="pre_expand" if expand_kv else "pre_latent",
    )(x, cos_tab, sin_tab, *consts)


def _post_kernel(h1_ref, ain_ref, o_ref, p_ref, mixg_ref, wg_ref, wa_ref, wb_ref, wo_ref,
                 f2g_ref, f2in_ref, f2out_ref, pleg_ref, plewg_ref, plewp_ref, y_ref, act_ref):
    d_model = h1_ref.shape[1]
    h1 = h1_ref[...]
    n = _rms(h1, mixg_ref[...]).astype(BF16)
    ga = jax.nn.sigmoid(_dot(n, wg_ref[:, :d_model]))
    gb = jax.nn.sigmoid(_dot(n, wg_ref[:, d_model:]))
    m = ga * _dot(ain_ref[...], wa_ref[...]) + gb * _dot(o_ref[...], wb_ref[...])
    h2 = h1 + _dot(m.astype(BF16), wo_ref[...])
    h3 = _ffn_half(h2, f2g_ref, f2in_ref, f2out_ref, act_ref)
    hn = _rms(h3, pleg_ref[...]).astype(BF16)
    gate = jax.nn.sigmoid(_dot(hn, plewg_ref[...]))
    y_ref[...] = h3 + gate * _dot(p_ref[...].astype(BF16), plewp_ref[...])


def _post_call(h1, ain, o, p, w):
    n_tok, d_model = h1.shape
    tm = TOKEN_TILE
    d_ff = w['f2out'].shape[0]

    def row_spec(cols):
        return pl.BlockSpec((tm, cols), lambda i: (i, 0))

    consts = [w['mixg'], w['wgates'], w['wa'], w['wb'], w['wo'], w['f2g'], w['f2in'], w['f2out'],
              w['pleg'], w['plewg'], w['plewp']]
    return pl.pallas_call(
        _post_kernel,
        grid=(n_tok // tm,),
        in_specs=[row_spec(d_model), row_spec(ain.shape[1]), row_spec(o.shape[1]), row_spec(p.shape[1])]
        + [_const_spec(c.shape) for c in consts],
        out_specs=row_spec(d_model),
        out_shape=jax.ShapeDtypeStruct((n_tok, d_model), F32),
        scratch_shapes=[pltpu.VMEM((tm, d_ff), BF16)],
        compiler_params=pltpu.CompilerParams(dimension_semantics=("arbitrary",),
                                             vmem_limit_bytes=VMEM_LIMIT),
        name="post",
    )(h1, ain, o, p, *consts)


ATT_BQ = 512
ATT_BK = 256
ATT_VT_COLS = 512
ATT_ONES_ROWS = 16


def _prompt_attn_kernel(q_ref, k_ref, v_ref, o_ref, vt_ref, st_ref):
    i = pl.program_id(2)
    bq = q_ref.shape[0]
    seq = k_ref.shape[0]

    @pl.when(i == 0)
    def _():
        for r in range(seq // ATT_VT_COLS):
            rs = slice(r * ATT_VT_COLS, (r + 1) * ATT_VT_COLS)
            vt = v_ref[rs, :].astype(F32).T.astype(BF16)
            for hh in range(2):
                vt_ref[hh, :V_DIM, rs] = vt[hh * V_DIM:(hh + 1) * V_DIM]
                vt_ref[hh, V_DIM:, rs] = jnp.ones((ATT_ONES_ROWS, ATT_VT_COLS), BF16)

    qts = [q_ref[:, hh * HEAD_PAD:(hh + 1) * HEAD_PAD].astype(F32).T.astype(BF16) for hh in range(2)]

    def scores(j, slot):
        start = pl.multiple_of(j * ATT_BK, ATT_BK)
        for hh in range(2):
            st_ref[slot, hh] = _dot(k_ref[pl.ds(start, ATT_BK), hh * HEAD_PAD:(hh + 1) * HEAD_PAD], qts[hh])

    def update(j, slot, carry, masked):
        start = pl.multiple_of(j * ATT_BK, ATT_BK)
        out = []
        for hh in range(2):
            m, acc = carry[hh]
            st = st_ref[slot, hh]
            if masked:
                key = start + lax.broadcasted_iota(jnp.int32, st.shape, 0)
                qry = i * bq + lax.broadcasted_iota(jnp.int32, st.shape, 1)
                st = jnp.where(key <= qry, st, -1e30)
            m_new = jnp.maximum(m, jnp.max(st, axis=0, keepdims=True))
            p = jnp.exp2(st - m_new)
            acc = jnp.exp2(m - m_new) * acc + _dot(vt_ref[hh, :, pl.ds(start, ATT_BK)], p.astype(BF16))
            out.append((m_new, acc))
        return tuple(out)

    one = (jnp.full((1, bq), -1e30, F32), jnp.zeros((V_DIM + ATT_ONES_ROWS, bq), F32))

    def pair(t, carry):
        scores(2 * t + 1, 1)
        carry = update(2 * t, 0, carry, False)
        scores(2 * t + 2, 0)
        return update(2 * t + 1, 1, carry, False)

    scores(0, 0)
    carry = lax.fori_loop(0, i, pair, (one, one))
    scores(2 * i + 1, 1)
    carry = update(2 * i, 0, carry, True)
    carry = update(2 * i + 1, 1, carry, True)
    ot = jnp.concatenate([acc[:V_DIM] / acc[V_DIM:V_DIM + 1] for (_, acc) in carry], axis=0)
    o_ref[...] = ot.T.astype(o_ref.dtype)


def _prompt_attn(q, k, v, batch, seq):
    assert ATT_BQ == 2 * ATT_BK and seq % ATT_BQ == 0 and seq % ATT_VT_COLS == 0
    nqb = seq // ATT_BQ
    n_tok = q.shape[0]
    return pl.pallas_call(
        _prompt_attn_kernel,
        grid=(batch, N_HEADS // 2, nqb),
        in_specs=[pl.BlockSpec((ATT_BQ, 2 * HEAD_PAD), lambda b, hp, i: (b * nqb + i, hp)),
                  pl.BlockSpec((seq, 2 * HEAD_PAD), lambda b, hp, i: (b, hp)),
                  pl.BlockSpec((seq, 2 * V_DIM), lambda b, hp, i: (b, hp))],
        out_specs=pl.BlockSpec((ATT_BQ, 2 * V_DIM), lambda b, hp, i: (b * nqb + i, hp)),
        out_shape=jax.ShapeDtypeStruct((n_tok, N_HEADS * V_DIM), BF16),
        scratch_shapes=[pltpu.VMEM((2, V_DIM + ATT_ONES_ROWS, seq), BF16),
                        pltpu.VMEM((2, 2, ATT_BK, ATT_BQ), F32)],
        compiler_params=pltpu.CompilerParams(dimension_semantics=("arbitrary",) * 3,
                                             vmem_limit_bytes=VMEM_LIMIT),
        name="prompt_attn",
    )(q, k, v)


SAMPLE_SUB = 1024


def _qabs_kernel(q_ref, gk_ref, wuk_ref, qlat_ref, qr_ref):
    for h in range(N_HEADS):
        qh = q_ref[:, h * HEAD_PAD:(h + 1) * HEAD_PAD] * gk_ref[...]
        qlat_ref[h] = _dot(qh[:, :QK_NOPE].astype(BF16), wuk_ref[h])
        qr_ref[h] = qh[:, QK_NOPE:QK_HEAD]


def _qabs_call(q, gk, wuk):
    n_tok = q.shape[0]
    kv_lora = wuk.shape[2]
    return pl.pallas_call(
        _qabs_kernel,
        out_shape=[jax.ShapeDtypeStruct((N_HEADS, n_tok, kv_lora), F32),
                   jax.ShapeDtypeStruct((N_HEADS, n_tok, QK_ROPE), F32)],
        compiler_params=pltpu.CompilerParams(vmem_limit_bytes=VMEM_LIMIT),
        name="q_absorb",
    )(q, gk, wuk)


def _sample_attn_kernel(layer, pt_ref, qlat_ref, qr_ref, cnew_ref, krnew_ref, wukt_ref, ckv_hbm, krt_hbm,
                        olat_ref, cbuf, kbuf, csem, ksem, lhs_ref, cb_ref):
    s = pl.program_id(0)
    n_seq = pl.num_programs(0)
    past = cbuf.shape[1]
    n_pages = past // PAGE
    t_new = cnew_ref.shape[0]
    kv_lora = cnew_ref.shape[1]
    n_rows = N_HEADS * t_new
    n_kn = N_HEADS * QK_NOPE
    slot = s % 2

    def copies(seq, slot_):
        out = []
        for p in range(n_pages):
            page = pt_ref[seq * n_pages + p]
            keys = pl.ds(p * PAGE, PAGE)
            out.append(pltpu.make_async_copy(ckv_hbm.at[layer, page], cbuf.at[slot_, keys, :], csem.at[slot_]))
            out.append(pltpu.make_async_copy(krt_hbm.at[layer, page], kbuf.at[slot_, :, keys], ksem.at[slot_]))
        return out

    @pl.when(s == 0)
    def _():
        lhs_ref[:n_kn, :] = wukt_ref[...]
        for cp in copies(s, slot):
            cp.start()

    for cp in copies(s, slot):
        cp.wait()

    lhs_ref[n_kn:, :] = qlat_ref[...].reshape(n_rows, kv_lora).astype(BF16)
    qr = qr_ref[...].reshape(n_rows, QK_ROPE).astype(BF16)

    def scaled_scores(big, rope_scores, r2):
        n = big.shape[1]
        kn = big[:n_kn]
        ss = jnp.sum((kn * kn).reshape(N_HEADS, QK_NOPE, n), axis=1)
        rs = lax.rsqrt((ss + r2) * (1.0 / QK_HEAD) + EPS)
        sc = (big[n_kn:] + rope_scores).reshape(N_HEADS, t_new, n) * rs[:, None, :]
        return sc.reshape(n_rows, n)

    def past_scores(k):
        keys = slice(k * SAMPLE_SUB, (k + 1) * SAMPLE_SUB)
        cb = cbuf[slot, keys, :].astype(BF16)
        cb_ref[k % 2] = cb
        krt = kbuf[slot, :, keys]
        r2 = jnp.sum(krt * krt, axis=0, keepdims=True)
        return scaled_scores(_dot_nt(lhs_ref[...], cb), _dot(qr, krt.astype(BF16)), r2)

    pad = PAGE - t_new
    cb_new = jnp.concatenate([cnew_ref[...], jnp.zeros((pad, kv_lora), F32)], axis=0).astype(BF16)
    kr_new = jnp.concatenate([krnew_ref[...], jnp.zeros((pad, QK_ROPE), F32)], axis=0)
    kr2 = kr_new * kr_new
    kr2_hi = kr2.astype(BF16)
    kr2_lo = (kr2 - kr2_hi.astype(F32)).astype(BF16)
    ones = jnp.ones((8, QK_ROPE), BF16)
    r2_new = (_dot_nt(ones, kr2_hi) + _dot_nt(ones, kr2_lo))[:1]
    sc_new = scaled_scores(_dot_nt(lhs_ref[...], cb_new), _dot_nt(qr, kr_new.astype(BF16)), r2_new)
    row = lax.broadcasted_iota(jnp.int32, (n_rows, PAGE), 0)
    col = lax.broadcasted_iota(jnp.int32, (n_rows, PAGE), 1)
    sc_new = jnp.where(col <= row % t_new, sc_new, -1e30)

    n_sub = past // SAMPLE_SUB
    sc_next = past_scores(0)

    nxt = jnp.minimum(s + 1, n_seq - 1)
    for cp in copies(nxt, 1 - slot):
        cp.start()

    m = jnp.max(sc_new, axis=-1, keepdims=True)
    p = jnp.exp2(sc_new - m)
    l = jnp.sum(p, axis=-1, keepdims=True)
    acc = _dot(p.astype(BF16), cb_new)
    for k in range(n_sub):
        sc = sc_next
        if k + 1 < n_sub:
            sc_next = past_scores(k + 1)
        m_new = jnp.maximum(m, jnp.max(sc, axis=-1, keepdims=True))
        alpha = jnp.exp2(m - m_new)
        p = jnp.exp2(sc - m_new)
        l = alpha * l + jnp.sum(p, axis=-1, keepdims=True)
        acc = alpha * acc + _dot(p.astype(BF16), cb_ref[k % 2])
        m = m_new
    olat_ref[...] = acc / l

    @pl.when(s == n_seq - 1)
    def _():
        for cp in copies(nxt, 1 - slot):
            cp.wait()


def _sample_attn(layer, page_table, qlat, qr, c_new, kr_new, wukt, cache_ckv, cache_krt):
    n_seq, n_pages = page_table.shape
    t_new = c_new.shape[1]
    kv_lora = c_new.shape[2]
    n_rows = N_HEADS * t_new
    past = n_pages * PAGE
    assert past % SAMPLE_SUB == 0
    grid_spec = pltpu.PrefetchScalarGridSpec(
        num_scalar_prefetch=1,
        grid=(n_seq,),
        in_specs=[pl.BlockSpec((N_HEADS, t_new, kv_lora), lambda s, pt: (0, s, 0)),
                  pl.BlockSpec((N_HEADS, t_new, QK_ROPE), lambda s, pt: (0, s, 0)),
                  pl.BlockSpec((None, t_new, kv_lora), lambda s, pt: (s, 0, 0)),
                  pl.BlockSpec((None, t_new, QK_ROPE), lambda s, pt: (s, 0, 0)),
                  pl.BlockSpec(wukt.shape, lambda s, pt: (0, 0)),
                  pl.BlockSpec(memory_space=pl.ANY),
                  pl.BlockSpec(memory_space=pl.ANY)],
        out_specs=pl.BlockSpec((None, n_rows, kv_lora), lambda s, pt: (s, 0, 0)),
        scratch_shapes=[pltpu.VMEM((2, past, kv_lora), F32),
                        pltpu.VMEM((2, QK_ROPE, past), F32),
                        pltpu.SemaphoreType.DMA((2,)),
                        pltpu.SemaphoreType.DMA((2,)),
                        pltpu.VMEM((N_HEADS * QK_NOPE + n_rows, kv_lora), BF16),
                        pltpu.VMEM((2, SAMPLE_SUB, kv_lora), BF16)],
    )
    return pl.pallas_call(
        functools.partial(_sample_attn_kernel, layer),
        grid_spec=grid_spec,
        out_shape=jax.ShapeDtypeStruct((n_seq, n_rows, kv_lora), F32),
        compiler_params=pltpu.CompilerParams(dimension_semantics=("arbitrary",),
                                             vmem_limit_bytes=VMEM_LIMIT),
        name="sample_attn",
    )(page_table.reshape(-1), qlat, qr, c_new, kr_new, wukt, cache_ckv, cache_krt)


def _oproj_kernel(olat_ref, wuv_ref, o_ref):
    n_seq, n_rows, kv_lora = olat_ref.shape
    t_new = n_rows // N_HEADS
    acc = None
    for h in range(N_HEADS):
        x = olat_ref[:, h * t_new:(h + 1) * t_new, :].reshape(n_seq * t_new, kv_lora).astype(BF16)
        part = _dot(x, wuv_ref[h])
        acc = part if acc is None else acc + part
    o_ref[...] = acc.astype(o_ref.dtype)


def _oproj_call(olat, wuv):
    n_seq, n_rows, _ = olat.shape
    return pl.pallas_call(
        _oproj_kernel,
        out_shape=jax.ShapeDtypeStruct((n_seq * n_rows // N_HEADS, N_HEADS * V_DIM), BF16),
        compiler_params=pltpu.CompilerParams(vmem_limit_bytes=VMEM_LIMIT),
        name="o_proj",
    )(olat, wuv)


def _pad_heads(w, width):
    k = w.shape[0]
    w = w.reshape(k, N_HEADS, width)
    return jnp.pad(w, ((0, 0), (0, 0), (0, HEAD_PAD - width))).reshape(k, N_HEADS * HEAD_PAD)


def _rot_cols(w):
    half = w.shape[-1] // 2
    return jnp.concatenate([-w[..., half:], w[..., :half]], axis=-1)


def _rope_block(w):
    return jnp.pad(w, ((0, 0), (QK_NOPE, HEAD_PAD - QK_HEAD)))


def _head_gain(g):
    return jnp.pad(g, (0, HEAD_PAD - QK_HEAD)).reshape(1, HEAD_PAD)


def _rope_tables(n_pos, offset):
    pos = jnp.arange(n_pos, dtype=F32) + offset
    inv_freq = ROPE_THETA ** (-jnp.arange(0, QK_ROPE, 2, dtype=F32) / QK_ROPE)
    ang = pos[:, None] * inv_freq[None, :]
    cos, sin = jnp.cos(ang), jnp.sin(ang)
    ones = jnp.ones((n_pos, QK_NOPE), F32)
    tail = jnp.ones((n_pos, HEAD_PAD - QK_HEAD), F32)
    cos_tab = jnp.concatenate([ones, cos, cos, tail], axis=1)
    sin_tab = jnp.concatenate([0 * ones, sin, sin, 0 * tail], axis=1)
    return cos_tab, sin_tab


def _layer_weights(i, ffn1_norm, ffn1_w_in, ffn1_w_out, mix_norm, w_in, qa_norm, kva_norm, w_uq, w_ukv,
                   q_norm, k_norm, sgu_ln_g, sgu_ln_b, sgu_w_s, sgu_b_s, w_a_out, w_b_out, w_o,
                   ffn2_norm, ffn2_w_in, ffn2_w_out, ple_norm, ple_w_gate, ple_w_proj):
    a_width = sgu_ln_g.shape[1]
    q_lora = qa_norm.shape[1]
    kv_lora = kva_norm.shape[1]
    s1 = 2 * a_width
    s2 = s1 + q_lora
    s3 = s2 + kv_lora
    s4 = s3 + QK_ROPE
    wi = w_in[i]
    w_kr = wi[:, s3:s4]
    wpre = jnp.concatenate([wi[:, :s3], _rope_block(w_kr), _rope_block(_rot_cols(w_kr))], axis=1)

    wq = w_uq[i].reshape(q_lora, N_HEADS, QK_HEAD)
    wq_rot = jnp.concatenate([jnp.zeros_like(wq[..., :QK_NOPE]), _rot_cols(wq[..., QK_NOPE:])], axis=-1)
    wuq = jnp.concatenate([_pad_heads(wq.reshape(q_lora, -1), QK_HEAD),
                           _pad_heads(wq_rot.reshape(q_lora, -1), QK_HEAD)], axis=1)

    wkv = w_ukv[i].reshape(kv_lora, N_HEADS, QK_NOPE + V_DIM)
    w_uk = wkv[..., :QK_NOPE]
    w_uv = wkv[..., QK_NOPE:]
    wukv = jnp.concatenate([_pad_heads(w_uk.reshape(kv_lora, -1), QK_NOPE),
                            w_uv.reshape(kv_lora, -1)], axis=1)
    wuk_t = jnp.transpose(w_uk, (1, 2, 0))
    eye = jnp.eye(N_HEADS, dtype=F32)
    wuv_blk = jnp.einsum('khd,hg->hkgd', w_uv, eye).reshape(N_HEADS, kv_lora, N_HEADS * V_DIM)

    row = lambda g: g[i].reshape(1, -1)
    return {
        'f1g': row(ffn1_norm), 'f1in': ffn1_w_in[i].astype(BF16), 'f1out': ffn1_w_out[i].astype(BF16),
        'mixg': row(mix_norm), 'wpre': wpre.astype(BF16), 'wgates': wi[:, s4:].astype(BF16),
        'qag': row(qa_norm), 'kvag': row(kva_norm), 'wuq': wuq.astype(BF16), 'wukv': wukv.astype(BF16),
        'qg': _head_gain(q_norm[i]), 'kg': _head_gain(k_norm[i]),
        'wuk_t': wuk_t.astype(BF16), 'wuv_blk': wuv_blk.astype(BF16),
        'lng': row(sgu_ln_g), 'lnb': row(sgu_ln_b), 'w_s': sgu_w_s[i], 'b_s': sgu_b_s[i],
        'wa': w_a_out[i].astype(BF16), 'wb': w_b_out[i].astype(BF16), 'wo': w_o[i].astype(BF16),
        'f2g': row(ffn2_norm), 'f2in': ffn2_w_in[i].astype(BF16), 'f2out': ffn2_w_out[i].astype(BF16),
        'pleg': row(ple_norm), 'plewg': ple_w_gate[i].astype(BF16), 'plewp': ple_w_proj[i].astype(BF16),
    }


def _mix_tables(w_s, b_s, period, group_dim):
    r = jnp.arange(CHUNK)
    same = (r[:, None] // period) == (r[None, :] // period)
    causal = (r[None, :] % period) <= (r[:, None] % period)
    reps = CHUNK // period
    mixw = jnp.where((same & causal)[None], jnp.tile(w_s[:, :period, :period], (1, reps, reps)), 0.0)
    bias = jnp.tile(b_s[:, :period], (1, reps))
    mixb = jnp.repeat(bias.T, group_dim, axis=1)
    return mixw.astype(BF16), mixb


def kernel(x_prompt, x_sample, cache_ckv, cache_krope, page_table, p_prompt, p_sample, ffn1_norm, ffn1_w_in, ffn1_w_out, mix_norm, w_in, qa_norm, kva_norm, w_uq, w_ukv, q_norm, k_norm, sgu_ln_g, sgu_ln_b, sgu_w_s, sgu_b_s, w_a_out, w_b_out, w_o, ffn2_norm, ffn2_w_in, ffn2_w_out, ple_norm, ple_w_gate, ple_w_proj):
    depth = w_in.shape[0]
    batch, seq, d_model = x_prompt.shape
    dec_batch, dec_seq, _ = x_sample.shape
    past_len = page_table.shape[1] * cache_ckv.shape[2]
    assert seq % TOKEN_TILE == 0 and TOKEN_TILE % dec_seq == 0 and CHUNK % dec_seq == 0
    assert cache_ckv.shape[2] == PAGE

    yp = x_prompt.reshape(batch * seq, d_model)
    ys = x_sample.reshape(dec_batch * dec_seq, d_model)
    cos_p, sin_p = _rope_tables(seq, 0)
    cos_s, sin_s = _rope_tables(dec_seq, past_len)
    reps = TOKEN_TILE // dec_seq
    cos_s, sin_s = jnp.tile(cos_s, (reps, 1)), jnp.tile(sin_s, (reps, 1))

    outs = [[] for _ in range(5)]
    for i in range(depth):
        w = _layer_weights(i, ffn1_norm, ffn1_w_in, ffn1_w_out, mix_norm, w_in, qa_norm, kva_norm, w_uq,
                           w_ukv, q_norm, k_norm, sgu_ln_g, sgu_ln_b, sgu_w_s, sgu_b_s, w_a_out, w_b_out,
                           w_o, ffn2_norm, ffn2_w_in, ffn2_w_out, ple_norm, ple_w_gate, ple_w_proj)

        group_dim = w['lng'].shape[1] // A_GROUPS
        w['mixw'], w['mixb'] = _mix_tables(w['w_s'], w['b_s'], CHUNK, group_dim)
        h1, ain, q, ckv_p, kr_p, k, v = _pre_call(yp, cos_p, sin_p, w, True, False, BF16)
        o = _prompt_attn(q, k, v, batch, seq)
        yp = _post_call(h1, ain, o, p_prompt[i].reshape(batch * seq, -1), w)

        w['mixw'], w['mixb'] = _mix_tables(w['w_s'], w['b_s'], dec_seq, group_dim)
        h1, ain, q, ckv_s, kr_s, vn_s = _pre_call(ys, cos_s, sin_s, w, False, True, F32)
        qlat, qr = _qabs_call(q, w['kg'], w['wuk_t'])
        kv_lora = ckv_s.shape[1]
        olat = _sample_attn(i, page_table, qlat, qr,
                            ckv_s.reshape(dec_batch, dec_seq, kv_lora),
                            kr_s.reshape(dec_batch, dec_seq, QK_ROPE),
                            w['wuk_t'].reshape(N_HEADS * QK_NOPE, kv_lora),
                            cache_ckv, jnp.swapaxes(cache_krope, 2, 3))
        o = _oproj_call(olat, w['wuv_blk'])
        ys = _post_call(h1, ain, o, p_sample[i].reshape(dec_batch * dec_seq, -1), w)

        outs[0].append(ckv_p.reshape(batch, seq, -1))
        outs[1].append(kr_p.reshape(batch, seq, -1))
        outs[2].append(ckv_s.reshape(dec_batch, dec_seq, -1))
        outs[3].append(kr_s.reshape(dec_batch, dec_seq, -1))
        outs[4].append(vn_s.reshape(dec_batch, dec_seq, -1))

    return (yp.reshape(batch, seq, d_model), ys.reshape(dec_batch, dec_seq, d_model),
            jnp.stack(outs[0]), jnp.stack(outs[1]), jnp.stack(outs[2]), jnp.stack(outs[3]),
            jnp.stack(outs[4]))
```

```python
import functools

import jax
import jax.numpy as jnp
from jax import lax
from jax.experimental import pallas as pl
from jax.experimental.pallas import tpu as pltpu

F32 = jnp.float32
BF16 = jnp.bfloat16

EPS = 1e-6
N_HEADS = 8
QK_NOPE = 64
QK_ROPE = 32
QK_HEAD = QK_NOPE + QK_ROPE
V_DIM = 64
HEAD_PAD = 128
A_GROUPS = 4
CHUNK = 128
ROPE_THETA = 10000.0
PAGE = 128
LOG2_E = 1.4426950408889634

FF_CHUNK = 256
TOKEN_TILE = 512
VMEM_LIMIT = 56 * 1024 * 1024

_NT = (((1,), (1,)), ((), ()))


def _dot(a, b):
    return jnp.dot(a, b, preferred_element_type=F32)


def _dot_nt(a, b):
    return lax.dot_general(a, b, _NT, preferred_element_type=F32)


def _rms(x, g):
    return x * lax.rsqrt(jnp.mean(x * x, axis=-1, keepdims=True) + EPS) * g


def _ffn_half(x, g_ref, win_ref, wout_ref, act_ref):
    d_ff = wout_ref.shape[0]
    xn = _rms(x, g_ref[...]).astype(BF16)
    for c in range(d_ff // FF_CHUNK):
        lo = c * FF_CHUNK
        a = _dot(xn, win_ref[:, lo:lo + FF_CHUNK])
        b = _dot(xn, win_ref[:, d_ff + lo:d_ff + lo + FF_CHUNK])
        act_ref[:, lo:lo + FF_CHUNK] = (a * jax.nn.sigmoid(a) * b).astype(BF16)
    return x + 0.5 * _dot(act_ref[...], wout_ref[...])


def _head_norm(xh, gain):
    ss = jnp.sum(xh * xh, axis=-1, keepdims=True)
    return xh * lax.rsqrt(ss * (1.0 / QK_HEAD) + EPS) * gain


def _pre_kernel(expand_kv, write_vn, *refs):
    (x_ref, cos_ref, sin_ref, f1g_ref, f1in_ref, f1out_ref, mixg_ref, wpre_ref, qag_ref, kvag_ref,
     wuq_ref, qg_ref, lng_ref, lnb_ref, mixw_ref, mixb_ref) = refs[:16]
    pos = 16
    if expand_kv:
        wukv_ref, kg_ref = refs[pos:pos + 2]
        pos += 2
    h1_ref, ain_ref, q_ref, ckv_ref, kr_ref = refs[pos:pos + 5]
    pos += 5
    if expand_kv:
        k_ref, v_ref = refs[pos:pos + 2]
        pos += 2
    if write_vn:
        vn_ref = refs[pos]
        pos += 1
    act_ref = refs[pos]

    a_width = lng_ref.shape[1]
    q_lora = qag_ref.shape[1]
    kv_lora = kvag_ref.shape[1]
    rows = x_ref.shape[0]

    h1 = _ffn_half(x_ref[...], f1g_ref, f1in_ref, f1out_ref, act_ref)
    h1_ref[...] = h1
    n = _rms(h1, mixg_ref[...]).astype(BF16)

    c0 = 2 * a_width
    uv = jax.nn.gelu(_dot(n, wpre_ref[:, :c0]))
    u, v = uv[:, :a_width], uv[:, a_width:]
    vc = v - jnp.mean(v, axis=-1, keepdims=True)
    vn = vc * lax.rsqrt(jnp.mean(vc * vc, axis=-1, keepdims=True) + EPS) * lng_ref[...] + lnb_ref[...]
    if write_vn:
        vn_ref[...] = vn
    vnb = vn.astype(BF16)
    gd = a_width // A_GROUPS
    for r in range(rows // CHUNK):
        rs = slice(r * CHUNK, (r + 1) * CHUNK)
        for g in range(A_GROUPS):
            gs = slice(g * gd, (g + 1) * gd)
            z = _dot(mixw_ref[g], vnb[rs, gs]) + mixb_ref[:, gs]
            ain_ref[rs, gs] = (u[rs, gs] * z).astype(ain_ref.dtype)

    cos = cos_ref[...]
    sin = sin_ref[...]
    c1 = c0 + q_lora
    c2 = c1 + kv_lora
    cq = _rms(_dot(n, wpre_ref[:, c0:c1]), qag_ref[...]).astype(BF16)
    nq = N_HEADS * HEAD_PAD
    qq = _dot(cq, wuq_ref[...])
    scale = QK_HEAD ** -0.5 * LOG2_E
    for h in range(N_HEADS):
        hs = slice(h * HEAD_PAD, (h + 1) * HEAD_PAD)
        qh = qq[:, hs] * cos + qq[:, nq + h * HEAD_PAD:nq + (h + 1) * HEAD_PAD] * sin
        q_ref[:, hs] = (_head_norm(qh, qg_ref[...]) * scale).astype(q_ref.dtype)

    c_kv = _rms(_dot(n, wpre_ref[:, c1:c2]), kvag_ref[...])
    ckv_ref[...] = c_kv
    krp = _dot(n, wpre_ref[:, c2:c2 + 2 * HEAD_PAD])
    kr_blk = krp[:, :HEAD_PAD] * cos + krp[:, HEAD_PAD:] * sin
    kr_ref[...] = kr_blk[:, QK_NOPE:QK_HEAD]

    if expand_kv:
        kv = _dot(c_kv.astype(BF16), wukv_ref[...])
        for h in range(N_HEADS):
            hs = slice(h * HEAD_PAD, (h + 1) * HEAD_PAD)
            k_ref[:, hs] = _head_norm(kv[:, hs] + kr_blk, kg_ref[...]).astype(k_ref.dtype)
        v_ref[...] = kv[:, nq:].astype(v_ref.dtype)


def _const_spec(shape):
    nd = len(shape)
    return pl.BlockSpec(shape, lambda i, _nd=nd: (0,) * _nd, pipeline_mode=pl.Buffered(1))


def _pre_call(x, cos_tab, sin_tab, w, expand_kv, write_vn, q_dtype):
    n_tok, d_model = x.shape
    tm = TOKEN_TILE
    n_tab = cos_tab.shape[0] // tm
    a_width = w['lng'].shape[1]
    kv_lora = w['kvag'].shape[1]
    d_ff = w['f1out'].shape[0]
    nq = N_HEADS * HEAD_PAD

    def row_spec(cols):
        return pl.BlockSpec((tm, cols), lambda i: (i, 0))

    tab_spec = pl.BlockSpec((tm, HEAD_PAD), lambda i: (i % n_tab, 0))
    consts = [w['f1g'], w['f1in'], w['f1out'], w['mixg'], w['wpre'], w['qag'], w['kvag'], w['wuq'],
              w['qg'], w['lng'], w['lnb'], w['mixw'], w['mixb']]
    if expand_kv:
        consts += [w['wukv'], w['kg']]
    in_specs = [row_spec(d_model), tab_spec, tab_spec] + [_const_spec(c.shape) for c in consts]
    out_shape = [jax.ShapeDtypeStruct((n_tok, d_model), F32),
                 jax.ShapeDtypeStruct((n_tok, a_width), BF16),
                 jax.ShapeDtypeStruct((n_tok, nq), q_dtype),
                 jax.ShapeDtypeStruct((n_tok, kv_lora), F32),
                 jax.ShapeDtypeStruct((n_tok, QK_ROPE), F32)]
    out_specs = [row_spec(d_model), row_spec(a_width), row_spec(nq), row_spec(kv_lora), row_spec(QK_ROPE)]
    if expand_kv:
        out_shape += [jax.ShapeDtypeStruct((n_tok, nq), BF16),
                      jax.ShapeDtypeStruct((n_tok, N_HEADS * V_DIM), BF16)]
        out_specs += [row_spec(nq), row_spec(N_HEADS * V_DIM)]
    if write_vn:
        out_shape.append(jax.ShapeDtypeStruct((n_tok, a_width), F32))
        out_specs.append(row_spec(a_width))
    return pl.pallas_call(
        functools.partial(_pre_kernel, expand_kv, write_vn),
        grid=(n_tok // tm,),
        in_specs=in_specs,
        out_specs=out_specs,
        out_shape=out_shape,
        scratch_shapes=[pltpu.VMEM((tm, d_ff), BF16)],
        compiler_params=pltpu.CompilerParams(dimension_semantics=("arbitrary",),
                                             vmem_limit_bytes=VMEM_LIMIT),
        name="pre_expand" if expand_kv else "pre_latent",
    )(x, cos_tab, sin_tab, *consts)


def _post_kernel(h1_ref, ain_ref, o_ref, p_ref, mixg_ref, wg_ref, wa_ref, wb_ref, wo_ref,
                 f2g_ref, f2in_ref, f2out_ref, pleg_ref, plewg_ref, plewp_ref, y_ref, act_ref):
    d_model = h1_ref.shape[1]
    h1 = h1_ref[...]
    n = _rms(h1, mixg_ref[...]).astype(BF16)
    ga = jax.nn.sigmoid(_dot(n, wg_ref[:, :d_model]))
    gb = jax.nn.sigmoid(_dot(n, wg_ref[:, d_model:]))
    m = ga * _dot(ain_ref[...], wa_ref[...]) + gb * _dot(o_ref[...], wb_ref[...])
    h2 = h1 + _dot(m.astype(BF16), wo_ref[...])
    h3 = _ffn_half(h2, f2g_ref, f2in_ref, f2out_ref, act_ref)
    hn = _rms(h3, pleg_ref[...]).astype(BF16)
    gate = jax.nn.sigmoid(_dot(hn, plewg_ref[...]))
    y_ref[...] = h3 + gate * _dot(p_ref[...].astype(BF16), plewp_ref[...])


def _post_call(h1, ain, o, p, w):
    n_tok, d_model = h1.shape
    tm = TOKEN_TILE
    d_ff = w['f2out'].shape[0]

    def row_spec(cols):
        return pl.BlockSpec((tm, cols), lambda i: (i, 0))

    consts = [w['mixg'], w['wgates'], w['wa'], w['wb'], w['wo'], w['f2g'], w['f2in'], w['f2out'],
              w['pleg'], w['plewg'], w['plewp']]
    return pl.pallas_call(
        _post_kernel,
        grid=(n_tok // tm,),
        in_specs=[row_spec(d_model), row_spec(ain.shape[1]), row_spec(o.shape[1]), row_spec(p.shape[1])]
        + [_const_spec(c.shape) for c in consts],
        out_specs=row_spec(d_model),
        out_shape=jax.ShapeDtypeStruct((n_tok, d_model), F32),
        scratch_shapes=[pltpu.VMEM((tm, d_ff), BF16)],
        compiler_params=pltpu.CompilerParams(dimension_semantics=("arbitrary",),
                                             vmem_limit_bytes=VMEM_LIMIT),
        name="post",
    )(h1, ain, o, p, *consts)


ATT_BQ = 512
ATT_BK = 256
ATT_VT_COLS = 512
ATT_ONES_ROWS = 16


def _prompt_attn_kernel(q_ref, k_ref, v_ref, o_ref, vt_ref, st_ref):
    bq = ATT_BQ
    seq = k_ref.shape[0]

    for r in range(seq // ATT_VT_COLS):
        rs = slice(r * ATT_VT_COLS, (r + 1) * ATT_VT_COLS)
        vt = v_ref[rs, :].astype(F32).T.astype(BF16)
        for hh in range(2):
            vt_ref[hh, :V_DIM, rs] = vt[hh * V_DIM:(hh + 1) * V_DIM]
            vt_ref[hh, V_DIM:, rs] = jnp.ones((ATT_ONES_ROWS, ATT_VT_COLS), BF16)

    def query_block(i, _):
        rows = pl.ds(pl.multiple_of(i * bq, bq), bq)
        qts = [q_ref[rows, hh * HEAD_PAD:(hh + 1) * HEAD_PAD].astype(F32).T.astype(BF16) for hh in range(2)]

        def scores(j, slot):
            start = pl.multiple_of(j * ATT_BK, ATT_BK)
            for hh in range(2):
                st_ref[slot, hh] = _dot(k_ref[pl.ds(start, ATT_BK), hh * HEAD_PAD:(hh + 1) * HEAD_PAD],
                                        qts[hh])

        def update(j, slot, carry, masked):
            start = pl.multiple_of(j * ATT_BK, ATT_BK)
            out = []
            for hh in range(2):
                m, acc = carry[hh]
                st = st_ref[slot, hh]
                if masked:
                    key = start + lax.broadcasted_iota(jnp.int32, st.shape, 0)
                    qry = i * bq + lax.broadcasted_iota(jnp.int32, st.shape, 1)
                    st = jnp.where(key <= qry, st, -1e30)
                m_new = jnp.maximum(m, jnp.max(st, axis=0, keepdims=True))
                p = jnp.exp2(st - m_new)
                acc = jnp.exp2(m - m_new) * acc + _dot(vt_ref[hh, :, pl.ds(start, ATT_BK)], p.astype(BF16))
                out.append((m_new, acc))
            return tuple(out)

        one = (jnp.full((1, bq), -1e30, F32), jnp.zeros((V_DIM + ATT_ONES_ROWS, bq), F32))

        def pair(t, carry):
            scores(2 * t + 1, 1)
            carry = update(2 * t, 0, carry, False)
            scores(2 * t + 2, 0)
            return update(2 * t + 1, 1, carry, False)

        scores(0, 0)
        carry = lax.fori_loop(0, i, pair, (one, one))
        scores(2 * i + 1, 1)
        carry = update(2 * i, 0, carry, True)
        carry = update(2 * i + 1, 1, carry, True)
        ot = jnp.concatenate([acc[:V_DIM] / acc[V_DIM:V_DIM + 1] for (_, acc) in carry], axis=0)
        o_ref[rows, :] = ot.T.astype(o_ref.dtype)
        return 0

    lax.fori_loop(0, seq // bq, query_block, 0)


def _prompt_attn(q, k, v, batch, seq):
    assert ATT_BQ == 2 * ATT_BK and seq % ATT_BQ == 0 and seq % ATT_VT_COLS == 0
    n_tok = q.shape[0]
    return pl.pallas_call(
        _prompt_attn_kernel,
        grid=(batch, N_HEADS // 2),
        in_specs=[pl.BlockSpec((seq, 2 * HEAD_PAD), lambda b, hp: (b, hp)),
                  pl.BlockSpec((seq, 2 * HEAD_PAD), lambda b, hp: (b, hp)),
                  pl.BlockSpec((seq, 2 * V_DIM), lambda b, hp: (b, hp))],
        out_specs=pl.BlockSpec((seq, 2 * V_DIM), lambda b, hp: (b, hp)),
        out_shape=jax.ShapeDtypeStruct((n_tok, N_HEADS * V_DIM), BF16),
        scratch_shapes=[pltpu.VMEM((2, V_DIM + ATT_ONES_ROWS, seq), BF16),
                        pltpu.VMEM((2, 2, ATT_BK, ATT_BQ), F32)],
        compiler_params=pltpu.CompilerParams(dimension_semantics=("arbitrary",) * 2,
                                             vmem_limit_bytes=VMEM_LIMIT),
        name="prompt_attn",
    )(q, k, v)


SAMPLE_SUB = 1024


def _qabs_kernel(q_ref, gk_ref, wuk_ref, qlat_ref, qr_ref):
    for h in range(N_HEADS):
        qh = q_ref[:, h * HEAD_PAD:(h + 1) * HEAD_PAD] * gk_ref[...]
        qlat_ref[h] = _dot(qh[:, :QK_NOPE].astype(BF16), wuk_ref[h])
        qr_ref[h] = qh[:, QK_NOPE:QK_HEAD]


def _qabs_call(q, gk, wuk):
    n_tok = q.shape[0]
    kv_lora = wuk.shape[2]
    return pl.pallas_call(
        _qabs_kernel,
        out_shape=[jax.ShapeDtypeStruct((N_HEADS, n_tok, kv_lora), F32),
                   jax.ShapeDtypeStruct((N_HEADS, n_tok, QK_ROPE), F32)],
        compiler_params=pltpu.CompilerParams(vmem_limit_bytes=VMEM_LIMIT),
        name="q_absorb",
    )(q, gk, wuk)


def _sample_attn_kernel(layer, pt_ref, qlat_ref, qr_ref, cnew_ref, krnew_ref, wukt_ref, ckv_hbm, krt_hbm,
                        olat_ref, cbuf, kbuf, csem, ksem, lhs_ref, cb_ref):
    s = pl.program_id(0)
    n_seq = pl.num_programs(0)
    past = cbuf.shape[1]
    n_pages = past // PAGE
    t_new = cnew_ref.shape[0]
    kv_lora = cnew_ref.shape[1]
    n_rows = N_HEADS * t_new
    n_kn = N_HEADS * QK_NOPE
    slot = s % 2

    def copies(seq, slot_):
        out = []
        for p in range(n_pages):
            page = pt_ref[seq * n_pages + p]
            keys = pl.ds(p * PAGE, PAGE)
            out.append(pltpu.make_async_copy(ckv_hbm.at[layer, page], cbuf.at[slot_, keys, :], csem.at[slot_]))
            out.append(pltpu.make_async_copy(krt_hbm.at[layer, page], kbuf.at[slot_, :, keys], ksem.at[slot_]))
        return out

    def start_all(cps):
        for n, cp in enumerate(cps):
            cp.start(priority=(n // 2) % 2)

    @pl.when(s == 0)
    def _():
        lhs_ref[:n_kn, :] = wukt_ref[...]
        start_all(copies(s, slot))

    for cp in copies(s, slot):
        cp.wait()

    lhs_ref[n_kn:, :] = qlat_ref[...].reshape(n_rows, kv_lora).astype(BF16)
    qr = qr_ref[...].reshape(n_rows, QK_ROPE).astype(BF16)

    def scaled_scores(big, rope_scores, r2):
        n = big.shape[1]
        kn = big[:n_kn]
        ss = jnp.sum((kn * kn).reshape(N_HEADS, QK_NOPE, n), axis=1)
        rs = lax.rsqrt((ss + r2) * (1.0 / QK_HEAD) + EPS)
        sc = (big[n_kn:] + rope_scores).reshape(N_HEADS, t_new, n) * rs[:, None, :]
        return sc.reshape(n_rows, n)

    def past_scores(k):
        keys = slice(k * SAMPLE_SUB, (k + 1) * SAMPLE_SUB)
        cb = cbuf[slot, keys, :].astype(BF16)
        cb_ref[k % 2] = cb
        krt = kbuf[slot, :, keys]
        r2 = jnp.sum(krt * krt, axis=0, keepdims=True)
        return scaled_scores(_dot_nt(lhs_ref[...], cb), _dot(qr, krt.astype(BF16)), r2)

    pad = PAGE - t_new
    cb_new = jnp.concatenate([cnew_ref[...], jnp.zeros((pad, kv_lora), F32)], axis=0).astype(BF16)
    kr_new = jnp.concatenate([krnew_ref[...], jnp.zeros((pad, QK_ROPE), F32)], axis=0)
    kr2 = kr_new * kr_new
    kr2_hi = kr2.astype(BF16)
    kr2_lo = (kr2 - kr2_hi.astype(F32)).astype(BF16)
    ones = jnp.ones((8, QK_ROPE), BF16)
    r2_new = (_dot_nt(ones, kr2_hi) + _dot_nt(ones, kr2_lo))[:1]
    sc_new = scaled_scores(_dot_nt(lhs_ref[...], cb_new), _dot_nt(qr, kr_new.astype(BF16)), r2_new)
    row = lax.broadcasted_iota(jnp.int32, (n_rows, PAGE), 0)
    col = lax.broadcasted_iota(jnp.int32, (n_rows, PAGE), 1)
    sc_new = jnp.where(col <= row % t_new, sc_new, -1e30)

    n_sub = past // SAMPLE_SUB
    sc_next = past_scores(0)

    nxt = jnp.minimum(s + 1, n_seq - 1)
    start_all(copies(nxt, 1 - slot))

    m = jnp.max(sc_new, axis=-1, keepdims=True)
    p = jnp.exp2(sc_new - m)
    l = jnp.sum(p, axis=-1, keepdims=True)
    acc = _dot(p.astype(BF16), cb_new)
    for k in range(n_sub):
        sc = sc_next
        if k + 1 < n_sub:
            sc_next = past_scores(k + 1)
        m_new = jnp.maximum(m, jnp.max(sc, axis=-1, keepdims=True))
        alpha = jnp.exp2(m - m_new)
        p = jnp.exp2(sc - m_new)
        l = alpha * l + jnp.sum(p, axis=-1, keepdims=True)
        acc = alpha * acc + _dot(p.astype(BF16), cb_ref[k % 2])
        m = m_new
    olat_ref[...] = acc / l

    @pl.when(s == n_seq - 1)
    def _():
        for cp in copies(nxt, 1 - slot):
            cp.wait()


def _sample_attn(layer, page_table, qlat, qr, c_new, kr_new, wukt, cache_ckv, cache_krt):
    n_seq, n_pages = page_table.shape
    t_new = c_new.shape[1]
    kv_lora = c_new.shape[2]
    n_rows = N_HEADS * t_new
    past = n_pages * PAGE
    assert past % SAMPLE_SUB == 0
    grid_spec = pltpu.PrefetchScalarGridSpec(
        num_scalar_prefetch=1,
        grid=(n_seq,),
        in_specs=[pl.BlockSpec((N_HEADS, t_new, kv_lora), lambda s, pt: (0, s, 0)),
                  pl.BlockSpec((N_HEADS, t_new, QK_ROPE), lambda s, pt: (0, s, 0)),
                  pl.BlockSpec((None, t_new, kv_lora), lambda s, pt: (s, 0, 0)),
                  pl.BlockSpec((None, t_new, QK_ROPE), lambda s, pt: (s, 0, 0)),
                  pl.BlockSpec(wukt.shape, lambda s, pt: (0, 0)),
                  pl.BlockSpec(memory_space=pl.ANY),
                  pl.BlockSpec(memory_space=pl.ANY)],
        out_specs=pl.BlockSpec((None, n_rows, kv_lora), lambda s, pt: (s, 0, 0)),
        scratch_shapes=[pltpu.VMEM((2, past, kv_lora), F32),
                        pltpu.VMEM((2, QK_ROPE, past), F32),
                        pltpu.SemaphoreType.DMA((2,)),
                        pltpu.SemaphoreType.DMA((2,)),
                        pltpu.VMEM((N_HEADS * QK_NOPE + n_rows, kv_lora), BF16),
                        pltpu.VMEM((2, SAMPLE_SUB, kv_lora), BF16)],
    )
    return pl.pallas_call(
        functools.partial(_sample_attn_kernel, layer),
        grid_spec=grid_spec,
        out_shape=jax.ShapeDtypeStruct((n_seq, n_rows, kv_lora), F32),
        compiler_params=pltpu.CompilerParams(dimension_semantics=("arbitrary",),
                                             vmem_limit_bytes=VMEM_LIMIT),
        name="sample_attn",
    )(page_table.reshape(-1), qlat, qr, c_new, kr_new, wukt, cache_ckv, cache_krt)


def _oproj_kernel(olat_ref, wuv_ref, o_ref):
    n_seq, n_rows, kv_lora = olat_ref.shape
    t_new = n_rows // N_HEADS
    acc = None
    for h in range(N_HEADS):
        x = olat_ref[:, h * t_new:(h + 1) * t_new, :].reshape(n_seq * t_new, kv_lora).astype(BF16)
        part = _dot(x, wuv_ref[h])
        acc = part if acc is None else acc + part
    o_ref[...] = acc.astype(o_ref.dtype)


def _oproj_call(olat, wuv):
    n_seq, n_rows, _ = olat.shape
    return pl.pallas_call(
        _oproj_kernel,
        out_shape=jax.ShapeDtypeStruct((n_seq * n_rows // N_HEADS, N_HEADS * V_DIM), BF16),
        compiler_params=pltpu.CompilerParams(vmem_limit_bytes=VMEM_LIMIT),
        name="o_proj",
    )(olat, wuv)


def _pad_heads(w, width):
    k = w.shape[0]
    w = w.reshape(k, N_HEADS, width)
    return jnp.pad(w, ((0, 0), (0, 0), (0, HEAD_PAD - width))).reshape(k, N_HEADS * HEAD_PAD)


def _rot_cols(w):
    half = w.shape[-1] // 2
    return jnp.concatenate([-w[..., half:], w[..., :half]], axis=-1)


def _rope_block(w):
    return jnp.pad(w, ((0, 0), (QK_NOPE, HEAD_PAD - QK_HEAD)))


def _head_gain(g):
    return jnp.pad(g, (0, HEAD_PAD - QK_HEAD)).reshape(1, HEAD_PAD)


def _rope_tables(n_pos, offset):
    pos = jnp.arange(n_pos, dtype=F32) + offset
    inv_freq = ROPE_THETA ** (-jnp.arange(0, QK_ROPE, 2, dtype=F32) / QK_ROPE)
    ang = pos[:, None] * inv_freq[None, :]
    cos, sin = jnp.cos(ang), jnp.sin(ang)
    ones = jnp.ones((n_pos, QK_NOPE), F32)
    tail = jnp.ones((n_pos, HEAD_PAD - QK_HEAD), F32)
    cos_tab = jnp.concatenate([ones, cos, cos, tail], axis=1)
    sin_tab = jnp.concatenate([0 * ones, sin, sin, 0 * tail], axis=1)
    return cos_tab, sin_tab


def _layer_weights(i, ffn1_norm, ffn1_w_in, ffn1_w_out, mix_norm, w_in, qa_norm, kva_norm, w_uq, w_ukv,
                   q_norm, k_norm, sgu_ln_g, sgu_ln_b, sgu_w_s, sgu_b_s, w_a_out, w_b_out, w_o,
                   ffn2_norm, ffn2_w_in, ffn2_w_out, ple_norm, ple_w_gate, ple_w_proj):
    a_width = sgu_ln_g.shape[1]
    q_lora = qa_norm.shape[1]
    kv_lora = kva_norm.shape[1]
    s1 = 2 * a_width
    s2 = s1 + q_lora
    s3 = s2 + kv_lora
    s4 = s3 + QK_ROPE
    wi = w_in[i]
    w_kr = wi[:, s3:s4]
    wpre = jnp.concatenate([wi[:, :s3], _rope_block(w_kr), _rope_block(_rot_cols(w_kr))], axis=1)

    wq = w_uq[i].reshape(q_lora, N_HEADS, QK_HEAD)
    wq_rot = jnp.concatenate([jnp.zeros_like(wq[..., :QK_NOPE]), _rot_cols(wq[..., QK_NOPE:])], axis=-1)
    wuq = jnp.concatenate([_pad_heads(wq.reshape(q_lora, -1), QK_HEAD),
                           _pad_heads(wq_rot.reshape(q_lora, -1), QK_HEAD)], axis=1)

    wkv = w_ukv[i].reshape(kv_lora, N_HEADS, QK_NOPE + V_DIM)
    w_uk = wkv[..., :QK_NOPE]
    w_uv = wkv[..., QK_NOPE:]
    wukv = jnp.concatenate([_pad_heads(w_uk.reshape(kv_lora, -1), QK_NOPE),
                            w_uv.reshape(kv_lora, -1)], axis=1)
    wuk_t = jnp.transpose(w_uk, (1, 2, 0))
    eye = jnp.eye(N_HEADS, dtype=F32)
    wuv_blk = jnp.einsum('khd,hg->hkgd', w_uv, eye).reshape(N_HEADS, kv_lora, N_HEADS * V_DIM)

    row = lambda g: g[i].reshape(1, -1)
    return {
        'f1g': row(ffn1_norm), 'f1in': ffn1_w_in[i].astype(BF16), 'f1out': ffn1_w_out[i].astype(BF16),
        'mixg': row(mix_norm), 'wpre': wpre.astype(BF16), 'wgates': wi[:, s4:].astype(BF16),
        'qag': row(qa_norm), 'kvag': row(kva_norm), 'wuq': wuq.astype(BF16), 'wukv': wukv.astype(BF16),
        'qg': _head_gain(q_norm[i]), 'kg': _head_gain(k_norm[i]),
        'wuk_t': wuk_t.astype(BF16), 'wuv_blk': wuv_blk.astype(BF16),
        'lng': row(sgu_ln_g), 'lnb': row(sgu_ln_b), 'w_s': sgu_w_s[i], 'b_s': sgu_b_s[i],
        'wa': w_a_out[i].astype(BF16), 'wb': w_b_out[i].astype(BF16), 'wo': w_o[i].astype(BF16),
        'f2g': row(ffn2_norm), 'f2in': ffn2_w_in[i].astype(BF16), 'f2out': ffn2_w_out[i].astype(BF16),
        'pleg': row(ple_norm), 'plewg': ple_w_gate[i].astype(BF16), 'plewp': ple_w_proj[i].astype(BF16),
    }


def _mix_tables(w_s, b_s, period, group_dim):
    r = jnp.arange(CHUNK)
    same = (r[:, None] // period) == (r[None, :] // period)
    causal = (r[None, :] % period) <= (r[:, None] % period)
    reps = CHUNK // period
    mixw = jnp.where((same & causal)[None], jnp.tile(w_s[:, :period, :period], (1, reps, reps)), 0.0)
    bias = jnp.tile(b_s[:, :period], (1, reps))
    mixb = jnp.repeat(bias.T, group_dim, axis=1)
    return mixw.astype(BF16), mixb


def kernel(x_prompt, x_sample, cache_ckv, cache_krope, page_table, p_prompt, p_sample, ffn1_norm, ffn1_w_in, ffn1_w_out, mix_norm, w_in, qa_norm, kva_norm, w_uq, w_ukv, q_norm, k_norm, sgu_ln_g, sgu_ln_b, sgu_w_s, sgu_b_s, w_a_out, w_b_out, w_o, ffn2_norm, ffn2_w_in, ffn2_w_out, ple_norm, ple_w_gate, ple_w_proj):
    depth = w_in.shape[0]
    batch, seq, d_model = x_prompt.shape
    dec_batch, dec_seq, _ = x_sample.shape
    past_len = page_table.shape[1] * cache_ckv.shape[2]
    assert seq % TOKEN_TILE == 0 and TOKEN_TILE % dec_seq == 0 and CHUNK % dec_seq == 0
    assert cache_ckv.shape[2] == PAGE

    yp = x_prompt.reshape(batch * seq, d_model)
    ys = x_sample.reshape(dec_batch * dec_seq, d_model)
    cos_p, sin_p = _rope_tables(seq, 0)
    cos_s, sin_s = _rope_tables(dec_seq, past_len)
    reps = TOKEN_TILE // dec_seq
    cos_s, sin_s = jnp.tile(cos_s, (reps, 1)), jnp.tile(sin_s, (reps, 1))

    outs = [[] for _ in range(5)]
    for i in range(depth):
        w = _layer_weights(i, ffn1_norm, ffn1_w_in, ffn1_w_out, mix_norm, w_in, qa_norm, kva_norm, w_uq,
                           w_ukv, q_norm, k_norm, sgu_ln_g, sgu_ln_b, sgu_w_s, sgu_b_s, w_a_out, w_b_out,
                           w_o, ffn2_norm, ffn2_w_in, ffn2_w_out, ple_norm, ple_w_gate, ple_w_proj)

        group_dim = w['lng'].shape[1] // A_GROUPS
        w['mixw'], w['mixb'] = _mix_tables(w['w_s'], w['b_s'], CHUNK, group_dim)
        h1, ain, q, ckv_p, kr_p, k, v = _pre_call(yp, cos_p, sin_p, w, True, False, BF16)
        o = _prompt_attn(q, k, v, batch, seq)
        yp = _post_call(h1, ain, o, p_prompt[i].reshape(batch * seq, -1), w)

        w['mixw'], w['mixb'] = _mix_tables(w['w_s'], w['b_s'], dec_seq, group_dim)
        h1, ain, q, ckv_s, kr_s, vn_s = _pre_call(ys, cos_s, sin_s, w, False, True, F32)
        qlat, qr = _qabs_call(q, w['kg'], w['wuk_t'])
        kv_lora = ckv_s.shape[1]
        olat = _sample_attn(i, page_table, qlat, qr,
                            ckv_s.reshape(dec_batch, dec_seq, kv_lora),
                            kr_s.reshape(dec_batch, dec_seq, QK_ROPE),
                            w['wuk_t'].reshape(N_HEADS * QK_NOPE, kv_lora),
                            cache_ckv, jnp.swapaxes(cache_krope, 2, 3))
        o = _oproj_call(olat, w['wuv_blk'])
        ys = _post_call(h1, ain, o, p_sample[i].reshape(dec_batch * dec_seq, -1), w)

        outs[0].append(ckv_p.reshape(batch, seq, -1))
        outs[1].append(kr_p.reshape(batch, seq, -1))
        outs[2].append(ckv_s.reshape(dec_batch, dec_seq, -1))
        outs[3].append(kr_s.reshape(dec_batch, dec_seq, -1))
        outs[4].append(vn_s.reshape(dec_batch, dec_seq, -1))

    return (yp.reshape(batch, seq, d_model), ys.reshape(dec_batch, dec_seq, d_model),
            jnp.stack(outs[0]), jnp.stack(outs[1]), jnp.stack(outs[2]), jnp.stack(outs[3]),
            jnp.stack(outs[4]))
```

```python
import functools

import jax
import jax.numpy as jnp
from jax import lax
from jax.experimental import pallas as pl
from jax.experimental.pallas import tpu as pltpu

F32 = jnp.float32
BF16 = jnp.bfloat16

EPS = 1e-6
N_HEADS = 8
QK_NOPE = 64
QK_ROPE = 32
QK_HEAD = QK_NOPE + QK_ROPE
V_DIM = 64
HEAD_PAD = 128
A_GROUPS = 4
CHUNK = 128
ROPE_THETA = 10000.0
PAGE = 128
LOG2_E = 1.4426950408889634

FF_CHUNK = 256
TOKEN_TILE = 512
VMEM_LIMIT = 56 * 1024 * 1024

_NT = (((1,), (1,)), ((), ()))


def _dot(a, b):
    return jnp.dot(a, b, preferred_element_type=F32)


def _dot_nt(a, b):
    return lax.dot_general(a, b, _NT, preferred_element_type=F32)


def _rms(x, g):
    return x * lax.rsqrt(jnp.mean(x * x, axis=-1, keepdims=True) + EPS) * g


def _ffn_half(x, g_ref, win_ref, wout_ref, act_ref):
    d_ff = wout_ref.shape[0]
    xn = _rms(x, g_ref[...]).astype(BF16)
    for c in range(d_ff // FF_CHUNK):
        lo = c * FF_CHUNK
        a = _dot(xn, win_ref[:, lo:lo + FF_CHUNK])
        b = _dot(xn, win_ref[:, d_ff + lo:d_ff + lo + FF_CHUNK])
        act_ref[:, lo:lo + FF_CHUNK] = (a * jax.nn.sigmoid(a) * b).astype(BF16)
    return x + 0.5 * _dot(act_ref[...], wout_ref[...])


def _head_norm(xh, gain):
    ss = jnp.sum(xh * xh, axis=-1, keepdims=True)
    return xh * lax.rsqrt(ss * (1.0 / QK_HEAD) + EPS) * gain


def _pre_kernel(expand_kv, write_vn, *refs):
    (x_ref, cos_ref, sin_ref, f1g_ref, f1in_ref, f1out_ref, mixg_ref, wpre_ref, qag_ref, kvag_ref,
     wuq_ref, qg_ref, lng_ref, lnb_ref, mixw_ref, mixb_ref) = refs[:16]
    pos = 16
    if expand_kv:
        wukv_ref, kg_ref = refs[pos:pos + 2]
        pos += 2
    h1_ref, ain_ref, q_ref, ckv_ref, kr_ref = refs[pos:pos + 5]
    pos += 5
    if expand_kv:
        k_ref, v_ref = refs[pos:pos + 2]
        pos += 2
    if write_vn:
        vn_ref = refs[pos]
        pos += 1
    act_ref = refs[pos]

    a_width = lng_ref.shape[1]
    q_lora = qag_ref.shape[1]
    kv_lora = kvag_ref.shape[1]
    rows = x_ref.shape[0]

    h1 = _ffn_half(x_ref[...], f1g_ref, f1in_ref, f1out_ref, act_ref)
    h1_ref[...] = h1
    n = _rms(h1, mixg_ref[...]).astype(BF16)

    c0 = 2 * a_width
    c1 = c0 + q_lora
    c2 = c1 + kv_lora
    nq = N_HEADS * HEAD_PAD
    p_cq = _dot(n, wpre_ref[:, c0:c1])
    p_ckv = _dot(n, wpre_ref[:, c1:c2])
    p_kr = _dot(n, wpre_ref[:, c2:c2 + 2 * HEAD_PAD])
    p_uv = _dot(n, wpre_ref[:, :c0])

    uv = jax.nn.gelu(p_uv)
    u, v = uv[:, :a_width], uv[:, a_width:]
    vc = v - jnp.mean(v, axis=-1, keepdims=True)
    vn = vc * lax.rsqrt(jnp.mean(vc * vc, axis=-1, keepdims=True) + EPS) * lng_ref[...] + lnb_ref[...]
    if write_vn:
        vn_ref[...] = vn
    vnb = vn.astype(BF16)

    cos = cos_ref[...]
    sin = sin_ref[...]
    cq = _rms(p_cq, qag_ref[...]).astype(BF16)
    scale = QK_HEAD ** -0.5 * LOG2_E
    for h in range(N_HEADS):
        t = _dot(cq, wuq_ref[:, 2 * h * HEAD_PAD:2 * (h + 1) * HEAD_PAD])
        qh = t[:, :HEAD_PAD] * cos + t[:, HEAD_PAD:] * sin
        q_ref[:, h * HEAD_PAD:(h + 1) * HEAD_PAD] = (_head_norm(qh, qg_ref[...]) * scale).astype(q_ref.dtype)
    c_kv = _rms(p_ckv, kvag_ref[...])
    ckv_ref[...] = c_kv
    kr_blk = p_kr[:, :HEAD_PAD] * cos + p_kr[:, HEAD_PAD:] * sin
    kr_ref[...] = kr_blk[:, QK_NOPE:QK_HEAD]
    if expand_kv:
        ckvb = c_kv.astype(BF16)
        for hp in range(N_HEADS // 2):
            t = _dot(ckvb, wukv_ref[:, 2 * hp * HEAD_PAD:2 * (hp + 1) * HEAD_PAD])
            for hh in range(2):
                hs = slice((2 * hp + hh) * HEAD_PAD, (2 * hp + hh + 1) * HEAD_PAD)
                kh = t[:, hh * HEAD_PAD:(hh + 1) * HEAD_PAD] + kr_blk
                k_ref[:, hs] = _head_norm(kh, kg_ref[...]).astype(k_ref.dtype)
        v_ref[...] = _dot(ckvb, wukv_ref[:, nq:]).astype(v_ref.dtype)

    gd = a_width // A_GROUPS
    for r in range(rows // CHUNK):
        rs = slice(r * CHUNK, (r + 1) * CHUNK)
        for g in range(A_GROUPS):
            gs = slice(g * gd, (g + 1) * gd)
            z = _dot(mixw_ref[g], vnb[rs, gs]) + mixb_ref[:, gs]
            ain_ref[rs, gs] = (u[rs, gs] * z).astype(ain_ref.dtype)


def _const_spec(shape):
    nd = len(shape)
    return pl.BlockSpec(shape, lambda i, _nd=nd: (0,) * _nd, pipeline_mode=pl.Buffered(1))


def _pre_call(x, cos_tab, sin_tab, w, expand_kv, write_vn, q_dtype):
    n_tok, d_model = x.shape
    tm = TOKEN_TILE
    n_tab = cos_tab.shape[0] // tm
    a_width = w['lng'].shape[1]
    kv_lora = w['kvag'].shape[1]
    d_ff = w['f1out'].shape[0]
    nq = N_HEADS * HEAD_PAD

    def row_spec(cols):
        return pl.BlockSpec((tm, cols), lambda i: (i, 0))

    tab_spec = pl.BlockSpec((tm, HEAD_PAD), lambda i: (i % n_tab, 0))
    consts = [w['f1g'], w['f1in'], w['f1out'], w['mixg'], w['wpre'], w['qag'], w['kvag'], w['wuq'],
              w['qg'], w['lng'], w['lnb'], w['mixw'], w['mixb']]
    if expand_kv:
        consts += [w['wukv'], w['kg']]
    in_specs = [row_spec(d_model), tab_spec, tab_spec] + [_const_spec(c.shape) for c in consts]
    out_shape = [jax.ShapeDtypeStruct((n_tok, d_model), F32),
                 jax.ShapeDtypeStruct((n_tok, a_width), BF16),
                 jax.ShapeDtypeStruct((n_tok, nq), q_dtype),
                 jax.ShapeDtypeStruct((n_tok, kv_lora), F32),
                 jax.ShapeDtypeStruct((n_tok, QK_ROPE), F32)]
    out_specs = [row_spec(d_model), row_spec(a_width), row_spec(nq), row_spec(kv_lora), row_spec(QK_ROPE)]
    if expand_kv:
        out_shape += [jax.ShapeDtypeStruct((n_tok, nq), BF16),
                      jax.ShapeDtypeStruct((n_tok, N_HEADS * V_DIM), BF16)]
        out_specs += [row_spec(nq), row_spec(N_HEADS * V_DIM)]
    if write_vn:
        out_shape.append(jax.ShapeDtypeStruct((n_tok, a_width), F32))
        out_specs.append(row_spec(a_width))
    return pl.pallas_call(
        functools.partial(_pre_kernel, expand_kv, write_vn),
        grid=(n_tok // tm,),
        in_specs=in_specs,
        out_specs=out_specs,
        out_shape=out_shape,
        scratch_shapes=[pltpu.VMEM((tm, d_ff), BF16)],
        compiler_params=pltpu.CompilerParams(dimension_semantics=("arbitrary",),
                                             vmem_limit_bytes=VMEM_LIMIT),
        name="pre_expand" if expand_kv else "pre_latent",
    )(x, cos_tab, sin_tab, *consts)


def _post_kernel(h1_ref, ain_ref, o_ref, p_ref, mixg_ref, wg_ref, wa_ref, wb_ref, wo_ref,
                 f2g_ref, f2in_ref, f2out_ref, pleg_ref, plewg_ref, plewp_ref, y_ref, act_ref):
    d_model = h1_ref.shape[1]
    h1 = h1_ref[...]
    n = _rms(h1, mixg_ref[...]).astype(BF16)
    ga = jax.nn.sigmoid(_dot(n, wg_ref[:, :d_model]))
    gb = jax.nn.sigmoid(_dot(n, wg_ref[:, d_model:]))
    m = ga * _dot(ain_ref[...], wa_ref[...]) + gb * _dot(o_ref[...], wb_ref[...])
    h2 = h1 + _dot(m.astype(BF16), wo_ref[...])
    h3 = _ffn_half(h2, f2g_ref, f2in_ref, f2out_ref, act_ref)
    hn = _rms(h3, pleg_ref[...]).astype(BF16)
    gate = jax.nn.sigmoid(_dot(hn, plewg_ref[...]))
    y_ref[...] = h3 + gate * _dot(p_ref[...].astype(BF16), plewp_ref[...])


def _post_call(h1, ain, o, p, w):
    n_tok, d_model = h1.shape
    tm = TOKEN_TILE
    d_ff = w['f2out'].shape[0]

    def row_spec(cols):
        return pl.BlockSpec((tm, cols), lambda i: (i, 0))

    consts = [w['mixg'], w['wgates'], w['wa'], w['wb'], w['wo'], w['f2g'], w['f2in'], w['f2out'],
              w['pleg'], w['plewg'], w['plewp']]
    return pl.pallas_call(
        _post_kernel,
        grid=(n_tok // tm,),
        in_specs=[row_spec(d_model), row_spec(ain.shape[1]), row_spec(o.shape[1]), row_spec(p.shape[1])]
        + [_const_spec(c.shape) for c in consts],
        out_specs=row_spec(d_model),
        out_shape=jax.ShapeDtypeStruct((n_tok, d_model), F32),
        scratch_shapes=[pltpu.VMEM((tm, d_ff), BF16)],
        compiler_params=pltpu.CompilerParams(dimension_semantics=("arbitrary",),
                                             vmem_limit_bytes=VMEM_LIMIT),
        name="post",
    )(h1, ain, o, p, *consts)


ATT_BQ = 512
ATT_BK = 256
ATT_VT_COLS = 512
ATT_ONES_ROWS = 16


def _prompt_attn_kernel(q_ref, k_ref, v_ref, o_ref, vt_ref, st_ref):
    bq = ATT_BQ
    seq = k_ref.shape[0]

    for r in range(seq // ATT_VT_COLS):
        rs = slice(r * ATT_VT_COLS, (r + 1) * ATT_VT_COLS)
        vt = v_ref[rs, :].astype(F32).T.astype(BF16)
        for hh in range(2):
            vt_ref[hh, :V_DIM, rs] = vt[hh * V_DIM:(hh + 1) * V_DIM]
            vt_ref[hh, V_DIM:, rs] = jnp.ones((ATT_ONES_ROWS, ATT_VT_COLS), BF16)

    def query_block(i, _):
        rows = pl.ds(pl.multiple_of(i * bq, bq), bq)
        qts = [q_ref[rows, hh * HEAD_PAD:(hh + 1) * HEAD_PAD].astype(F32).T.astype(BF16) for hh in range(2)]

        def scores(j, slot):
            start = pl.multiple_of(j * ATT_BK, ATT_BK)
            for hh in range(2):
                st_ref[slot, hh] = _dot(k_ref[pl.ds(start, ATT_BK), hh * HEAD_PAD:(hh + 1) * HEAD_PAD],
                                        qts[hh])

        def update(j, slot, carry, masked):
            start = pl.multiple_of(j * ATT_BK, ATT_BK)
            out = []
            for hh in range(2):
                m, acc = carry[hh]
                st = st_ref[slot, hh]
                if masked:
                    key = start + lax.broadcasted_iota(jnp.int32, st.shape, 0)
                    qry = i * bq + lax.broadcasted_iota(jnp.int32, st.shape, 1)
                    st = jnp.where(key <= qry, st, -1e30)
                m_new = jnp.maximum(m, jnp.max(st, axis=0, keepdims=True))
                p = jnp.exp2(st - m_new)
                acc = jnp.exp2(m - m_new) * acc + _dot(vt_ref[hh, :, pl.ds(start, ATT_BK)], p.astype(BF16))
                out.append((m_new, acc))
            return tuple(out)

        one = (jnp.full((1, bq), -1e30, F32), jnp.zeros((V_DIM + ATT_ONES_ROWS, bq), F32))

        def pair(t, carry):
            scores(2 * t + 1, 1)
            carry = update(2 * t, 0, carry, False)
            scores(2 * t + 2, 0)
            return update(2 * t + 1, 1, carry, False)

        scores(0, 0)
        carry = lax.fori_loop(0, i, pair, (one, one))
        scores(2 * i + 1, 1)
        carry = update(2 * i, 0, carry, True)
        carry = update(2 * i + 1, 1, carry, True)
        ot = jnp.concatenate([acc[:V_DIM] / acc[V_DIM:V_DIM + 1] for (_, acc) in carry], axis=0)
        o_ref[rows, :] = ot.T.astype(o_ref.dtype)
        return 0

    lax.fori_loop(0, seq // bq, query_block, 0)


def _prompt_attn(q, k, v, batch, seq):
    assert ATT_BQ == 2 * ATT_BK and seq % ATT_BQ == 0 and seq % ATT_VT_COLS == 0
    n_tok = q.shape[0]
    return pl.pallas_call(
        _prompt_attn_kernel,
        grid=(batch, N_HEADS // 2),
        in_specs=[pl.BlockSpec((seq, 2 * HEAD_PAD), lambda b, hp: (b, hp)),
                  pl.BlockSpec((seq, 2 * HEAD_PAD), lambda b, hp: (b, hp)),
                  pl.BlockSpec((seq, 2 * V_DIM), lambda b, hp: (b, hp))],
        out_specs=pl.BlockSpec((seq, 2 * V_DIM), lambda b, hp: (b, hp)),
        out_shape=jax.ShapeDtypeStruct((n_tok, N_HEADS * V_DIM), BF16),
        scratch_shapes=[pltpu.VMEM((2, V_DIM + ATT_ONES_ROWS, seq), BF16),
                        pltpu.VMEM((2, 2, ATT_BK, ATT_BQ), F32)],
        compiler_params=pltpu.CompilerParams(dimension_semantics=("arbitrary",) * 2,
                                             vmem_limit_bytes=VMEM_LIMIT),
        name="prompt_attn",
    )(q, k, v)


SAMPLE_SUB = 1024
SAMPLE_SEQS = 2


def _qabs_kernel(q_ref, gk_ref, wuk_ref, qlat_ref, qr_ref):
    for h in range(N_HEADS):
        qh = q_ref[:, h * HEAD_PAD:(h + 1) * HEAD_PAD] * gk_ref[...]
        qlat_ref[h] = _dot(qh[:, :QK_NOPE].astype(BF16), wuk_ref[h])
        qr_ref[h] = qh[:, QK_NOPE:QK_HEAD]


def _qabs_call(q, gk, wuk):
    n_tok = q.shape[0]
    kv_lora = wuk.shape[2]
    return pl.pallas_call(
        _qabs_kernel,
        out_shape=[jax.ShapeDtypeStruct((N_HEADS, n_tok, kv_lora), F32),
                   jax.ShapeDtypeStruct((N_HEADS, n_tok, QK_ROPE), F32)],
        compiler_params=pltpu.CompilerParams(vmem_limit_bytes=VMEM_LIMIT),
        name="q_absorb",
    )(q, gk, wuk)


def _sample_attn_kernel(layer, pt_ref, qlat_ref, qr_ref, cnew_ref, krnew_ref, wukt_ref, ckv_hbm, krt_hbm,
                        olat_ref, cbuf, kbuf, csem, ksem, lhs_ref, cb_ref):
    s = pl.program_id(0)
    n_steps = pl.num_programs(0)
    n_par = cnew_ref.shape[0]
    past = cbuf.shape[1] // n_par
    n_pages = past // PAGE
    t_new = cnew_ref.shape[1]
    kv_lora = cnew_ref.shape[2]
    n_rows = N_HEADS * t_new
    n_kn = N_HEADS * QK_NOPE
    slot = s % 2

    def copies(step, slot_):
        out = []
        for a in range(n_par):
            for p in range(n_pages):
                page = pt_ref[(step * n_par + a) * n_pages + p]
                keys = pl.ds(a * past + p * PAGE, PAGE)
                out.append(pltpu.make_async_copy(ckv_hbm.at[layer, page], cbuf.at[slot_, keys, :],
                                                 csem.at[slot_]))
                out.append(pltpu.make_async_copy(krt_hbm.at[layer, page], kbuf.at[slot_, :, keys],
                                                 ksem.at[slot_]))
        return out

    def start_all(cps):
        for n, cp in enumerate(cps):
            cp.start(priority=(n // 2) % 2)

    @pl.when(s == 0)
    def _():
        for a in range(n_par):
            lhs_ref[a, :n_kn, :] = wukt_ref[...]
        start_all(copies(s, slot))

    for cp in copies(s, slot):
        cp.wait()

    qrs = []
    for a in range(n_par):
        tok = slice(a * t_new, (a + 1) * t_new)
        lhs_ref[a, n_kn:, :] = qlat_ref[:, tok, :].reshape(n_rows, kv_lora).astype(BF16)
        qrs.append(qr_ref[:, tok, :].reshape(n_rows, QK_ROPE).astype(BF16))

    def scaled_scores(big, rope_scores, r2):
        n = big.shape[1]
        kn = big[:n_kn]
        ss = jnp.sum((kn * kn).reshape(N_HEADS, QK_NOPE, n), axis=1)
        rs = lax.rsqrt((ss + r2) * (1.0 / QK_HEAD) + EPS)
        sc = (big[n_kn:] + rope_scores).reshape(N_HEADS, t_new, n) * rs[:, None, :]
        return sc.reshape(n_rows, n)

    def past_scores(a, k):
        keys = slice(a * past + k * SAMPLE_SUB, a * past + (k + 1) * SAMPLE_SUB)
        cb = cbuf[slot, keys, :].astype(BF16)
        cb_ref[a, k % 2] = cb
        krt = kbuf[slot, :, keys]
        r2 = jnp.sum(krt * krt, axis=0, keepdims=True)
        return scaled_scores(_dot_nt(lhs_ref[a], cb), _dot(qrs[a], krt.astype(BF16)), r2)

    def new_scores(a):
        pad = PAGE - t_new
        cb_new = jnp.concatenate([cnew_ref[a], jnp.zeros((pad, kv_lora), F32)], axis=0).astype(BF16)
        kr_new = jnp.concatenate([krnew_ref[a], jnp.zeros((pad, QK_ROPE), F32)], axis=0)
        kr2 = kr_new * kr_new
        kr2_hi = kr2.astype(BF16)
        kr2_lo = (kr2 - kr2_hi.astype(F32)).astype(BF16)
        ones = jnp.ones((8, QK_ROPE), BF16)
        r2_new = (_dot_nt(ones, kr2_hi) + _dot_nt(ones, kr2_lo))[:1]
        sc = scaled_scores(_dot_nt(lhs_ref[a], cb_new), _dot_nt(qrs[a], kr_new.astype(BF16)), r2_new)
        row = lax.broadcasted_iota(jnp.int32, (n_rows, PAGE), 0)
        col = lax.broadcasted_iota(jnp.int32, (n_rows, PAGE), 1)
        return jnp.where(col <= row % t_new, sc, -1e30), cb_new

    n_sub = past // SAMPLE_SUB
    new = [new_scores(a) for a in range(n_par)]
    sc_next = [past_scores(a, 0) for a in range(n_par)]

    nxt = jnp.minimum(s + 1, n_steps - 1)
    start_all(copies(nxt, 1 - slot))

    state = []
    for a in range(n_par):
        sc_new, cb_new = new[a]
        m = jnp.max(sc_new, axis=-1, keepdims=True)
        p = jnp.exp2(sc_new - m)
        state.append((m, jnp.sum(p, axis=-1, keepdims=True), _dot(p.astype(BF16), cb_new)))
    for k in range(n_sub):
        sc_cur = sc_next
        if k + 1 < n_sub:
            sc_next = [past_scores(a, k + 1) for a in range(n_par)]
        for a in range(n_par):
            m, l, acc = state[a]
            m_new = jnp.maximum(m, jnp.max(sc_cur[a], axis=-1, keepdims=True))
            alpha = jnp.exp2(m - m_new)
            p = jnp.exp2(sc_cur[a] - m_new)
            l = alpha * l + jnp.sum(p, axis=-1, keepdims=True)
            acc = alpha * acc + _dot(p.astype(BF16), cb_ref[a, k % 2])
            state[a] = (m_new, l, acc)
    for a in range(n_par):
        _, l, acc = state[a]
        olat_ref[a] = acc / l

    @pl.when(s == n_steps - 1)
    def _():
        for cp in copies(nxt, 1 - slot):
            cp.wait()


def _sample_attn(layer, page_table, qlat, qr, c_new, kr_new, wukt, cache_ckv, cache_krt):
    n_seq, n_pages = page_table.shape
    t_new = c_new.shape[1]
    kv_lora = c_new.shape[2]
    n_rows = N_HEADS * t_new
    past = n_pages * PAGE
    n_par = SAMPLE_SEQS
    assert past % SAMPLE_SUB == 0 and n_seq % n_par == 0
    grid_spec = pltpu.PrefetchScalarGridSpec(
        num_scalar_prefetch=1,
        grid=(n_seq // n_par,),
        in_specs=[pl.BlockSpec((N_HEADS, n_par * t_new, kv_lora), lambda s, pt: (0, s, 0)),
                  pl.BlockSpec((N_HEADS, n_par * t_new, QK_ROPE), lambda s, pt: (0, s, 0)),
                  pl.BlockSpec((n_par, t_new, kv_lora), lambda s, pt: (s, 0, 0)),
                  pl.BlockSpec((n_par, t_new, QK_ROPE), lambda s, pt: (s, 0, 0)),
                  pl.BlockSpec(wukt.shape, lambda s, pt: (0, 0)),
                  pl.BlockSpec(memory_space=pl.ANY),
                  pl.BlockSpec(memory_space=pl.ANY)],
        out_specs=pl.BlockSpec((n_par, n_rows, kv_lora), lambda s, pt: (s, 0, 0)),
        scratch_shapes=[pltpu.VMEM((2, n_par * past, kv_lora), F32),
                        pltpu.VMEM((2, QK_ROPE, n_par * past), F32),
                        pltpu.SemaphoreType.DMA((2,)),
                        pltpu.SemaphoreType.DMA((2,)),
                        pltpu.VMEM((n_par, N_HEADS * QK_NOPE + n_rows, kv_lora), BF16),
                        pltpu.VMEM((n_par, 2, SAMPLE_SUB, kv_lora), BF16)],
    )
    return pl.pallas_call(
        functools.partial(_sample_attn_kernel, layer),
        grid_spec=grid_spec,
        out_shape=jax.ShapeDtypeStruct((n_seq, n_rows, kv_lora), F32),
        compiler_params=pltpu.CompilerParams(dimension_semantics=("arbitrary",),
                                             vmem_limit_bytes=VMEM_LIMIT),
        name="sample_attn",
    )(page_table.reshape(-1), qlat, qr, c_new, kr_new, wukt, cache_ckv, cache_krt)


def _oproj_kernel(olat_ref, wuv_ref, o_ref):
    n_seq, n_rows, kv_lora = olat_ref.shape
    t_new = n_rows // N_HEADS
    acc = None
    for h in range(N_HEADS):
        x = olat_ref[:, h * t_new:(h + 1) * t_new, :].reshape(n_seq * t_new, kv_lora).astype(BF16)
        part = _dot(x, wuv_ref[h])
        acc = part if acc is None else acc + part
    o_ref[...] = acc.astype(o_ref.dtype)


def _oproj_call(olat, wuv):
    n_seq, n_rows, _ = olat.shape
    return pl.pallas_call(
        _oproj_kernel,
        out_shape=jax.ShapeDtypeStruct((n_seq * n_rows // N_HEADS, N_HEADS * V_DIM), BF16),
        compiler_params=pltpu.CompilerParams(vmem_limit_bytes=VMEM_LIMIT),
        name="o_proj",
    )(olat, wuv)


def _pad_heads(w, width):
    k = w.shape[0]
    w = w.reshape(k, N_HEADS, width)
    return jnp.pad(w, ((0, 0), (0, 0), (0, HEAD_PAD - width))).reshape(k, N_HEADS * HEAD_PAD)


def _rot_cols(w):
    half = w.shape[-1] // 2
    return jnp.concatenate([-w[..., half:], w[..., :half]], axis=-1)


def _rope_block(w):
    return jnp.pad(w, ((0, 0), (QK_NOPE, HEAD_PAD - QK_HEAD)))


def _head_gain(g):
    return jnp.pad(g, (0, HEAD_PAD - QK_HEAD)).reshape(1, HEAD_PAD)


def _rope_tables(n_pos, offset):
    pos = jnp.arange(n_pos, dtype=F32) + offset
    inv_freq = ROPE_THETA ** (-jnp.arange(0, QK_ROPE, 2, dtype=F32) / QK_ROPE)
    ang = pos[:, None] * inv_freq[None, :]
    cos, sin = jnp.cos(ang), jnp.sin(ang)
    ones = jnp.ones((n_pos, QK_NOPE), F32)
    tail = jnp.ones((n_pos, HEAD_PAD - QK_HEAD), F32)
    cos_tab = jnp.concatenate([ones, cos, cos, tail], axis=1)
    sin_tab = jnp.concatenate([0 * ones, sin, sin, 0 * tail], axis=1)
    return cos_tab, sin_tab


def _layer_weights(i, ffn1_norm, ffn1_w_in, ffn1_w_out, mix_norm, w_in, qa_norm, kva_norm, w_uq, w_ukv,
                   q_norm, k_norm, sgu_ln_g, sgu_ln_b, sgu_w_s, sgu_b_s, w_a_out, w_b_out, w_o,
                   ffn2_norm, ffn2_w_in, ffn2_w_out, ple_norm, ple_w_gate, ple_w_proj):
    a_width = sgu_ln_g.shape[1]
    q_lora = qa_norm.shape[1]
    kv_lora = kva_norm.shape[1]
    s1 = 2 * a_width
    s2 = s1 + q_lora
    s3 = s2 + kv_lora
    s4 = s3 + QK_ROPE
    wi = w_in[i]
    w_kr = wi[:, s3:s4]
    wpre = jnp.concatenate([wi[:, :s3], _rope_block(w_kr), _rope_block(_rot_cols(w_kr))], axis=1)

    wq = w_uq[i].reshape(q_lora, N_HEADS, QK_HEAD)
    wq_rot = jnp.concatenate([jnp.zeros_like(wq[..., :QK_NOPE]), _rot_cols(wq[..., QK_NOPE:])], axis=-1)
    wuq = jnp.stack([_pad_heads(wq.reshape(q_lora, -1), QK_HEAD).reshape(q_lora, N_HEADS, HEAD_PAD),
                     _pad_heads(wq_rot.reshape(q_lora, -1), QK_HEAD).reshape(q_lora, N_HEADS, HEAD_PAD)],
                    axis=2).reshape(q_lora, 2 * N_HEADS * HEAD_PAD)

    wkv = w_ukv[i].reshape(kv_lora, N_HEADS, QK_NOPE + V_DIM)
    w_uk = wkv[..., :QK_NOPE]
    w_uv = wkv[..., QK_NOPE:]
    wukv = jnp.concatenate([_pad_heads(w_uk.reshape(kv_lora, -1), QK_NOPE),
                            w_uv.reshape(kv_lora, -1)], axis=1)
    wuk_t = jnp.transpose(w_uk, (1, 2, 0))
    eye = jnp.eye(N_HEADS, dtype=F32)
    wuv_blk = jnp.einsum('khd,hg->hkgd', w_uv, eye).reshape(N_HEADS, kv_lora, N_HEADS * V_DIM)

    row = lambda g: g[i].reshape(1, -1)
    return {
        'f1g': row(ffn1_norm), 'f1in': ffn1_w_in[i].astype(BF16), 'f1out': ffn1_w_out[i].astype(BF16),
        'mixg': row(mix_norm), 'wpre': wpre.astype(BF16), 'wgates': wi[:, s4:].astype(BF16),
        'qag': row(qa_norm), 'kvag': row(kva_norm), 'wuq': wuq.astype(BF16), 'wukv': wukv.astype(BF16),
        'qg': _head_gain(q_norm[i]), 'kg': _head_gain(k_norm[i]),
        'wuk_t': wuk_t.astype(BF16), 'wuv_blk': wuv_blk.astype(BF16),
        'lng': row(sgu_ln_g), 'lnb': row(sgu_ln_b), 'w_s': sgu_w_s[i], 'b_s': sgu_b_s[i],
        'wa': w_a_out[i].astype(BF16), 'wb': w_b_out[i].astype(BF16), 'wo': w_o[i].astype(BF16),
        'f2g': row(ffn2_norm), 'f2in': ffn2_w_in[i].astype(BF16), 'f2out': ffn2_w_out[i].astype(BF16),
        'pleg': row(ple_norm), 'plewg': ple_w_gate[i].astype(BF16), 'plewp': ple_w_proj[i].astype(BF16),
    }


def _mix_tables(w_s, b_s, period, group_dim):
    r = jnp.arange(CHUNK)
    same = (r[:, None] // period) == (r[None, :] // period)
    causal = (r[None, :] % period) <= (r[:, None] % period)
    reps = CHUNK // period
    mixw = jnp.where((same & causal)[None], jnp.tile(w_s[:, :period, :period], (1, reps, reps)), 0.0)
    bias = jnp.tile(b_s[:, :period], (1, reps))
    mixb = jnp.repeat(bias.T, group_dim, axis=1)
    return mixw.astype(BF16), mixb


def kernel(x_prompt, x_sample, cache_ckv, cache_krope, page_table, p_prompt, p_sample, ffn1_norm, ffn1_w_in, ffn1_w_out, mix_norm, w_in, qa_norm, kva_norm, w_uq, w_ukv, q_norm, k_norm, sgu_ln_g, sgu_ln_b, sgu_w_s, sgu_b_s, w_a_out, w_b_out, w_o, ffn2_norm, ffn2_w_in, ffn2_w_out, ple_norm, ple_w_gate, ple_w_proj):
    depth = w_in.shape[0]
    batch, seq, d_model = x_prompt.shape
    dec_batch, dec_seq, _ = x_sample.shape
    past_len = page_table.shape[1] * cache_ckv.shape[2]
    assert seq % TOKEN_TILE == 0 and TOKEN_TILE % dec_seq == 0 and CHUNK % dec_seq == 0
    assert cache_ckv.shape[2] == PAGE

    yp = x_prompt.reshape(batch * seq, d_model)
    ys = x_sample.reshape(dec_batch * dec_seq, d_model)
    cos_p, sin_p = _rope_tables(seq, 0)
    cos_s, sin_s = _rope_tables(dec_seq, past_len)
    reps = TOKEN_TILE // dec_seq
    cos_s, sin_s = jnp.tile(cos_s, (reps, 1)), jnp.tile(sin_s, (reps, 1))

    outs = [[] for _ in range(5)]
    for i in range(depth):
        w = _layer_weights(i, ffn1_norm, ffn1_w_in, ffn1_w_out, mix_norm, w_in, qa_norm, kva_norm, w_uq,
                           w_ukv, q_norm, k_norm, sgu_ln_g, sgu_ln_b, sgu_w_s, sgu_b_s, w_a_out, w_b_out,
                           w_o, ffn2_norm, ffn2_w_in, ffn2_w_out, ple_norm, ple_w_gate, ple_w_proj)

        group_dim = w['lng'].shape[1] // A_GROUPS
        w['mixw'], w['mixb'] = _mix_tables(w['w_s'], w['b_s'], CHUNK, group_dim)
        h1, ain, q, ckv_p, kr_p, k, v = _pre_call(yp, cos_p, sin_p, w, True, False, BF16)
        o = _prompt_attn(q, k, v, batch, seq)
        yp = _post_call(h1, ain, o, p_prompt[i].reshape(batch * seq, -1), w)

        w['mixw'], w['mixb'] = _mix_tables(w['w_s'], w['b_s'], dec_seq, group_dim)
        h1, ain, q, ckv_s, kr_s, vn_s = _pre_call(ys, cos_s, sin_s, w, False, True, F32)
        qlat, qr = _qabs_call(q, w['kg'], w['wuk_t'])
        kv_lora = ckv_s.shape[1]
        olat = _sample_attn(i, page_table, qlat, qr,
                            ckv_s.reshape(dec_batch, dec_seq, kv_lora),
                            kr_s.reshape(dec_batch, dec_seq, QK_ROPE),
                            w['wuk_t'].reshape(N_HEADS * QK_NOPE, kv_lora),
                            cache_ckv, jnp.swapaxes(cache_krope, 2, 3))
        o = _oproj_call(olat, w['wuv_blk'])
        ys = _post_call(h1, ain, o, p_sample[i].reshape(dec_batch * dec_seq, -1), w)

        outs[0].append(ckv_p.reshape(batch, seq, -1))
        outs[1].append(kr_p.reshape(batch, seq, -1))
        outs[2].append(ckv_s.reshape(dec_batch, dec_seq, -1))
        outs[3].append(kr_s.reshape(dec_batch, dec_seq, -1))
        outs[4].append(vn_s.reshape(dec_batch, dec_seq, -1))

    return (yp.reshape(batch, seq, d_model), ys.reshape(dec_batch, dec_seq, d_model),
            jnp.stack(outs[0]), jnp.stack(outs[1]), jnp.stack(outs[2]), jnp.stack(outs[3]),
            jnp.stack(outs[4]))
```

```python
import functools

import jax
import jax.numpy as jnp
from jax import lax
from jax.experimental import pallas as pl
from jax.experimental.pallas import tpu as pltpu

F32 = jnp.float32
BF16 = jnp.bfloat16

EPS = 1e-6
N_HEADS = 8
QK_NOPE = 64
QK_ROPE = 32
QK_HEAD = QK_NOPE + QK_ROPE
V_DIM = 64
HEAD_PAD = 128
A_GROUPS = 4
CHUNK = 128
ROPE_THETA = 10000.0
PAGE = 128
LOG2_E = 1.4426950408889634

FF_CHUNK = 256
TOKEN_TILE = 512
VMEM_LIMIT = 56 * 1024 * 1024

_NT = (((1,), (1,)), ((), ()))


def _dot(a, b):
    return jnp.dot(a, b, preferred_element_type=F32)


def _dot_nt(a, b):
    return lax.dot_general(a, b, _NT, preferred_element_type=F32)


def _rms(x, g):
    return x * lax.rsqrt(jnp.mean(x * x, axis=-1, keepdims=True) + EPS) * g


def _ffn_half(x, g_ref, win_ref, wout_ref, act_ref):
    d_ff = wout_ref.shape[0]
    xn = _rms(x, g_ref[...]).astype(BF16)
    for c in range(d_ff // FF_CHUNK):
        lo = c * FF_CHUNK
        a = _dot(xn, win_ref[:, lo:lo + FF_CHUNK])
        b = _dot(xn, win_ref[:, d_ff + lo:d_ff + lo + FF_CHUNK])
        act_ref[:, lo:lo + FF_CHUNK] = (a * jax.nn.sigmoid(a) * b).astype(BF16)
    return x + 0.5 * _dot(act_ref[...], wout_ref[...])


def _head_norm(xh, gain):
    ss = jnp.sum(xh * xh, axis=-1, keepdims=True)
    return xh * lax.rsqrt(ss * (1.0 / QK_HEAD) + EPS) * gain


def _pre_kernel(expand_kv, write_vn, *refs):
    (x_ref, cos_ref, sin_ref, f1g_ref, f1in_ref, f1out_ref, mixg_ref, wpre_ref, qag_ref, kvag_ref,
     wuq_ref, qg_ref, lng_ref, lnb_ref, mixw_ref, mixb_ref) = refs[:16]
    pos = 16
    if expand_kv:
        wukv_ref, kg_ref = refs[pos:pos + 2]
        pos += 2
    h1_ref, ain_ref, q_ref, ckv_ref, kr_ref = refs[pos:pos + 5]
    pos += 5
    if expand_kv:
        k_ref, v_ref = refs[pos:pos + 2]
        pos += 2
    if write_vn:
        vn_ref = refs[pos]
        pos += 1
    act_ref = refs[pos]

    a_width = lng_ref.shape[1]
    q_lora = qag_ref.shape[1]
    kv_lora = kvag_ref.shape[1]
    rows = x_ref.shape[0]

    h1 = _ffn_half(x_ref[...], f1g_ref, f1in_ref, f1out_ref, act_ref)
    h1_ref[...] = h1
    n = _rms(h1, mixg_ref[...]).astype(BF16)

    c0 = 2 * a_width
    c1 = c0 + q_lora
    c2 = c1 + kv_lora
    nq = N_HEADS * HEAD_PAD
    p_cq = _dot(n, wpre_ref[:, c0:c1])
    p_ckv = _dot(n, wpre_ref[:, c1:c2])
    p_kr = _dot(n, wpre_ref[:, c2:c2 + 2 * HEAD_PAD])
    p_uv = _dot(n, wpre_ref[:, :c0])

    uv = jax.nn.gelu(p_uv)
    u, v = uv[:, :a_width], uv[:, a_width:]
    vc = v - jnp.mean(v, axis=-1, keepdims=True)
    vn = vc * lax.rsqrt(jnp.mean(vc * vc, axis=-1, keepdims=True) + EPS) * lng_ref[...] + lnb_ref[...]
    if write_vn:
        vn_ref[...] = vn
    vnb = vn.astype(BF16)

    cos = cos_ref[...]
    sin = sin_ref[...]
    cq = _rms(p_cq, qag_ref[...]).astype(BF16)
    scale = QK_HEAD ** -0.5 * LOG2_E
    for h in range(N_HEADS):
        t = _dot(cq, wuq_ref[:, 2 * h * HEAD_PAD:2 * (h + 1) * HEAD_PAD])
        qh = t[:, :HEAD_PAD] * cos + t[:, HEAD_PAD:] * sin
        q_ref[:, h * HEAD_PAD:(h + 1) * HEAD_PAD] = (_head_norm(qh, qg_ref[...]) * scale).astype(q_ref.dtype)
    c_kv = _rms(p_ckv, kvag_ref[...])
    ckv_ref[...] = c_kv
    kr_blk = p_kr[:, :HEAD_PAD] * cos + p_kr[:, HEAD_PAD:] * sin
    kr_ref[...] = kr_blk[:, QK_NOPE:QK_HEAD]
    if expand_kv:
        ckvb = c_kv.astype(BF16)
        for hp in range(N_HEADS // 2):
            t = _dot(ckvb, wukv_ref[:, 2 * hp * HEAD_PAD:2 * (hp + 1) * HEAD_PAD])
            for hh in range(2):
                hs = slice((2 * hp + hh) * HEAD_PAD, (2 * hp + hh + 1) * HEAD_PAD)
                kh = t[:, hh * HEAD_PAD:(hh + 1) * HEAD_PAD] + kr_blk
                k_ref[:, hs] = _head_norm(kh, kg_ref[...]).astype(k_ref.dtype)
        v_ref[...] = _dot(ckvb, wukv_ref[:, nq:]).astype(v_ref.dtype)

    gd = a_width // A_GROUPS
    for r in range(rows // CHUNK):
        rs = slice(r * CHUNK, (r + 1) * CHUNK)
        for g in range(A_GROUPS):
            gs = slice(g * gd, (g + 1) * gd)
            z = _dot(mixw_ref[g], vnb[rs, gs]) + mixb_ref[:, gs]
            ain_ref[rs, gs] = (u[rs, gs] * z).astype(ain_ref.dtype)


def _const_spec(shape):
    nd = len(shape)
    return pl.BlockSpec(shape, lambda i, _nd=nd: (0,) * _nd, pipeline_mode=pl.Buffered(1))


def _pre_call(x, cos_tab, sin_tab, w, expand_kv, write_vn, q_dtype):
    n_tok, d_model = x.shape
    tm = TOKEN_TILE
    n_tab = cos_tab.shape[0] // tm
    a_width = w['lng'].shape[1]
    kv_lora = w['kvag'].shape[1]
    d_ff = w['f1out'].shape[0]
    nq = N_HEADS * HEAD_PAD

    def row_spec(cols):
        return pl.BlockSpec((tm, cols), lambda i: (i, 0))

    tab_spec = pl.BlockSpec((tm, HEAD_PAD), lambda i: (i % n_tab, 0))
    consts = [w['f1g'], w['f1in'], w['f1out'], w['mixg'], w['wpre'], w['qag'], w['kvag'], w['wuq'],
              w['qg'], w['lng'], w['lnb'], w['mixw'], w['mixb']]
    if expand_kv:
        consts += [w['wukv'], w['kg']]
    in_specs = [row_spec(d_model), tab_spec, tab_spec] + [_const_spec(c.shape) for c in consts]
    out_shape = [jax.ShapeDtypeStruct((n_tok, d_model), F32),
                 jax.ShapeDtypeStruct((n_tok, a_width), BF16),
                 jax.ShapeDtypeStruct((n_tok, nq), q_dtype),
                 jax.ShapeDtypeStruct((n_tok, kv_lora), F32),
                 jax.ShapeDtypeStruct((n_tok, QK_ROPE), F32)]
    out_specs = [row_spec(d_model), row_spec(a_width), row_spec(nq), row_spec(kv_lora), row_spec(QK_ROPE)]
    if expand_kv:
        out_shape += [jax.ShapeDtypeStruct((n_tok, nq), BF16),
                      jax.ShapeDtypeStruct((n_tok, N_HEADS * V_DIM), BF16)]
        out_specs += [row_spec(nq), row_spec(N_HEADS * V_DIM)]
    if write_vn:
        out_shape.append(jax.ShapeDtypeStruct((n_tok, a_width), F32))
        out_specs.append(row_spec(a_width))
    return pl.pallas_call(
        functools.partial(_pre_kernel, expand_kv, write_vn),
        grid=(n_tok // tm,),
        in_specs=in_specs,
        out_specs=out_specs,
        out_shape=out_shape,
        scratch_shapes=[pltpu.VMEM((tm, d_ff), BF16)],
        compiler_params=pltpu.CompilerParams(dimension_semantics=("arbitrary",),
                                             vmem_limit_bytes=VMEM_LIMIT),
        name="pre_expand" if expand_kv else "pre_latent",
    )(x, cos_tab, sin_tab, *consts)


def _post_kernel(h1_ref, ain_ref, o_ref, p_ref, mixg_ref, wg_ref, wa_ref, wb_ref, wo_ref,
                 f2g_ref, f2in_ref, f2out_ref, pleg_ref, plewg_ref, plewp_ref, y_ref, act_ref):
    d_model = h1_ref.shape[1]
    h1 = h1_ref[...]
    n = _rms(h1, mixg_ref[...]).astype(BF16)
    ga = jax.nn.sigmoid(_dot(n, wg_ref[:, :d_model]))
    gb = jax.nn.sigmoid(_dot(n, wg_ref[:, d_model:]))
    m = ga * _dot(ain_ref[...], wa_ref[...]) + gb * _dot(o_ref[...], wb_ref[...])
    h2 = h1 + _dot(m.astype(BF16), wo_ref[...])
    h3 = _ffn_half(h2, f2g_ref, f2in_ref, f2out_ref, act_ref)
    hn = _rms(h3, pleg_ref[...]).astype(BF16)
    gate = jax.nn.sigmoid(_dot(hn, plewg_ref[...]))
    y_ref[...] = h3 + gate * _dot(p_ref[...].astype(BF16), plewp_ref[...])


def _post_call(h1, ain, o, p, w):
    n_tok, d_model = h1.shape
    tm = TOKEN_TILE
    d_ff = w['f2out'].shape[0]

    def row_spec(cols):
        return pl.BlockSpec((tm, cols), lambda i: (i, 0))

    consts = [w['mixg'], w['wgates'], w['wa'], w['wb'], w['wo'], w['f2g'], w['f2in'], w['f2out'],
              w['pleg'], w['plewg'], w['plewp']]
    return pl.pallas_call(
        _post_kernel,
        grid=(n_tok // tm,),
        in_specs=[row_spec(d_model), row_spec(ain.shape[1]), row_spec(o.shape[1]), row_spec(p.shape[1])]
        + [_const_spec(c.shape) for c in consts],
        out_specs=row_spec(d_model),
        out_shape=jax.ShapeDtypeStruct((n_tok, d_model), F32),
        scratch_shapes=[pltpu.VMEM((tm, d_ff), BF16)],
        compiler_params=pltpu.CompilerParams(dimension_semantics=("arbitrary",),
                                             vmem_limit_bytes=VMEM_LIMIT),
        name="post",
    )(h1, ain, o, p, *consts)


ATT_BQ = 512
ATT_BK = 256
ATT_VT_COLS = 512
ATT_ONES_ROWS = 16


def _prompt_attn_kernel(q_ref, k_ref, v_ref, o_ref, vt_ref, st_ref):
    bq = ATT_BQ
    seq = k_ref.shape[0]

    for r in range(seq // ATT_VT_COLS):
        rs = slice(r * ATT_VT_COLS, (r + 1) * ATT_VT_COLS)
        vt = v_ref[rs, :].astype(F32).T.astype(BF16)
        for hh in range(2):
            vt_ref[hh, :V_DIM, rs] = vt[hh * V_DIM:(hh + 1) * V_DIM]
            vt_ref[hh, V_DIM:, rs] = jnp.ones((ATT_ONES_ROWS, ATT_VT_COLS), BF16)

    def query_block(i, _):
        rows = pl.ds(pl.multiple_of(i * bq, bq), bq)
        qts = [q_ref[rows, hh * HEAD_PAD:(hh + 1) * HEAD_PAD].astype(F32).T.astype(BF16) for hh in range(2)]

        def scores(j, slot):
            start = pl.multiple_of(j * ATT_BK, ATT_BK)
            for hh in range(2):
                st_ref[slot, hh] = _dot(k_ref[pl.ds(start, ATT_BK), hh * HEAD_PAD:(hh + 1) * HEAD_PAD],
                                        qts[hh])

        def update(j, slot, carry, masked):
            start = pl.multiple_of(j * ATT_BK, ATT_BK)
            out = []
            for hh in range(2):
                m, acc = carry[hh]
                st = st_ref[slot, hh]
                if masked:
                    key = start + lax.broadcasted_iota(jnp.int32, st.shape, 0)
                    qry = i * bq + lax.broadcasted_iota(jnp.int32, st.shape, 1)
                    st = jnp.where(key <= qry, st, -1e30)
                m_new = jnp.maximum(m, jnp.max(st, axis=0, keepdims=True))
                p = jnp.exp2(st - m_new)
                acc = jnp.exp2(m - m_new) * acc + _dot(vt_ref[hh, :, pl.ds(start, ATT_BK)], p.astype(BF16))
                out.append((m_new, acc))
            return tuple(out)

        one = (jnp.full((1, bq), -1e30, F32), jnp.zeros((V_DIM + ATT_ONES_ROWS, bq), F32))

        def pair(t, carry):
            scores(2 * t + 1, 1)
            carry = update(2 * t, 0, carry, False)
            scores(2 * t + 2, 0)
            return update(2 * t + 1, 1, carry, False)

        scores(0, 0)
        carry = lax.fori_loop(0, i, pair, (one, one))
        scores(2 * i + 1, 1)
        carry = update(2 * i, 0, carry, True)
        carry = update(2 * i + 1, 1, carry, True)
        ot = jnp.concatenate([acc[:V_DIM] / acc[V_DIM:V_DIM + 1] for (_, acc) in carry], axis=0)
        o_ref[rows, :] = ot.T.astype(o_ref.dtype)
        return 0

    lax.fori_loop(0, seq // bq, query_block, 0)


def _prompt_attn(q, k, v, batch, seq):
    assert ATT_BQ == 2 * ATT_BK and seq % ATT_BQ == 0 and seq % ATT_VT_COLS == 0
    n_tok = q.shape[0]
    return pl.pallas_call(
        _prompt_attn_kernel,
        grid=(batch, N_HEADS // 2),
        in_specs=[pl.BlockSpec((seq, 2 * HEAD_PAD), lambda b, hp: (b, hp)),
                  pl.BlockSpec((seq, 2 * HEAD_PAD), lambda b, hp: (b, hp)),
                  pl.BlockSpec((seq, 2 * V_DIM), lambda b, hp: (b, hp))],
        out_specs=pl.BlockSpec((seq, 2 * V_DIM), lambda b, hp: (b, hp)),
        out_shape=jax.ShapeDtypeStruct((n_tok, N_HEADS * V_DIM), BF16),
        scratch_shapes=[pltpu.VMEM((2, V_DIM + ATT_ONES_ROWS, seq), BF16),
                        pltpu.VMEM((2, 2, ATT_BK, ATT_BQ), F32)],
        compiler_params=pltpu.CompilerParams(dimension_semantics=("arbitrary",) * 2,
                                             vmem_limit_bytes=VMEM_LIMIT),
        name="prompt_attn",
    )(q, k, v)


SAMPLE_SUB = 1024
SAMPLE_SEQS = 1
SAMPLE_SLOTS = 3


def _qabs_kernel(q_ref, gk_ref, wuk_ref, qlat_ref, qr_ref):
    for h in range(N_HEADS):
        qh = q_ref[:, h * HEAD_PAD:(h + 1) * HEAD_PAD] * gk_ref[...]
        qlat_ref[h] = _dot(qh[:, :QK_NOPE].astype(BF16), wuk_ref[h])
        qr_ref[h] = qh[:, QK_NOPE:QK_HEAD]


def _qabs_call(q, gk, wuk):
    n_tok = q.shape[0]
    kv_lora = wuk.shape[2]
    return pl.pallas_call(
        _qabs_kernel,
        out_shape=[jax.ShapeDtypeStruct((N_HEADS, n_tok, kv_lora), F32),
                   jax.ShapeDtypeStruct((N_HEADS, n_tok, QK_ROPE), F32)],
        compiler_params=pltpu.CompilerParams(vmem_limit_bytes=VMEM_LIMIT),
        name="q_absorb",
    )(q, gk, wuk)


def _sample_attn_kernel(layer, pt_ref, qlat_ref, qr_ref, cnew_ref, krnew_ref, wukt_ref, ckv_hbm, krt_hbm,
                        olat_ref, cbuf, kbuf, csem, ksem, lhs_ref, cb_ref):
    s = pl.program_id(0)
    n_steps = pl.num_programs(0)
    n_par = cnew_ref.shape[0]
    past = cbuf.shape[1] // n_par
    n_pages = past // PAGE
    t_new = cnew_ref.shape[1]
    kv_lora = cnew_ref.shape[2]
    n_rows = N_HEADS * t_new
    n_kn = N_HEADS * QK_NOPE
    n_slots = cbuf.shape[0]
    ahead = n_slots - 1
    slot = s % n_slots

    def copies(step, slot_):
        out = []
        for a in range(n_par):
            for p in range(n_pages):
                page = pt_ref[(step * n_par + a) * n_pages + p]
                keys = pl.ds(a * past + p * PAGE, PAGE)
                out.append(pltpu.make_async_copy(ckv_hbm.at[layer, page], cbuf.at[slot_, keys, :],
                                                 csem.at[slot_]))
                out.append(pltpu.make_async_copy(krt_hbm.at[layer, page], kbuf.at[slot_, :, keys],
                                                 ksem.at[slot_]))
        return out

    def start_all(cps):
        for n, cp in enumerate(cps):
            cp.start(priority=(n // 2) % 2)

    @pl.when(s == 0)
    def _():
        for a in range(n_par):
            lhs_ref[a, :n_kn, :] = wukt_ref[...]
        for d in range(ahead):
            start_all(copies(jnp.minimum(d, n_steps - 1), d))

    for cp in copies(s, slot):
        cp.wait()

    qrs = []
    for a in range(n_par):
        tok = slice(a * t_new, (a + 1) * t_new)
        lhs_ref[a, n_kn:, :] = qlat_ref[:, tok, :].reshape(n_rows, kv_lora).astype(BF16)
        qrs.append(qr_ref[:, tok, :].reshape(n_rows, QK_ROPE).astype(BF16))

    def scaled_scores(big, rope_scores, r2):
        n = big.shape[1]
        kn = big[:n_kn]
        ss = jnp.sum((kn * kn).reshape(N_HEADS, QK_NOPE, n), axis=1)
        rs = lax.rsqrt((ss + r2) * (1.0 / QK_HEAD) + EPS)
        sc = (big[n_kn:] + rope_scores).reshape(N_HEADS, t_new, n) * rs[:, None, :]
        return sc.reshape(n_rows, n)

    def past_scores(a, k):
        keys = slice(a * past + k * SAMPLE_SUB, a * past + (k + 1) * SAMPLE_SUB)
        cb = cbuf[slot, keys, :].astype(BF16)
        cb_ref[a, k % 2] = cb
        krt = kbuf[slot, :, keys]
        r2 = jnp.sum(krt * krt, axis=0, keepdims=True)
        return scaled_scores(_dot_nt(lhs_ref[a], cb), _dot(qrs[a], krt.astype(BF16)), r2)

    def new_scores(a):
        pad = PAGE - t_new
        cb_new = jnp.concatenate([cnew_ref[a], jnp.zeros((pad, kv_lora), F32)], axis=0).astype(BF16)
        kr_new = jnp.concatenate([krnew_ref[a], jnp.zeros((pad, QK_ROPE), F32)], axis=0)
        kr2 = kr_new * kr_new
        kr2_hi = kr2.astype(BF16)
        kr2_lo = (kr2 - kr2_hi.astype(F32)).astype(BF16)
        ones = jnp.ones((8, QK_ROPE), BF16)
        r2_new = (_dot_nt(ones, kr2_hi) + _dot_nt(ones, kr2_lo))[:1]
        sc = scaled_scores(_dot_nt(lhs_ref[a], cb_new), _dot_nt(qrs[a], kr_new.astype(BF16)), r2_new)
        row = lax.broadcasted_iota(jnp.int32, (n_rows, PAGE), 0)
        col = lax.broadcasted_iota(jnp.int32, (n_rows, PAGE), 1)
        return jnp.where(col <= row % t_new, sc, -1e30), cb_new

    n_sub = past // SAMPLE_SUB
    new = [new_scores(a) for a in range(n_par)]
    sc_next = [past_scores(a, 0) for a in range(n_par)]

    nxt = jnp.minimum(s + ahead, n_steps - 1)
    nxt_slot = (s + ahead) % n_slots
    start_all(copies(nxt, nxt_slot))

    state = []
    for a in range(n_par):
        sc_new, cb_new = new[a]
        m = jnp.max(sc_new, axis=-1, keepdims=True)
        p = jnp.exp2(sc_new - m)
        state.append((m, jnp.sum(p, axis=-1, keepdims=True), _dot(p.astype(BF16), cb_new)))
    for k in range(n_sub):
        sc_cur = sc_next
        if k + 1 < n_sub:
            sc_next = [past_scores(a, k + 1) for a in range(n_par)]
        for a in range(n_par):
            m, l, acc = state[a]
            m_new = jnp.maximum(m, jnp.max(sc_cur[a], axis=-1, keepdims=True))
            alpha = jnp.exp2(m - m_new)
            p = jnp.exp2(sc_cur[a] - m_new)
            l = alpha * l + jnp.sum(p, axis=-1, keepdims=True)
            acc = alpha * acc + _dot(p.astype(BF16), cb_ref[a, k % 2])
            state[a] = (m_new, l, acc)
    for a in range(n_par):
        _, l, acc = state[a]
        olat_ref[a] = acc / l

    @pl.when(s == n_steps - 1)
    def _():
        for d in range(1, n_slots):
            for cp in copies(nxt, (s + d) % n_slots):
                cp.wait()


def _sample_attn(layer, page_table, qlat, qr, c_new, kr_new, wukt, cache_ckv, cache_krt):
    n_seq, n_pages = page_table.shape
    t_new = c_new.shape[1]
    kv_lora = c_new.shape[2]
    n_rows = N_HEADS * t_new
    past = n_pages * PAGE
    n_par = SAMPLE_SEQS
    assert past % SAMPLE_SUB == 0 and n_seq % n_par == 0
    grid_spec = pltpu.PrefetchScalarGridSpec(
        num_scalar_prefetch=1,
        grid=(n_seq // n_par,),
        in_specs=[pl.BlockSpec((N_HEADS, n_par * t_new, kv_lora), lambda s, pt: (0, s, 0)),
                  pl.BlockSpec((N_HEADS, n_par * t_new, QK_ROPE), lambda s, pt: (0, s, 0)),
                  pl.BlockSpec((n_par, t_new, kv_lora), lambda s, pt: (s, 0, 0)),
                  pl.BlockSpec((n_par, t_new, QK_ROPE), lambda s, pt: (s, 0, 0)),
                  pl.BlockSpec(wukt.shape, lambda s, pt: (0, 0)),
                  pl.BlockSpec(memory_space=pl.ANY),
                  pl.BlockSpec(memory_space=pl.ANY)],
        out_specs=pl.BlockSpec((n_par, n_rows, kv_lora), lambda s, pt: (s, 0, 0)),
        scratch_shapes=[pltpu.VMEM((SAMPLE_SLOTS, n_par * past, kv_lora), F32),
                        pltpu.VMEM((SAMPLE_SLOTS, QK_ROPE, n_par * past), F32),
                        pltpu.SemaphoreType.DMA((SAMPLE_SLOTS,)),
                        pltpu.SemaphoreType.DMA((SAMPLE_SLOTS,)),
                        pltpu.VMEM((n_par, N_HEADS * QK_NOPE + n_rows, kv_lora), BF16),
                        pltpu.VMEM((n_par, 2, SAMPLE_SUB, kv_lora), BF16)],
    )
    return pl.pallas_call(
        functools.partial(_sample_attn_kernel, layer),
        grid_spec=grid_spec,
        out_shape=jax.ShapeDtypeStruct((n_seq, n_rows, kv_lora), F32),
        compiler_params=pltpu.CompilerParams(dimension_semantics=("arbitrary",),
                                             vmem_limit_bytes=VMEM_LIMIT),
        name="sample_attn",
    )(page_table.reshape(-1), qlat, qr, c_new, kr_new, wukt, cache_ckv, cache_krt)


def _oproj_kernel(olat_ref, wuv_ref, o_ref):
    n_seq, n_rows, kv_lora = olat_ref.shape
    t_new = n_rows // N_HEADS
    acc = None
    for h in range(N_HEADS):
        x = olat_ref[:, h * t_new:(h + 1) * t_new, :].reshape(n_seq * t_new, kv_lora).astype(BF16)
        part = _dot(x, wuv_ref[h])
        acc = part if acc is None else acc + part
    o_ref[...] = acc.astype(o_ref.dtype)


def _oproj_call(olat, wuv):
    n_seq, n_rows, _ = olat.shape
    return pl.pallas_call(
        _oproj_kernel,
        out_shape=jax.ShapeDtypeStruct((n_seq * n_rows // N_HEADS, N_HEADS * V_DIM), BF16),
        compiler_params=pltpu.CompilerParams(vmem_limit_bytes=VMEM_LIMIT),
        name="o_proj",
    )(olat, wuv)


def _pad_heads(w, width):
    k = w.shape[0]
    w = w.reshape(k, N_HEADS, width)
    return jnp.pad(w, ((0, 0), (0, 0), (0, HEAD_PAD - width))).reshape(k, N_HEADS * HEAD_PAD)


def _rot_cols(w):
    half = w.shape[-1] // 2
    return jnp.concatenate([-w[..., half:], w[..., :half]], axis=-1)


def _rope_block(w):
    return jnp.pad(w, ((0, 0), (QK_NOPE, HEAD_PAD - QK_HEAD)))


def _head_gain(g):
    return jnp.pad(g, (0, HEAD_PAD - QK_HEAD)).reshape(1, HEAD_PAD)


def _rope_tables(n_pos, offset):
    pos = jnp.arange(n_pos, dtype=F32) + offset
    inv_freq = ROPE_THETA ** (-jnp.arange(0, QK_ROPE, 2, dtype=F32) / QK_ROPE)
    ang = pos[:, None] * inv_freq[None, :]
    cos, sin = jnp.cos(ang), jnp.sin(ang)
    ones = jnp.ones((n_pos, QK_NOPE), F32)
    tail = jnp.ones((n_pos, HEAD_PAD - QK_HEAD), F32)
    cos_tab = jnp.concatenate([ones, cos, cos, tail], axis=1)
    sin_tab = jnp.concatenate([0 * ones, sin, sin, 0 * tail], axis=1)
    return cos_tab, sin_tab


def _layer_weights(i, ffn1_norm, ffn1_w_in, ffn1_w_out, mix_norm, w_in, qa_norm, kva_norm, w_uq, w_ukv,
                   q_norm, k_norm, sgu_ln_g, sgu_ln_b, sgu_w_s, sgu_b_s, w_a_out, w_b_out, w_o,
                   ffn2_norm, ffn2_w_in, ffn2_w_out, ple_norm, ple_w_gate, ple_w_proj):
    a_width = sgu_ln_g.shape[1]
    q_lora = qa_norm.shape[1]
    kv_lora = kva_norm.shape[1]
    s1 = 2 * a_width
    s2 = s1 + q_lora
    s3 = s2 + kv_lora
    s4 = s3 + QK_ROPE
    wi = w_in[i]
    w_kr = wi[:, s3:s4]
    wpre = jnp.concatenate([wi[:, :s3], _rope_block(w_kr), _rope_block(_rot_cols(w_kr))], axis=1)

    wq = w_uq[i].reshape(q_lora, N_HEADS, QK_HEAD)
    wq_rot = jnp.concatenate([jnp.zeros_like(wq[..., :QK_NOPE]), _rot_cols(wq[..., QK_NOPE:])], axis=-1)
    wuq = jnp.stack([_pad_heads(wq.reshape(q_lora, -1), QK_HEAD).reshape(q_lora, N_HEADS, HEAD_PAD),
                     _pad_heads(wq_rot.reshape(q_lora, -1), QK_HEAD).reshape(q_lora, N_HEADS, HEAD_PAD)],
                    axis=2).reshape(q_lora, 2 * N_HEADS * HEAD_PAD)

    wkv = w_ukv[i].reshape(kv_lora, N_HEADS, QK_NOPE + V_DIM)
    w_uk = wkv[..., :QK_NOPE]
    w_uv = wkv[..., QK_NOPE:]
    wukv = jnp.concatenate([_pad_heads(w_uk.reshape(kv_lora, -1), QK_NOPE),
                            w_uv.reshape(kv_lora, -1)], axis=1)
    wuk_t = jnp.transpose(w_uk, (1, 2, 0))
    eye = jnp.eye(N_HEADS, dtype=F32)
    wuv_blk = jnp.einsum('khd,hg->hkgd', w_uv, eye).reshape(N_HEADS, kv_lora, N_HEADS * V_DIM)

    row = lambda g: g[i].reshape(1, -1)
    return {
        'f1g': row(ffn1_norm), 'f1in': ffn1_w_in[i].astype(BF16), 'f1out': ffn1_w_out[i].astype(BF16),
        'mixg': row(mix_norm), 'wpre': wpre.astype(BF16), 'wgates': wi[:, s4:].astype(BF16),
        'qag': row(qa_norm), 'kvag': row(kva_norm), 'wuq': wuq.astype(BF16), 'wukv': wukv.astype(BF16),
        'qg': _head_gain(q_norm[i]), 'kg': _head_gain(k_norm[i]),
        'wuk_t': wuk_t.astype(BF16), 'wuv_blk': wuv_blk.astype(BF16),
        'lng': row(sgu_ln_g), 'lnb': row(sgu_ln_b), 'w_s': sgu_w_s[i], 'b_s': sgu_b_s[i],
        'wa': w_a_out[i].astype(BF16), 'wb': w_b_out[i].astype(BF16), 'wo': w_o[i].astype(BF16),
        'f2g': row(ffn2_norm), 'f2in': ffn2_w_in[i].astype(BF16), 'f2out': ffn2_w_out[i].astype(BF16),
        'pleg': row(ple_norm), 'plewg': ple_w_gate[i].astype(BF16), 'plewp': ple_w_proj[i].astype(BF16),
    }


def _mix_tables(w_s, b_s, period, group_dim):
    r = jnp.arange(CHUNK)
    same = (r[:, None] // period) == (r[None, :] // period)
    causal = (r[None, :] % period) <= (r[:, None] % period)
    reps = CHUNK // period
    mixw = jnp.where((same & causal)[None], jnp.tile(w_s[:, :period, :period], (1, reps, reps)), 0.0)
    bias = jnp.tile(b_s[:, :period], (1, reps))
    mixb = jnp.repeat(bias.T, group_dim, axis=1)
    return mixw.astype(BF16), mixb


def kernel(x_prompt, x_sample, cache_ckv, cache_krope, page_table, p_prompt, p_sample, ffn1_norm, ffn1_w_in, ffn1_w_out, mix_norm, w_in, qa_norm, kva_norm, w_uq, w_ukv, q_norm, k_norm, sgu_ln_g, sgu_ln_b, sgu_w_s, sgu_b_s, w_a_out, w_b_out, w_o, ffn2_norm, ffn2_w_in, ffn2_w_out, ple_norm, ple_w_gate, ple_w_proj):
    depth = w_in.shape[0]
    batch, seq, d_model = x_prompt.shape
    dec_batch, dec_seq, _ = x_sample.shape
    past_len = page_table.shape[1] * cache_ckv.shape[2]
    assert seq % TOKEN_TILE == 0 and TOKEN_TILE % dec_seq == 0 and CHUNK % dec_seq == 0
    assert cache_ckv.shape[2] == PAGE

    yp = x_prompt.reshape(batch * seq, d_model)
    ys = x_sample.reshape(dec_batch * dec_seq, d_model)
    cos_p, sin_p = _rope_tables(seq, 0)
    cos_s, sin_s = _rope_tables(dec_seq, past_len)
    reps = TOKEN_TILE // dec_seq
    cos_s, sin_s = jnp.tile(cos_s, (reps, 1)), jnp.tile(sin_s, (reps, 1))

    outs = [[] for _ in range(5)]
    for i in range(depth):
        w = _layer_weights(i, ffn1_norm, ffn1_w_in, ffn1_w_out, mix_norm, w_in, qa_norm, kva_norm, w_uq,
                           w_ukv, q_norm, k_norm, sgu_ln_g, sgu_ln_b, sgu_w_s, sgu_b_s, w_a_out, w_b_out,
                           w_o, ffn2_norm, ffn2_w_in, ffn2_w_out, ple_norm, ple_w_gate, ple_w_proj)

        group_dim = w['lng'].shape[1] // A_GROUPS
        w['mixw'], w['mixb'] = _mix_tables(w['w_s'], w['b_s'], CHUNK, group_dim)
        h1, ain, q, ckv_p, kr_p, k, v = _pre_call(yp, cos_p, sin_p, w, True, False, BF16)
        o = _prompt_attn(q, k, v, batch, seq)
        yp = _post_call(h1, ain, o, p_prompt[i].reshape(batch * seq, -1), w)

        w['mixw'], w['mixb'] = _mix_tables(w['w_s'], w['b_s'], dec_seq, group_dim)
        h1, ain, q, ckv_s, kr_s, vn_s = _pre_call(ys, cos_s, sin_s, w, False, True, F32)
        qlat, qr = _qabs_call(q, w['kg'], w['wuk_t'])
        kv_lora = ckv_s.shape[1]
        olat = _sample_attn(i, page_table, qlat, qr,
                            ckv_s.reshape(dec_batch, dec_seq, kv_lora),
                            kr_s.reshape(dec_batch, dec_seq, QK_ROPE),
                            w['wuk_t'].reshape(N_HEADS * QK_NOPE, kv_lora),
                            cache_ckv, jnp.swapaxes(cache_krope, 2, 3))
        o = _oproj_call(olat, w['wuv_blk'])
        ys = _post_call(h1, ain, o, p_sample[i].reshape(dec_batch * dec_seq, -1), w)

        outs[0].append(ckv_p.reshape(batch, seq, -1))
        outs[1].append(kr_p.reshape(batch, seq, -1))
        outs[2].append(ckv_s.reshape(dec_batch, dec_seq, -1))
        outs[3].append(kr_s.reshape(dec_batch, dec_seq, -1))
        outs[4].append(vn_s.reshape(dec_batch, dec_seq, -1))

    return (yp.reshape(batch, seq, d_model), ys.reshape(dec_batch, dec_seq, d_model),
            jnp.stack(outs[0]), jnp.stack(outs[1]), jnp.stack(outs[2]), jnp.stack(outs[3]),
            jnp.stack(outs[4]))
```

```python
import functools

import jax
import jax.numpy as jnp
from jax import lax
from jax.experimental import pallas as pl
from jax.experimental.pallas import tpu as pltpu

F32 = jnp.float32
BF16 = jnp.bfloat16

EPS = 1e-6
N_HEADS = 8
QK_NOPE = 64
QK_ROPE = 32
QK_HEAD = QK_NOPE + QK_ROPE
V_DIM = 64
HEAD_PAD = 128
A_GROUPS = 4
CHUNK = 128
ROPE_THETA = 10000.0
PAGE = 128
LOG2_E = 1.4426950408889634

FF_CHUNK = 256
TOKEN_TILE = 512
VMEM_LIMIT = 56 * 1024 * 1024

_NT = (((1,), (1,)), ((), ()))


def _dot(a, b):
    return jnp.dot(a, b, preferred_element_type=F32)


def _dot_nt(a, b):
    return lax.dot_general(a, b, _NT, preferred_element_type=F32)


def _rms(x, g):
    return x * lax.rsqrt(jnp.mean(x * x, axis=-1, keepdims=True) + EPS) * g


def _ffn_half(x, g_ref, win_ref, wout_ref, act_ref):
    d_ff = wout_ref.shape[0]
    xn = _rms(x, g_ref[...]).astype(BF16)
    for c in range(d_ff // FF_CHUNK):
        lo = c * FF_CHUNK
        a = _dot(xn, win_ref[:, lo:lo + FF_CHUNK])
        b = _dot(xn, win_ref[:, d_ff + lo:d_ff + lo + FF_CHUNK])
        act_ref[:, lo:lo + FF_CHUNK] = (a * jax.nn.sigmoid(a) * b).astype(BF16)
    return x + 0.5 * _dot(act_ref[...], wout_ref[...])


def _head_norm(xh, gain):
    ss = jnp.sum(xh * xh, axis=-1, keepdims=True)
    return xh * lax.rsqrt(ss * (1.0 / QK_HEAD) + EPS) * gain


def _pre_kernel(expand_kv, write_vn, *refs):
    (x_ref, cos_ref, sin_ref, f1g_ref, f1in_ref, f1out_ref, mixg_ref, wpre_ref, qag_ref, kvag_ref,
     wuq_ref, qg_ref, lng_ref, lnb_ref, mixw_ref, mixb_ref) = refs[:16]
    pos = 16
    if expand_kv:
        wukv_ref, kg_ref = refs[pos:pos + 2]
        pos += 2
    h1_ref, ain_ref, q_ref, ckv_ref, kr_ref = refs[pos:pos + 5]
    pos += 5
    if expand_kv:
        k_ref, v_ref = refs[pos:pos + 2]
        pos += 2
    if write_vn:
        vn_ref = refs[pos]
        pos += 1
    act_ref = refs[pos]

    a_width = lng_ref.shape[1]
    q_lora = qag_ref.shape[1]
    kv_lora = kvag_ref.shape[1]
    rows = x_ref.shape[0]

    h1 = _ffn_half(x_ref[...], f1g_ref, f1in_ref, f1out_ref, act_ref)
    h1_ref[...] = h1
    n = _rms(h1, mixg_ref[...]).astype(BF16)

    c0 = 2 * a_width
    c1 = c0 + q_lora
    c2 = c1 + kv_lora
    nq = N_HEADS * HEAD_PAD
    p_cq = _dot(n, wpre_ref[:, c0:c1])
    p_ckv = _dot(n, wpre_ref[:, c1:c2])
    p_kr = _dot(n, wpre_ref[:, c2:c2 + 2 * HEAD_PAD])
    p_uv = _dot(n, wpre_ref[:, :c0])

    uv = jax.nn.gelu(p_uv)
    u, v = uv[:, :a_width], uv[:, a_width:]
    vc = v - jnp.mean(v, axis=-1, keepdims=True)
    vn = vc * lax.rsqrt(jnp.mean(vc * vc, axis=-1, keepdims=True) + EPS) * lng_ref[...] + lnb_ref[...]
    if write_vn:
        vn_ref[...] = vn
    vnb = vn.astype(BF16)

    cos = cos_ref[...]
    sin = sin_ref[...]
    cq = _rms(p_cq, qag_ref[...]).astype(BF16)
    scale = QK_HEAD ** -0.5 * LOG2_E
    for h in range(N_HEADS):
        t = _dot(cq, wuq_ref[:, 2 * h * HEAD_PAD:2 * (h + 1) * HEAD_PAD])
        qh = t[:, :HEAD_PAD] * cos + t[:, HEAD_PAD:] * sin
        q_ref[:, h * HEAD_PAD:(h + 1) * HEAD_PAD] = (_head_norm(qh, qg_ref[...]) * scale).astype(q_ref.dtype)
    c_kv = _rms(p_ckv, kvag_ref[...])
    ckv_ref[...] = c_kv
    kr_blk = p_kr[:, :HEAD_PAD] * cos + p_kr[:, HEAD_PAD:] * sin
    kr_ref[...] = kr_blk[:, QK_NOPE:QK_HEAD]
    if expand_kv:
        ckvb = c_kv.astype(BF16)
        for hp in range(N_HEADS // 2):
            t = _dot(ckvb, wukv_ref[:, 2 * hp * HEAD_PAD:2 * (hp + 1) * HEAD_PAD])
            for hh in range(2):
                hs = slice((2 * hp + hh) * HEAD_PAD, (2 * hp + hh + 1) * HEAD_PAD)
                kh = t[:, hh * HEAD_PAD:(hh + 1) * HEAD_PAD] + kr_blk
                k_ref[:, hs] = _head_norm(kh, kg_ref[...]).astype(k_ref.dtype)
        v_ref[...] = _dot(ckvb, wukv_ref[:, nq:]).astype(v_ref.dtype)

    gd = a_width // A_GROUPS
    for r in range(rows // CHUNK):
        rs = slice(r * CHUNK, (r + 1) * CHUNK)
        for g in range(A_GROUPS):
            gs = slice(g * gd, (g + 1) * gd)
            z = _dot(mixw_ref[g], vnb[rs, gs]) + mixb_ref[:, gs]
            ain_ref[rs, gs] = (u[rs, gs] * z).astype(ain_ref.dtype)


def _const_spec(shape):
    nd = len(shape)
    return pl.BlockSpec(shape, lambda i, _nd=nd: (0,) * _nd, pipeline_mode=pl.Buffered(1))


def _pre_call(x, cos_tab, sin_tab, w, expand_kv, write_vn, q_dtype):
    n_tok, d_model = x.shape
    tm = TOKEN_TILE
    n_tab = cos_tab.shape[0] // tm
    a_width = w['lng'].shape[1]
    kv_lora = w['kvag'].shape[1]
    d_ff = w['f1out'].shape[0]
    nq = N_HEADS * HEAD_PAD

    def row_spec(cols):
        return pl.BlockSpec((tm, cols), lambda i: (i, 0))

    tab_spec = pl.BlockSpec((tm, HEAD_PAD), lambda i: (i % n_tab, 0))
    consts = [w['f1g'], w['f1in'], w['f1out'], w['mixg'], w['wpre'], w['qag'], w['kvag'], w['wuq'],
              w['qg'], w['lng'], w['lnb'], w['mixw'], w['mixb']]
    if expand_kv:
        consts += [w['wukv'], w['kg']]
    in_specs = [row_spec(d_model), tab_spec, tab_spec] + [_const_spec(c.shape) for c in consts]
    out_shape = [jax.ShapeDtypeStruct((n_tok, d_model), F32),
                 jax.ShapeDtypeStruct((n_tok, a_width), BF16),
                 jax.ShapeDtypeStruct((n_tok, nq), q_dtype),
                 jax.ShapeDtypeStruct((n_tok, kv_lora), F32),
                 jax.ShapeDtypeStruct((n_tok, QK_ROPE), F32)]
    out_specs = [row_spec(d_model), row_spec(a_width), row_spec(nq), row_spec(kv_lora), row_spec(QK_ROPE)]
    if expand_kv:
        out_shape += [jax.ShapeDtypeStruct((n_tok, nq), BF16),
                      jax.ShapeDtypeStruct((n_tok, N_HEADS * V_DIM), BF16)]
        out_specs += [row_spec(nq), row_spec(N_HEADS * V_DIM)]
    if write_vn:
        out_shape.append(jax.ShapeDtypeStruct((n_tok, a_width), F32))
        out_specs.append(row_spec(a_width))
    return pl.pallas_call(
        functools.partial(_pre_kernel, expand_kv, write_vn),
        grid=(n_tok // tm,),
        in_specs=in_specs,
        out_specs=out_specs,
        out_shape=out_shape,
        scratch_shapes=[pltpu.VMEM((tm, d_ff), BF16)],
        compiler_params=pltpu.CompilerParams(dimension_semantics=("arbitrary",),
                                             vmem_limit_bytes=VMEM_LIMIT),
        name="pre_expand" if expand_kv else "pre_latent",
    )(x, cos_tab, sin_tab, *consts)


def _post_kernel(h1_ref, ain_ref, o_ref, p_ref, mixg_ref, wg_ref, wa_ref, wb_ref, wo_ref,
                 f2g_ref, f2in_ref, f2out_ref, pleg_ref, plewg_ref, plewp_ref, y_ref, act_ref):
    d_model = h1_ref.shape[1]
    h1 = h1_ref[...]
    n = _rms(h1, mixg_ref[...]).astype(BF16)
    ga = jax.nn.sigmoid(_dot(n, wg_ref[:, :d_model]))
    gb = jax.nn.sigmoid(_dot(n, wg_ref[:, d_model:]))
    m = ga * _dot(ain_ref[...], wa_ref[...]) + gb * _dot(o_ref[...], wb_ref[...])
    h2 = h1 + _dot(m.astype(BF16), wo_ref[...])
    h3 = _ffn_half(h2, f2g_ref, f2in_ref, f2out_ref, act_ref)
    hn = _rms(h3, pleg_ref[...]).astype(BF16)
    gate = jax.nn.sigmoid(_dot(hn, plewg_ref[...]))
    y_ref[...] = h3 + gate * _dot(p_ref[...].astype(BF16), plewp_ref[...])


def _post_call(h1, ain, o, p, w):
    n_tok, d_model = h1.shape
    tm = TOKEN_TILE
    d_ff = w['f2out'].shape[0]

    def row_spec(cols):
        return pl.BlockSpec((tm, cols), lambda i: (i, 0))

    consts = [w['mixg'], w['wgates'], w['wa'], w['wb'], w['wo'], w['f2g'], w['f2in'], w['f2out'],
              w['pleg'], w['plewg'], w['plewp']]
    return pl.pallas_call(
        _post_kernel,
        grid=(n_tok // tm,),
        in_specs=[row_spec(d_model), row_spec(ain.shape[1]), row_spec(o.shape[1]), row_spec(p.shape[1])]
        + [_const_spec(c.shape) for c in consts],
        out_specs=row_spec(d_model),
        out_shape=jax.ShapeDtypeStruct((n_tok, d_model), F32),
        scratch_shapes=[pltpu.VMEM((tm, d_ff), BF16)],
        compiler_params=pltpu.CompilerParams(dimension_semantics=("arbitrary",),
                                             vmem_limit_bytes=VMEM_LIMIT),
        name="post",
    )(h1, ain, o, p, *consts)


ATT_BQ = 512
ATT_BK = 256
ATT_VT_COLS = 512
ATT_ONES_ROWS = 16
ATT_HEADS = 2


def _prompt_attn_kernel(q_ref, k_ref, v_ref, o_ref, vt_ref, st_ref):
    bq = ATT_BQ
    seq = k_ref.shape[0]

    heads = range(ATT_HEADS)
    for r in range(seq // ATT_VT_COLS):
        rs = slice(r * ATT_VT_COLS, (r + 1) * ATT_VT_COLS)
        vt = v_ref[rs, :].astype(F32).T.astype(BF16)
        for hh in heads:
            vt_ref[hh, :V_DIM, rs] = vt[hh * V_DIM:(hh + 1) * V_DIM]
            vt_ref[hh, V_DIM:, rs] = jnp.ones((ATT_ONES_ROWS, ATT_VT_COLS), BF16)

    def query_block(i, odd):
        rows = pl.ds(pl.multiple_of(i * bq, bq), bq)
        qts = [q_ref[rows, hh * HEAD_PAD:(hh + 1) * HEAD_PAD].astype(F32).T.astype(BF16) for hh in heads]

        def scores(j, slot, lo=0):
            start = pl.multiple_of(j * ATT_BK, ATT_BK)
            for hh in heads:
                st_ref[slot, hh, :, lo:] = _dot(
                    k_ref[pl.ds(start, ATT_BK), hh * HEAD_PAD:(hh + 1) * HEAD_PAD], qts[hh][:, lo:])

        def update(j, slot, carry, masked, lo=0):
            start = pl.multiple_of(j * ATT_BK, ATT_BK)
            out = []
            for hh in heads:
                m, acc = carry[hh]
                st = st_ref[slot, hh, :, lo:]
                if masked:
                    key = start + lax.broadcasted_iota(jnp.int32, st.shape, 0)
                    qry = i * bq + lo + lax.broadcasted_iota(jnp.int32, st.shape, 1)
                    st = jnp.where(key <= qry, st, -1e30)
                m_new = jnp.maximum(m[:, lo:], jnp.max(st, axis=0, keepdims=True))
                p = jnp.exp2(st - m_new)
                acc_new = (jnp.exp2(m[:, lo:] - m_new) * acc[:, lo:]
                           + _dot(vt_ref[hh, :, pl.ds(start, ATT_BK)], p.astype(BF16)))
                if lo:
                    m_new = jnp.concatenate([m[:, :lo], m_new], axis=1)
                    acc_new = jnp.concatenate([acc[:, :lo], acc_new], axis=1)
                out.append((m_new, acc_new))
            return tuple(out)

        one = (jnp.full((1, bq), -1e30, F32), jnp.zeros((V_DIM + ATT_ONES_ROWS, bq), F32))

        def pair(j, carry):
            scores(j + 1, 1)
            carry = update(j, 0, carry, False)
            scores(j + 2, 0)
            return update(j + 1, 1, carry, False)

        def quad(t, carry):
            return pair(4 * t + 2, pair(4 * t, carry))

        scores(0, 0)
        carry = lax.fori_loop(0, i // 2, quad, (one,) * ATT_HEADS)
        if odd:
            carry = pair(2 * i - 2, carry)
        scores(2 * i + 1, 1, lo=ATT_BK)
        carry = update(2 * i, 0, carry, True)
        carry = update(2 * i + 1, 1, carry, True, lo=ATT_BK)
        ot = jnp.concatenate([acc[:V_DIM] / acc[V_DIM:V_DIM + 1] for (_, acc) in carry], axis=0)
        o_ref[rows, :] = ot.T.astype(o_ref.dtype)

    def block_pair(u, _):
        query_block(2 * u, False)
        query_block(2 * u + 1, True)
        return 0

    lax.fori_loop(0, seq // (2 * bq), block_pair, 0)


def _prompt_attn(q, k, v, batch, seq):
    assert ATT_BQ == 2 * ATT_BK and seq % (2 * ATT_BQ) == 0 and seq % ATT_VT_COLS == 0
    n_tok = q.shape[0]
    return pl.pallas_call(
        _prompt_attn_kernel,
        grid=(batch, N_HEADS // ATT_HEADS),
        in_specs=[pl.BlockSpec((seq, ATT_HEADS * HEAD_PAD), lambda b, hp: (b, hp)),
                  pl.BlockSpec((seq, ATT_HEADS * HEAD_PAD), lambda b, hp: (b, hp)),
                  pl.BlockSpec((seq, ATT_HEADS * V_DIM), lambda b, hp: (b, hp))],
        out_specs=pl.BlockSpec((seq, ATT_HEADS * V_DIM), lambda b, hp: (b, hp)),
        out_shape=jax.ShapeDtypeStruct((n_tok, N_HEADS * V_DIM), BF16),
        scratch_shapes=[pltpu.VMEM((ATT_HEADS, V_DIM + ATT_ONES_ROWS, seq), BF16),
                        pltpu.VMEM((2, ATT_HEADS, ATT_BK, ATT_BQ), F32)],
        compiler_params=pltpu.CompilerParams(dimension_semantics=("arbitrary",) * 2,
                                             vmem_limit_bytes=VMEM_LIMIT),
        name="prompt_attn",
    )(q, k, v)


SAMPLE_SUB = 2048
SAMPLE_SUB_MIN = 2048


def _sample_blocks(past):
    blocks, left, size = [], past, SAMPLE_SUB
    while left > 0:
        while size > SAMPLE_SUB_MIN and left < 2 * size:
            size //= 2
        size = min(size, left)
        blocks.append(size)
        left -= size
    return blocks
SAMPLE_SEQS = 1
SAMPLE_SLOTS = 3


def _qabs_kernel(q_ref, gk_ref, wuk_ref, qlat_ref, qr_ref):
    for h in range(N_HEADS):
        qh = q_ref[:, h * HEAD_PAD:(h + 1) * HEAD_PAD] * gk_ref[...]
        qlat_ref[h] = _dot(qh[:, :QK_NOPE].astype(BF16), wuk_ref[h])
        qr_ref[h] = qh[:, QK_NOPE:QK_HEAD]


def _qabs_call(q, gk, wuk):
    n_tok = q.shape[0]
    kv_lora = wuk.shape[2]
    return pl.pallas_call(
        _qabs_kernel,
        out_shape=[jax.ShapeDtypeStruct((N_HEADS, n_tok, kv_lora), F32),
                   jax.ShapeDtypeStruct((N_HEADS, n_tok, QK_ROPE), F32)],
        compiler_params=pltpu.CompilerParams(vmem_limit_bytes=VMEM_LIMIT),
        name="q_absorb",
    )(q, gk, wuk)


def _sample_attn_kernel(layer, pt_ref, qlat_ref, qr_ref, cnew_ref, krnew_ref, wukt_ref, ckv_hbm, krt_hbm,
                        olat_ref, cbuf, kbuf, csem, ksem, lhs_ref, cb_ref):
    s = pl.program_id(0)
    n_steps = pl.num_programs(0)
    n_par = cnew_ref.shape[0]
    past = cbuf.shape[1] // n_par
    n_pages = past // PAGE
    t_new = cnew_ref.shape[1]
    kv_lora = cnew_ref.shape[2]
    n_rows = N_HEADS * t_new
    n_kn = N_HEADS * QK_NOPE
    n_slots = cbuf.shape[0]
    ahead = n_slots - 1
    slot = s % n_slots

    def copies(step, slot_):
        out = []
        for a in range(n_par):
            for p in range(n_pages):
                page = pt_ref[(step * n_par + a) * n_pages + p]
                keys = pl.ds(a * past + p * PAGE, PAGE)
                out.append(pltpu.make_async_copy(ckv_hbm.at[layer, page], cbuf.at[slot_, keys, :],
                                                 csem.at[slot_]))
                out.append(pltpu.make_async_copy(krt_hbm.at[layer, page], kbuf.at[slot_, :, keys],
                                                 ksem.at[slot_]))
        return out

    def start_all(cps):
        for n, cp in enumerate(cps):
            cp.start(priority=(n // 2) % 2)

    @pl.when(s == 0)
    def _():
        for a in range(n_par):
            lhs_ref[a, :n_kn, :] = wukt_ref[...]
        for d in range(ahead):
            start_all(copies(jnp.minimum(d, n_steps - 1), d))

    for cp in copies(s, slot):
        cp.wait()

    qrs = []
    for a in range(n_par):
        tok = slice(a * t_new, (a + 1) * t_new)
        lhs_ref[a, n_kn:, :] = qlat_ref[:, tok, :].reshape(n_rows, kv_lora).astype(BF16)
        qrs.append(qr_ref[:, tok, :].reshape(n_rows, QK_ROPE).astype(BF16))

    def scaled_scores(big, rope_scores, r2):
        n = big.shape[1]
        kn = big[:n_kn]
        ss = jnp.sum((kn * kn).reshape(N_HEADS, QK_NOPE, n), axis=1)
        rs = lax.rsqrt((ss + r2) * (1.0 / QK_HEAD) + EPS)
        sc = (big[n_kn:] + rope_scores).reshape(N_HEADS, t_new, n) * rs[:, None, :]
        return sc.reshape(n_rows, n)

    blocks = _sample_blocks(past)
    starts = [sum(blocks[:k]) for k in range(len(blocks))]

    def past_scores(a, k):
        keys = slice(a * past + starts[k], a * past + starts[k] + blocks[k])
        cb = cbuf[slot, keys, :].astype(BF16)
        cb_ref[a, k % 2, :blocks[k], :] = cb
        krt = kbuf[slot, :, keys]
        r2 = jnp.sum(krt * krt, axis=0, keepdims=True)
        return scaled_scores(_dot_nt(lhs_ref[a], cb), _dot(qrs[a], krt.astype(BF16)), r2)

    def new_scores(a):
        pad = PAGE - t_new
        cb_new = jnp.concatenate([cnew_ref[a], jnp.zeros((pad, kv_lora), F32)], axis=0).astype(BF16)
        kr_new = jnp.concatenate([krnew_ref[a], jnp.zeros((pad, QK_ROPE), F32)], axis=0)
        kr2 = kr_new * kr_new
        kr2_hi = kr2.astype(BF16)
        kr2_lo = (kr2 - kr2_hi.astype(F32)).astype(BF16)
        ones = jnp.ones((8, QK_ROPE), BF16)
        r2_new = (_dot_nt(ones, kr2_hi) + _dot_nt(ones, kr2_lo))[:1]
        sc = scaled_scores(_dot_nt(lhs_ref[a], cb_new), _dot_nt(qrs[a], kr_new.astype(BF16)), r2_new)
        row = lax.broadcasted_iota(jnp.int32, (n_rows, PAGE), 0)
        col = lax.broadcasted_iota(jnp.int32, (n_rows, PAGE), 1)
        return jnp.where(col <= row % t_new, sc, -1e30), cb_new

    n_sub = len(blocks)
    new =[new_scores(a) for a in range(n_par)]
    sc_next = [past_scores(a, 0) for a in range(n_par)]

    nxt = jnp.minimum(s + ahead, n_steps - 1)
    nxt_slot = (s + ahead) % n_slots
    start_all(copies(nxt, nxt_slot))

    state = []
    for a in range(n_par):
        sc_new, cb_new = new[a]
        m = jnp.max(sc_new, axis=-1, keepdims=True)
        p = jnp.exp2(sc_new - m)
        state.append((m, jnp.sum(p, axis=-1, keepdims=True), _dot(p.astype(BF16), cb_new)))
    for k in range(n_sub):
        sc_cur = sc_next
        if k + 1 < n_sub:
            sc_next = [past_scores(a, k + 1) for a in range(n_par)]
        for a in range(n_par):
            m, l, acc = state[a]
            m_new = jnp.maximum(m, jnp.max(sc_cur[a], axis=-1, keepdims=True))
            alpha = jnp.exp2(m - m_new)
            p = jnp.exp2(sc_cur[a] - m_new)
            l = alpha * l + jnp.sum(p, axis=-1, keepdims=True)
            acc = alpha * acc + _dot(p.astype(BF16), cb_ref[a, k % 2, :blocks[k], :])
            state[a] = (m_new, l, acc)
    for a in range(n_par):
        _, l, acc = state[a]
        olat_ref[a] = acc / l

    @pl.when(s == n_steps - 1)
    def _():
        for d in range(1, n_slots):
            for cp in copies(nxt, (s + d) % n_slots):
                cp.wait()


def _sample_attn(layer, page_table, qlat, qr, c_new, kr_new, wukt, cache_ckv, cache_krt):
    n_seq, n_pages = page_table.shape
    t_new = c_new.shape[1]
    kv_lora = c_new.shape[2]
    n_rows = N_HEADS * t_new
    past = n_pages * PAGE
    n_par = SAMPLE_SEQS
    assert n_seq % n_par == 0
    grid_spec = pltpu.PrefetchScalarGridSpec(
        num_scalar_prefetch=1,
        grid=(n_seq // n_par,),
        in_specs=[pl.BlockSpec((N_HEADS, n_par * t_new, kv_lora), lambda s, pt: (0, s, 0)),
                  pl.BlockSpec((N_HEADS, n_par * t_new, QK_ROPE), lambda s, pt: (0, s, 0)),
                  pl.BlockSpec((n_par, t_new, kv_lora), lambda s, pt: (s, 0, 0)),
                  pl.BlockSpec((n_par, t_new, QK_ROPE), lambda s, pt: (s, 0, 0)),
                  pl.BlockSpec(wukt.shape, lambda s, pt: (0, 0)),
                  pl.BlockSpec(memory_space=pl.ANY),
                  pl.BlockSpec(memory_space=pl.ANY)],
        out_specs=pl.BlockSpec((n_par, n_rows, kv_lora), lambda s, pt: (s, 0, 0)),
        scratch_shapes=[pltpu.VMEM((SAMPLE_SLOTS, n_par * past, kv_lora), F32),
                        pltpu.VMEM((SAMPLE_SLOTS, QK_ROPE, n_par * past), F32),
                        pltpu.SemaphoreType.DMA((SAMPLE_SLOTS,)),
                        pltpu.SemaphoreType.DMA((SAMPLE_SLOTS,)),
                        pltpu.VMEM((n_par, N_HEADS * QK_NOPE + n_rows, kv_lora), BF16),
                        pltpu.VMEM((n_par, 2, max(_sample_blocks(past)), kv_lora), BF16)],
    )
    return pl.pallas_call(
        functools.partial(_sample_attn_kernel, layer),
        grid_spec=grid_spec,
        out_shape=jax.ShapeDtypeStruct((n_seq, n_rows, kv_lora), F32),
        compiler_params=pltpu.CompilerParams(dimension_semantics=("arbitrary",),
                                             vmem_limit_bytes=VMEM_LIMIT),
        name="sample_attn",
    )(page_table.reshape(-1), qlat, qr, c_new, kr_new, wukt, cache_ckv, cache_krt)


def _oproj_kernel(olat_ref, wuv_ref, o_ref):
    n_seq, n_rows, kv_lora = olat_ref.shape
    t_new = n_rows // N_HEADS
    acc = None
    for h in range(N_HEADS):
        x = olat_ref[:, h * t_new:(h + 1) * t_new, :].reshape(n_seq * t_new, kv_lora).astype(BF16)
        part = _dot(x, wuv_ref[h])
        acc = part if acc is None else acc + part
    o_ref[...] = acc.astype(o_ref.dtype)


def _oproj_call(olat, wuv):
    n_seq, n_rows, _ = olat.shape
    return pl.pallas_call(
        _oproj_kernel,
        out_shape=jax.ShapeDtypeStruct((n_seq * n_rows // N_HEADS, N_HEADS * V_DIM), BF16),
        compiler_params=pltpu.CompilerParams(vmem_limit_bytes=VMEM_LIMIT),
        name="o_proj",
    )(olat, wuv)


def _pad_heads(w, width):
    k = w.shape[0]
    w = w.reshape(k, N_HEADS, width)
    return jnp.pad(w, ((0, 0), (0, 0), (0, HEAD_PAD - width))).reshape(k, N_HEADS * HEAD_PAD)


def _rot_cols(w):
    half = w.shape[-1] // 2
    return jnp.concatenate([-w[..., half:], w[..., :half]], axis=-1)


def _rope_block(w):
    return jnp.pad(w, ((0, 0), (QK_NOPE, HEAD_PAD - QK_HEAD)))


def _head_gain(g):
    return jnp.pad(g, (0, HEAD_PAD - QK_HEAD)).reshape(1, HEAD_PAD)


def _rope_tables(n_pos, offset):
    pos = jnp.arange(n_pos, dtype=F32) + offset
    inv_freq = ROPE_THETA ** (-jnp.arange(0, QK_ROPE, 2, dtype=F32) / QK_ROPE)
    ang = pos[:, None] * inv_freq[None, :]
    cos, sin = jnp.cos(ang), jnp.sin(ang)
    ones = jnp.ones((n_pos, QK_NOPE), F32)
    tail = jnp.ones((n_pos, HEAD_PAD - QK_HEAD), F32)
    cos_tab = jnp.concatenate([ones, cos, cos, tail], axis=1)
    sin_tab = jnp.concatenate([0 * ones, sin, sin, 0 * tail], axis=1)
    return cos_tab, sin_tab


def _layer_weights(i, ffn1_norm, ffn1_w_in, ffn1_w_out, mix_norm, w_in, qa_norm, kva_norm, w_uq, w_ukv,
                   q_norm, k_norm, sgu_ln_g, sgu_ln_b, sgu_w_s, sgu_b_s, w_a_out, w_b_out, w_o,
                   ffn2_norm, ffn2_w_in, ffn2_w_out, ple_norm, ple_w_gate, ple_w_proj):
    a_width = sgu_ln_g.shape[1]
    q_lora = qa_norm.shape[1]
    kv_lora = kva_norm.shape[1]
    s1 = 2 * a_width
    s2 = s1 + q_lora
    s3 = s2 + kv_lora
    s4 = s3 + QK_ROPE
    wi = w_in[i]
    w_kr = wi[:, s3:s4]
    wpre = jnp.concatenate([wi[:, :s3], _rope_block(w_kr), _rope_block(_rot_cols(w_kr))], axis=1)

    wq = w_uq[i].reshape(q_lora, N_HEADS, QK_HEAD)
    wq_rot = jnp.concatenate([jnp.zeros_like(wq[..., :QK_NOPE]), _rot_cols(wq[..., QK_NOPE:])], axis=-1)
    wuq = jnp.stack([_pad_heads(wq.reshape(q_lora, -1), QK_HEAD).reshape(q_lora, N_HEADS, HEAD_PAD),
                     _pad_heads(wq_rot.reshape(q_lora, -1), QK_HEAD).reshape(q_lora, N_HEADS, HEAD_PAD)],
                    axis=2).reshape(q_lora, 2 * N_HEADS * HEAD_PAD)

    wkv = w_ukv[i].reshape(kv_lora, N_HEADS, QK_NOPE + V_DIM)
    w_uk = wkv[..., :QK_NOPE]
    w_uv = wkv[..., QK_NOPE:]
    wukv = jnp.concatenate([_pad_heads(w_uk.reshape(kv_lora, -1), QK_NOPE),
                            w_uv.reshape(kv_lora, -1)], axis=1)
    wuk_t = jnp.transpose(w_uk, (1, 2, 0))
    eye = jnp.eye(N_HEADS, dtype=F32)
    wuv_blk = jnp.einsum('khd,hg->hkgd', w_uv, eye).reshape(N_HEADS, kv_lora, N_HEADS * V_DIM)

    row = lambda g: g[i].reshape(1, -1)
    return {
        'f1g': row(ffn1_norm), 'f1in': ffn1_w_in[i].astype(BF16), 'f1out': ffn1_w_out[i].astype(BF16),
        'mixg': row(mix_norm), 'wpre': wpre.astype(BF16), 'wgates': wi[:, s4:].astype(BF16),
        'qag': row(qa_norm), 'kvag': row(kva_norm), 'wuq': wuq.astype(BF16), 'wukv': wukv.astype(BF16),
        'qg': _head_gain(q_norm[i]), 'kg': _head_gain(k_norm[i]),
        'wuk_t': wuk_t.astype(BF16), 'wuv_blk': wuv_blk.astype(BF16),
        'lng': row(sgu_ln_g), 'lnb': row(sgu_ln_b), 'w_s': sgu_w_s[i], 'b_s': sgu_b_s[i],
        'wa': w_a_out[i].astype(BF16), 'wb': w_b_out[i].astype(BF16), 'wo': w_o[i].astype(BF16),
        'f2g': row(ffn2_norm), 'f2in': ffn2_w_in[i].astype(BF16), 'f2out': ffn2_w_out[i].astype(BF16),
        'pleg': row(ple_norm), 'plewg': ple_w_gate[i].astype(BF16), 'plewp': ple_w_proj[i].astype(BF16),
    }


def _mix_tables(w_s, b_s, period, group_dim):
    r = jnp.arange(CHUNK)
    same = (r[:, None] // period) == (r[None, :] // period)
    causal = (r[None, :] % period) <= (r[:, None] % period)
    reps = CHUNK // period
    mixw = jnp.where((same & causal)[None], jnp.tile(w_s[:, :period, :period], (1, reps, reps)), 0.0)
    bias = jnp.tile(b_s[:, :period], (1, reps))
    mixb = jnp.repeat(bias.T, group_dim, axis=1)
    return mixw.astype(BF16), mixb


def kernel(x_prompt, x_sample, cache_ckv, cache_krope, page_table, p_prompt, p_sample, ffn1_norm, ffn1_w_in, ffn1_w_out, mix_norm, w_in, qa_norm, kva_norm, w_uq, w_ukv, q_norm, k_norm, sgu_ln_g, sgu_ln_b, sgu_w_s, sgu_b_s, w_a_out, w_b_out, w_o, ffn2_norm, ffn2_w_in, ffn2_w_out, ple_norm, ple_w_gate, ple_w_proj):
    depth = w_in.shape[0]
    batch, seq, d_model = x_prompt.shape
    dec_batch, dec_seq, _ = x_sample.shape
    past_len = page_table.shape[1] * cache_ckv.shape[2]
    assert seq % TOKEN_TILE == 0 and TOKEN_TILE % dec_seq == 0 and CHUNK % dec_seq == 0
    assert cache_ckv.shape[2] == PAGE

    yp = x_prompt.reshape(batch * seq, d_model)
    ys = x_sample.reshape(dec_batch * dec_seq, d_model)
    cos_p, sin_p = _rope_tables(seq, 0)
    cos_s, sin_s = _rope_tables(dec_seq, past_len)
    reps = TOKEN_TILE // dec_seq
    cos_s, sin_s = jnp.tile(cos_s, (reps, 1)), jnp.tile(sin_s, (reps, 1))

    outs = [[] for _ in range(5)]
    for i in range(depth):
        w = _layer_weights(i, ffn1_norm, ffn1_w_in, ffn1_w_out, mix_norm, w_in, qa_norm, kva_norm, w_uq,
                           w_ukv, q_norm, k_norm, sgu_ln_g, sgu_ln_b, sgu_w_s, sgu_b_s, w_a_out, w_b_out,
                           w_o, ffn2_norm, ffn2_w_in, ffn2_w_out, ple_norm, ple_w_gate, ple_w_proj)

        group_dim = w['lng'].shape[1] // A_GROUPS
        w['mixw'], w['mixb'] = _mix_tables(w['w_s'], w['b_s'], CHUNK, group_dim)
        h1, ain, q, ckv_p, kr_p, k, v = _pre_call(yp, cos_p, sin_p, w, True, False, BF16)
        o = _prompt_attn(q, k, v, batch, seq)
        yp = _post_call(h1, ain, o, p_prompt[i].reshape(batch * seq, -1), w)

        w['mixw'], w['mixb'] = _mix_tables(w['w_s'], w['b_s'], dec_seq, group_dim)
        h1, ain, q, ckv_s, kr_s, vn_s = _pre_call(ys, cos_s, sin_s, w, False, True, F32)
        qlat, qr = _qabs_call(q, w['kg'], w['wuk_t'])
        kv_lora = ckv_s.shape[1]
        olat = _sample_attn(i, page_table, qlat, qr,
                            ckv_s.reshape(dec_batch, dec_seq, kv_lora),
                            kr_s.reshape(dec_batch, dec_seq, QK_ROPE),
                            w['wuk_t'].reshape(N_HEADS * QK_NOPE, kv_lora),
                            cache_ckv, jnp.swapaxes(cache_krope, 2, 3))
        o = _oproj_call(olat, w['wuv_blk'])
        ys = _post_call(h1, ain, o, p_sample[i].reshape(dec_batch * dec_seq, -1), w)

        outs[0].append(ckv_p.reshape(batch, seq, -1))
        outs[1].append(kr_p.reshape(batch, seq, -1))
        outs[2].append(ckv_s.reshape(dec_batch, dec_seq, -1))
        outs[3].append(kr_s.reshape(dec_batch, dec_seq, -1))
        outs[4].append(vn_s.reshape(dec_batch, dec_seq, -1))

    return (yp.reshape(batch, seq, d_model), ys.reshape(dec_batch, dec_seq, d_model),
            jnp.stack(outs[0]), jnp.stack(outs[1]), jnp.stack(outs[2]), jnp.stack(outs[3]),
            jnp.stack(outs[4]))
```

```python
import functools

import jax
import jax.numpy as jnp
from jax import lax
from jax.experimental import pallas as pl
from jax.experimental.pallas import tpu as pltpu

F32 = jnp.float32
BF16 = jnp.bfloat16

EPS = 1e-6
N_HEADS = 8
QK_NOPE = 64
QK_ROPE = 32
QK_HEAD = QK_NOPE + QK_ROPE
V_DIM = 64
HEAD_PAD = 128
A_GROUPS = 4
CHUNK = 128
ROPE_THETA = 10000.0
PAGE = 128
LOG2_E = 1.4426950408889634

FF_CHUNK = 256
TOKEN_TILE = 512
VMEM_LIMIT = 56 * 1024 * 1024

_NT = (((1,), (1,)), ((), ()))


def _dot(a, b):
    return jnp.dot(a, b, preferred_element_type=F32)


def _dot_nt(a, b):
    return lax.dot_general(a, b, _NT, preferred_element_type=F32)


def _rms(x, g):
    return x * lax.rsqrt(jnp.mean(x * x, axis=-1, keepdims=True) + EPS) * g


def _ffn_half(x, g_ref, win_ref, wout_ref, act_ref):
    d_ff = wout_ref.shape[0]
    xn = _rms(x, g_ref[...]).astype(BF16)
    for c in range(d_ff // FF_CHUNK):
        lo = c * FF_CHUNK
        a = _dot(xn, win_ref[:, lo:lo + FF_CHUNK])
        b = _dot(xn, win_ref[:, d_ff + lo:d_ff + lo + FF_CHUNK])
        act_ref[:, lo:lo + FF_CHUNK] = (a * jax.nn.sigmoid(a) * b).astype(BF16)
    return x + 0.5 * _dot(act_ref[...], wout_ref[...])


def _head_norm(xh, gain):
    ss = jnp.sum(xh * xh, axis=-1, keepdims=True)
    return xh * lax.rsqrt(ss * (1.0 / QK_HEAD) + EPS) * gain


def _pre_kernel(expand_kv, write_vn, *refs):
    (x_ref, cos_ref, sin_ref, f1g_ref, f1in_ref, f1out_ref, mixg_ref, wpre_ref, qag_ref, kvag_ref,
     wuq_ref, qg_ref, lng_ref, lnb_ref, mixw_ref, mixb_ref) = refs[:16]
    pos = 16
    if expand_kv:
        wukv_ref, kg_ref = refs[pos:pos + 2]
        pos += 2
    h1_ref, ain_ref, q_ref, ckv_ref, kr_ref = refs[pos:pos + 5]
    pos += 5
    if expand_kv:
        k_ref, v_ref = refs[pos:pos + 2]
        pos += 2
    if write_vn:
        vn_ref = refs[pos]
        pos += 1
    act_ref = refs[pos]

    a_width = lng_ref.shape[1]
    q_lora = qag_ref.shape[1]
    kv_lora = kvag_ref.shape[1]
    rows = x_ref.shape[0]

    h1 = _ffn_half(x_ref[...], f1g_ref, f1in_ref, f1out_ref, act_ref)
    h1_ref[...] = h1
    n = _rms(h1, mixg_ref[...]).astype(BF16)

    c0 = 2 * a_width
    c1 = c0 + q_lora
    c2 = c1 + kv_lora
    nq = N_HEADS * HEAD_PAD
    p_cq = _dot(n, wpre_ref[:, c0:c1])
    p_ckv = _dot(n, wpre_ref[:, c1:c2])
    p_kr = _dot(n, wpre_ref[:, c2:c2 + 2 * HEAD_PAD])
    p_uv = _dot(n, wpre_ref[:, :c0])

    uv = jax.nn.gelu(p_uv)
    u, v = uv[:, :a_width], uv[:, a_width:]
    vc = v - jnp.mean(v, axis=-1, keepdims=True)
    vn = vc * lax.rsqrt(jnp.mean(vc * vc, axis=-1, keepdims=True) + EPS) * lng_ref[...] + lnb_ref[...]
    if write_vn:
        vn_ref[...] = vn
    vnb = vn.astype(BF16)

    cos = cos_ref[...]
    sin = sin_ref[...]
    cq = _rms(p_cq, qag_ref[...]).astype(BF16)
    scale = QK_HEAD ** -0.5 * LOG2_E
    for h in range(N_HEADS):
        t = _dot(cq, wuq_ref[:, 2 * h * HEAD_PAD:2 * (h + 1) * HEAD_PAD])
        qh = t[:, :HEAD_PAD] * cos + t[:, HEAD_PAD:] * sin
        q_ref[:, h * HEAD_PAD:(h + 1) * HEAD_PAD] = (_head_norm(qh, qg_ref[...]) * scale).astype(q_ref.dtype)
    c_kv = _rms(p_ckv, kvag_ref[...])
    ckv_ref[...] = c_kv
    kr_blk = p_kr[:, :HEAD_PAD] * cos + p_kr[:, HEAD_PAD:] * sin
    kr_ref[...] = kr_blk[:, QK_NOPE:QK_HEAD]
    if expand_kv:
        ckvb = c_kv.astype(BF16)
        for hp in range(N_HEADS // 2):
            t = _dot(ckvb, wukv_ref[:, 2 * hp * HEAD_PAD:2 * (hp + 1) * HEAD_PAD])
            for hh in range(2):
                hs = slice((2 * hp + hh) * HEAD_PAD, (2 * hp + hh + 1) * HEAD_PAD)
                kh = t[:, hh * HEAD_PAD:(hh + 1) * HEAD_PAD] + kr_blk
                k_ref[:, hs] = _head_norm(kh, kg_ref[...]).astype(k_ref.dtype)
        v_ref[...] = _dot(ckvb, wukv_ref[:, nq:]).astype(v_ref.dtype)

    gd = a_width // A_GROUPS
    for r in range(rows // CHUNK):
        rs = slice(r * CHUNK, (r + 1) * CHUNK)
        for g in range(A_GROUPS):
            gs = slice(g * gd, (g + 1) * gd)
            z = _dot(mixw_ref[g], vnb[rs, gs]) + mixb_ref[:, gs]
            ain_ref[rs, gs] = (u[rs, gs] * z).astype(ain_ref.dtype)


def _const_spec(shape):
    nd = len(shape)
    return pl.BlockSpec(shape, lambda i, _nd=nd: (0,) * _nd, pipeline_mode=pl.Buffered(1))


def _pre_call(x, cos_tab, sin_tab, w, expand_kv, write_vn, q_dtype):
    n_tok, d_model = x.shape
    tm = TOKEN_TILE
    n_tab = cos_tab.shape[0] // tm
    a_width = w['lng'].shape[1]
    kv_lora = w['kvag'].shape[1]
    d_ff = w['f1out'].shape[0]
    nq = N_HEADS * HEAD_PAD

    def row_spec(cols):
        return pl.BlockSpec((tm, cols), lambda i: (i, 0))

    tab_spec = pl.BlockSpec((tm, HEAD_PAD), lambda i: (i % n_tab, 0))
    consts = [w['f1g'], w['f1in'], w['f1out'], w['mixg'], w['wpre'], w['qag'], w['kvag'], w['wuq'],
              w['qg'], w['lng'], w['lnb'], w['mixw'], w['mixb']]
    if expand_kv:
        consts += [w['wukv'], w['kg']]
    in_specs = [row_spec(d_model), tab_spec, tab_spec] + [_const_spec(c.shape) for c in consts]
    out_shape = [jax.ShapeDtypeStruct((n_tok, d_model), F32),
                 jax.ShapeDtypeStruct((n_tok, a_width), BF16),
                 jax.ShapeDtypeStruct((n_tok, nq), q_dtype),
                 jax.ShapeDtypeStruct((n_tok, kv_lora), F32),
                 jax.ShapeDtypeStruct((n_tok, QK_ROPE), F32)]
    out_specs = [row_spec(d_model), row_spec(a_width), row_spec(nq), row_spec(kv_lora), row_spec(QK_ROPE)]
    if expand_kv:
        out_shape += [jax.ShapeDtypeStruct((n_tok, nq), BF16),
                      jax.ShapeDtypeStruct((n_tok, N_HEADS * V_DIM), BF16)]
        out_specs += [row_spec(nq), row_spec(N_HEADS * V_DIM)]
    if write_vn:
        out_shape.append(jax.ShapeDtypeStruct((n_tok, a_width), F32))
        out_specs.append(row_spec(a_width))
    return pl.pallas_call(
        functools.partial(_pre_kernel, expand_kv, write_vn),
        grid=(n_tok // tm,),
        in_specs=in_specs,
        out_specs=out_specs,
        out_shape=out_shape,
        scratch_shapes=[pltpu.VMEM((tm, d_ff), BF16)],
        compiler_params=pltpu.CompilerParams(dimension_semantics=("arbitrary",),
                                             vmem_limit_bytes=VMEM_LIMIT),
        name="pre_expand" if expand_kv else "pre_latent",
    )(x, cos_tab, sin_tab, *consts)


def _post_kernel(h1_ref, ain_ref, o_ref, p_ref, mixg_ref, wg_ref, wa_ref, wb_ref, wo_ref,
                 f2g_ref, f2in_ref, f2out_ref, pleg_ref, plewg_ref, plewp_ref, y_ref, act_ref):
    d_model = h1_ref.shape[1]
    h1 = h1_ref[...]
    n = _rms(h1, mixg_ref[...]).astype(BF16)
    ga = jax.nn.sigmoid(_dot(n, wg_ref[:, :d_model]))
    gb = jax.nn.sigmoid(_dot(n, wg_ref[:, d_model:]))
    m = ga * _dot(ain_ref[...], wa_ref[...]) + gb * _dot(o_ref[...], wb_ref[...])
    h2 = h1 + _dot(m.astype(BF16), wo_ref[...])
    h3 = _ffn_half(h2, f2g_ref, f2in_ref, f2out_ref, act_ref)
    hn = _rms(h3, pleg_ref[...]).astype(BF16)
    gate = jax.nn.sigmoid(_dot(hn, plewg_ref[...]))
    y_ref[...] = h3 + gate * _dot(p_ref[...].astype(BF16), plewp_ref[...])


def _post_call(h1, ain, o, p, w):
    n_tok, d_model = h1.shape
    tm = TOKEN_TILE
    d_ff = w['f2out'].shape[0]

    def row_spec(cols):
        return pl.BlockSpec((tm, cols), lambda i: (i, 0))

    consts = [w['mixg'], w['wgates'], w['wa'], w['wb'], w['wo'], w['f2g'], w['f2in'], w['f2out'],
              w['pleg'], w['plewg'], w['plewp']]
    return pl.pallas_call(
        _post_kernel,
        grid=(n_tok // tm,),
        in_specs=[row_spec(d_model), row_spec(ain.shape[1]), row_spec(o.shape[1]), row_spec(p.shape[1])]
        + [_const_spec(c.shape) for c in consts],
        out_specs=row_spec(d_model),
        out_shape=jax.ShapeDtypeStruct((n_tok, d_model), F32),
        scratch_shapes=[pltpu.VMEM((tm, d_ff), BF16)],
        compiler_params=pltpu.CompilerParams(dimension_semantics=("arbitrary",),
                                             vmem_limit_bytes=VMEM_LIMIT),
        name="post",
    )(h1, ain, o, p, *consts)


ATT_BQ = 512
ATT_BK = 256
ATT_VT_COLS = 512
ATT_ONES_ROWS = 16
ATT_HEADS = 2


def _prompt_attn_kernel(q_ref, k_ref, v_ref, o_ref, vt_ref, st_ref):
    bq = ATT_BQ
    seq = k_ref.shape[0]

    heads = range(ATT_HEADS)
    for r in range(seq // ATT_VT_COLS):
        rs = slice(r * ATT_VT_COLS, (r + 1) * ATT_VT_COLS)
        vt = v_ref[rs, :].astype(F32).T.astype(BF16)
        for hh in heads:
            vt_ref[hh, :V_DIM, rs] = vt[hh * V_DIM:(hh + 1) * V_DIM]
            vt_ref[hh, V_DIM:, rs] = jnp.ones((ATT_ONES_ROWS, ATT_VT_COLS), BF16)

    def query_block(i):
        rows = slice(i * bq, (i + 1) * bq)
        qts = [q_ref[rows, hh * HEAD_PAD:(hh + 1) * HEAD_PAD].astype(F32).T.astype(BF16) for hh in heads]

        def scores(j, slot, lo=0):
            start = j * ATT_BK
            for hh in heads:
                st_ref[slot, hh, :, lo:] = _dot(
                    k_ref[pl.ds(start, ATT_BK), hh * HEAD_PAD:(hh + 1) * HEAD_PAD], qts[hh][:, lo:])

        def update(j, slot, carry, masked, lo=0):
            start = j * ATT_BK
            out = []
            for hh in heads:
                m, acc = carry[hh]
                st = st_ref[slot, hh, :, lo:]
                if masked:
                    key = start + lax.broadcasted_iota(jnp.int32, st.shape, 0)
                    qry = i * bq + lo + lax.broadcasted_iota(jnp.int32, st.shape, 1)
                    st = jnp.where(key <= qry, st, -1e30)
                m_new = jnp.maximum(m[:, lo:], jnp.max(st, axis=0, keepdims=True))
                p = jnp.exp2(st - m_new)
                acc_new = (jnp.exp2(m[:, lo:] - m_new) * acc[:, lo:]
                           + _dot(vt_ref[hh, :, pl.ds(start, ATT_BK)], p.astype(BF16)))
                if lo:
                    m_new = jnp.concatenate([m[:, :lo], m_new], axis=1)
                    acc_new = jnp.concatenate([acc[:, :lo], acc_new], axis=1)
                out.append((m_new, acc_new))
            return tuple(out)

        one = (jnp.full((1, bq), -1e30, F32), jnp.zeros((V_DIM + ATT_ONES_ROWS, bq), F32))

        carry = (one,) * ATT_HEADS
        scores(0, 0)
        for j in range(2 * i):
            scores(j + 1, (j + 1) % 2)
            carry = update(j, j % 2, carry, False)
        scores(2 * i + 1, 1, lo=ATT_BK)
        carry = update(2 * i, 0, carry, True)
        carry = update(2 * i + 1, 1, carry, True, lo=ATT_BK)
        ot = jnp.concatenate([acc[:V_DIM] / acc[V_DIM:V_DIM + 1] for (_, acc) in carry], axis=0)
        o_ref[rows, :] = ot.T.astype(o_ref.dtype)

    for i in range(seq // bq):
        query_block(i)


def _prompt_attn(q, k, v, batch, seq):
    assert ATT_BQ == 2 * ATT_BK and seq % (2 * ATT_BQ) == 0 and seq % ATT_VT_COLS == 0
    n_tok = q.shape[0]
    return pl.pallas_call(
        _prompt_attn_kernel,
        grid=(batch, N_HEADS // ATT_HEADS),
        in_specs=[pl.BlockSpec((seq, ATT_HEADS * HEAD_PAD), lambda b, hp: (b, hp)),
                  pl.BlockSpec((seq, ATT_HEADS * HEAD_PAD), lambda b, hp: (b, hp)),
                  pl.BlockSpec((seq, ATT_HEADS * V_DIM), lambda b, hp: (b, hp))],
        out_specs=pl.BlockSpec((seq, ATT_HEADS * V_DIM), lambda b, hp: (b, hp)),
        out_shape=jax.ShapeDtypeStruct((n_tok, N_HEADS * V_DIM), BF16),
        scratch_shapes=[pltpu.VMEM((ATT_HEADS, V_DIM + ATT_ONES_ROWS, seq), BF16),
                        pltpu.VMEM((2, ATT_HEADS, ATT_BK, ATT_BQ), F32)],
        compiler_params=pltpu.CompilerParams(dimension_semantics=("arbitrary",) * 2,
                                             vmem_limit_bytes=VMEM_LIMIT),
        name="prompt_attn",
    )(q, k, v)


SAMPLE_SUB = 2048
SAMPLE_SEQS = 1
SAMPLE_SLOTS = 3


def _sample_blocks(past):
    return [min(SAMPLE_SUB, past - lo) for lo in range(0, past, SAMPLE_SUB)]


def _qabs_kernel(q_ref, gk_ref, wuk_ref, qlat_ref, qr_ref):
    for h in range(N_HEADS):
        qh = q_ref[:, h * HEAD_PAD:(h + 1) * HEAD_PAD] * gk_ref[...]
        qlat_ref[h] = _dot(qh[:, :QK_NOPE].astype(BF16), wuk_ref[h])
        qr_ref[h] = qh[:, QK_NOPE:QK_HEAD]


def _qabs_call(q, gk, wuk):
    n_tok = q.shape[0]
    kv_lora = wuk.shape[2]
    return pl.pallas_call(
        _qabs_kernel,
        out_shape=[jax.ShapeDtypeStruct((N_HEADS, n_tok, kv_lora), F32),
                   jax.ShapeDtypeStruct((N_HEADS, n_tok, QK_ROPE), F32)],
        compiler_params=pltpu.CompilerParams(vmem_limit_bytes=VMEM_LIMIT),
        name="q_absorb",
    )(q, gk, wuk)


def _sample_attn_kernel(layer, pt_ref, qlat_ref, qr_ref, cnew_ref, krnew_ref, wukt_ref, ckv_hbm, krt_hbm,
                        olat_ref, cbuf, kbuf, csem, ksem, lhs_ref, cb_ref):
    s = pl.program_id(0)
    n_steps = pl.num_programs(0)
    n_par = cnew_ref.shape[0]
    past = cbuf.shape[1] // n_par
    n_pages = past // PAGE
    t_new = cnew_ref.shape[1]
    kv_lora = cnew_ref.shape[2]
    n_rows = N_HEADS * t_new
    n_kn = N_HEADS * QK_NOPE
    n_slots = cbuf.shape[0]
    ahead = n_slots - 1
    slot = s % n_slots

    def copies(step, slot_):
        out = []
        for a in range(n_par):
            for p in range(n_pages):
                page = pt_ref[(step * n_par + a) * n_pages + p]
                keys = pl.ds(a * past + p * PAGE, PAGE)
                out.append(pltpu.make_async_copy(ckv_hbm.at[layer, page], cbuf.at[slot_, keys, :],
                                                 csem.at[slot_]))
                out.append(pltpu.make_async_copy(krt_hbm.at[layer, page], kbuf.at[slot_, :, keys],
                                                 ksem.at[slot_]))
        return out

    def start_all(cps):
        for n, cp in enumerate(cps):
            cp.start(priority=(n // 2) % 2)

    @pl.when(s == 0)
    def _():
        for a in range(n_par):
            lhs_ref[a, :n_kn, :] = wukt_ref[...]
        for d in range(ahead):
            start_all(copies(jnp.minimum(d, n_steps - 1), d))

    for cp in copies(s, slot):
        cp.wait()

    qrs = []
    for a in range(n_par):
        tok = slice(a * t_new, (a + 1) * t_new)
        lhs_ref[a, n_kn:, :] = qlat_ref[:, tok, :].reshape(n_rows, kv_lora).astype(BF16)
        qrs.append(qr_ref[:, tok, :].reshape(n_rows, QK_ROPE).astype(BF16))

    def scaled_scores(big, rope_scores, r2):
        n = big.shape[1]
        kn = big[:n_kn]
        ss = jnp.sum((kn * kn).reshape(N_HEADS, QK_NOPE, n), axis=1)
        rs = lax.rsqrt((ss + r2) * (1.0 / QK_HEAD) + EPS)
        sc = (big[n_kn:] + rope_scores).reshape(N_HEADS, t_new, n) * rs[:, None, :]
        return sc.reshape(n_rows, n)

    blocks = _sample_blocks(past)
    starts = [sum(blocks[:k]) for k in range(len(blocks))]

    def past_scores(a, k):
        keys = slice(a * past + starts[k], a * past + starts[k] + blocks[k])
        cb = cbuf[slot, keys, :].astype(BF16)
        cb_ref[a, k % 2, :blocks[k], :] = cb
        krt = kbuf[slot, :, keys]
        r2 = jnp.sum(krt * krt, axis=0, keepdims=True)
        return scaled_scores(_dot_nt(lhs_ref[a], cb), _dot(qrs[a], krt.astype(BF16)), r2)

    def new_scores(a):
        pad = PAGE - t_new
        cb_new = jnp.concatenate([cnew_ref[a], jnp.zeros((pad, kv_lora), F32)], axis=0).astype(BF16)
        kr_new = jnp.concatenate([krnew_ref[a], jnp.zeros((pad, QK_ROPE), F32)], axis=0)
        kr2 = kr_new * kr_new
        kr2_hi = kr2.astype(BF16)
        kr2_lo = (kr2 - kr2_hi.astype(F32)).astype(BF16)
        ones = jnp.ones((8, QK_ROPE), BF16)
        r2_new = (_dot_nt(ones, kr2_hi) + _dot_nt(ones, kr2_lo))[:1]
        sc = scaled_scores(_dot_nt(lhs_ref[a], cb_new), _dot_nt(qrs[a], kr_new.astype(BF16)), r2_new)
        row = lax.broadcasted_iota(jnp.int32, (n_rows, PAGE), 0)
        col = lax.broadcasted_iota(jnp.int32, (n_rows, PAGE), 1)
        return jnp.where(col <= row % t_new, sc, -1e30), cb_new

    n_sub = len(blocks)
    new = [new_scores(a) for a in range(n_par)]
    sc_next = [past_scores(a, 0) for a in range(n_par)]

    nxt = jnp.minimum(s + ahead, n_steps - 1)
    nxt_slot = (s + ahead) % n_slots
    start_all(copies(nxt, nxt_slot))

    state = []
    for a in range(n_par):
        sc_new, cb_new = new[a]
        m = jnp.max(sc_new, axis=-1, keepdims=True)
        p = jnp.exp2(sc_new - m)
        state.append((m, jnp.sum(p, axis=-1, keepdims=True), _dot(p.astype(BF16), cb_new)))
    for k in range(n_sub):
        sc_cur = sc_next
        if k + 1 < n_sub:
            sc_next = [past_scores(a, k + 1) for a in range(n_par)]
        for a in range(n_par):
            m, l, acc = state[a]
            m_new = jnp.maximum(m, jnp.max(sc_cur[a], axis=-1, keepdims=True))
            alpha = jnp.exp2(m - m_new)
            p = jnp.exp2(sc_cur[a] - m_new)
            l = alpha * l + jnp.sum(p, axis=-1, keepdims=True)
            acc = alpha * acc + _dot(p.astype(BF16), cb_ref[a, k % 2, :blocks[k], :])
            state[a] = (m_new, l, acc)
    for a in range(n_par):
        _, l, acc = state[a]
        olat_ref[a] = acc / l

    @pl.when(s == n_steps - 1)
    def _():
        for d in range(1, n_slots):
            for cp in copies(nxt, (s + d) % n_slots):
                cp.wait()


def _sample_attn(layer, page_table, qlat, qr, c_new, kr_new, wukt, cache_ckv, cache_krt):
    n_seq, n_pages = page_table.shape
    t_new = c_new.shape[1]
    kv_lora = c_new.shape[2]
    n_rows = N_HEADS * t_new
    past = n_pages * PAGE
    n_par = SAMPLE_SEQS
    assert n_seq % n_par == 0
    grid_spec = pltpu.PrefetchScalarGridSpec(
        num_scalar_prefetch=1,
        grid=(n_seq // n_par,),
        in_specs=[pl.BlockSpec((N_HEADS, n_par * t_new, kv_lora), lambda s, pt: (0, s, 0)),
                  pl.BlockSpec((N_HEADS, n_par * t_new, QK_ROPE), lambda s, pt: (0, s, 0)),
                  pl.BlockSpec((n_par, t_new, kv_lora), lambda s, pt: (s, 0, 0)),
                  pl.BlockSpec((n_par, t_new, QK_ROPE), lambda s, pt: (s, 0, 0)),
                  pl.BlockSpec(wukt.shape, lambda s, pt: (0, 0)),
                  pl.BlockSpec(memory_space=pl.ANY),
                  pl.BlockSpec(memory_space=pl.ANY)],
        out_specs=pl.BlockSpec((n_par, n_rows, kv_lora), lambda s, pt: (s, 0, 0)),
        scratch_shapes=[pltpu.VMEM((SAMPLE_SLOTS, n_par * past, kv_lora), F32),
                        pltpu.VMEM((SAMPLE_SLOTS, QK_ROPE, n_par * past), F32),
                        pltpu.SemaphoreType.DMA((SAMPLE_SLOTS,)),
                        pltpu.SemaphoreType.DMA((SAMPLE_SLOTS,)),
                        pltpu.VMEM((n_par, N_HEADS * QK_NOPE + n_rows, kv_lora), BF16),
                        pltpu.VMEM((n_par, 2, max(_sample_blocks(past)), kv_lora), BF16)],
    )
    return pl.pallas_call(
        functools.partial(_sample_attn_kernel, layer),
        grid_spec=grid_spec,
        out_shape=jax.ShapeDtypeStruct((n_seq, n_rows, kv_lora), F32),
        compiler_params=pltpu.CompilerParams(dimension_semantics=("arbitrary",),
                                             vmem_limit_bytes=VMEM_LIMIT),
        name="sample_attn",
    )(page_table.reshape(-1), qlat, qr, c_new, kr_new, wukt, cache_ckv, cache_krt)


def _oproj_kernel(olat_ref, wuv_ref, o_ref):
    n_seq, n_rows, kv_lora = olat_ref.shape
    t_new = n_rows // N_HEADS
    acc = None
    for h in range(N_HEADS):
        x = olat_ref[:, h * t_new:(h + 1) * t_new, :].reshape(n_seq * t_new, kv_lora).astype(BF16)
        part = _dot(x, wuv_ref[h])
        acc = part if acc is None else acc + part
    o_ref[...] = acc.astype(o_ref.dtype)


def _oproj_call(olat, wuv):
    n_seq, n_rows, _ = olat.shape
    return pl.pallas_call(
        _oproj_kernel,
        out_shape=jax.ShapeDtypeStruct((n_seq * n_rows // N_HEADS, N_HEADS * V_DIM), BF16),
        compiler_params=pltpu.CompilerParams(vmem_limit_bytes=VMEM_LIMIT),
        name="o_proj",
    )(olat, wuv)


def _pad_heads(w, width):
    k = w.shape[0]
    w = w.reshape(k, N_HEADS, width)
    return jnp.pad(w, ((0, 0), (0, 0), (0, HEAD_PAD - width))).reshape(k, N_HEADS * HEAD_PAD)


def _rot_cols(w):
    half = w.shape[-1] // 2
    return jnp.concatenate([-w[..., half:], w[..., :half]], axis=-1)


def _rope_block(w):
    return jnp.pad(w, ((0, 0), (QK_NOPE, HEAD_PAD - QK_HEAD)))


def _head_gain(g):
    return jnp.pad(g, (0, HEAD_PAD - QK_HEAD)).reshape(1, HEAD_PAD)


def _rope_tables(n_pos, offset):
    pos = jnp.arange(n_pos, dtype=F32) + offset
    inv_freq = ROPE_THETA ** (-jnp.arange(0, QK_ROPE, 2, dtype=F32) / QK_ROPE)
    ang = pos[:, None] * inv_freq[None, :]
    cos, sin = jnp.cos(ang), jnp.sin(ang)
    ones = jnp.ones((n_pos, QK_NOPE), F32)
    tail = jnp.ones((n_pos, HEAD_PAD - QK_HEAD), F32)
    cos_tab = jnp.concatenate([ones, cos, cos, tail], axis=1)
    sin_tab = jnp.concatenate([0 * ones, sin, sin, 0 * tail], axis=1)
    return cos_tab, sin_tab


def _layer_weights(i, ffn1_norm, ffn1_w_in, ffn1_w_out, mix_norm, w_in, qa_norm, kva_norm, w_uq, w_ukv,
                   q_norm, k_norm, sgu_ln_g, sgu_ln_b, sgu_w_s, sgu_b_s, w_a_out, w_b_out, w_o,
                   ffn2_norm, ffn2_w_in, ffn2_w_out, ple_norm, ple_w_gate, ple_w_proj):
    a_width = sgu_ln_g.shape[1]
    q_lora = qa_norm.shape[1]
    kv_lora = kva_norm.shape[1]
    s1 = 2 * a_width
    s2 = s1 + q_lora
    s3 = s2 + kv_lora
    s4 = s3 + QK_ROPE
    wi = w_in[i]
    w_kr = wi[:, s3:s4]
    wpre = jnp.concatenate([wi[:, :s3], _rope_block(w_kr), _rope_block(_rot_cols(w_kr))], axis=1)

    wq = w_uq[i].reshape(q_lora, N_HEADS, QK_HEAD)
    wq_rot = jnp.concatenate([jnp.zeros_like(wq[..., :QK_NOPE]), _rot_cols(wq[..., QK_NOPE:])], axis=-1)
    wuq = jnp.stack([_pad_heads(wq.reshape(q_lora, -1), QK_HEAD).reshape(q_lora, N_HEADS, HEAD_PAD),
                     _pad_heads(wq_rot.reshape(q_lora, -1), QK_HEAD).reshape(q_lora, N_HEADS, HEAD_PAD)],
                    axis=2).reshape(q_lora, 2 * N_HEADS * HEAD_PAD)

    wkv = w_ukv[i].reshape(kv_lora, N_HEADS, QK_NOPE + V_DIM)
    w_uk = wkv[..., :QK_NOPE]
    w_uv = wkv[..., QK_NOPE:]
    wukv = jnp.concatenate([_pad_heads(w_uk.reshape(kv_lora, -1), QK_NOPE),
                            w_uv.reshape(kv_lora, -1)], axis=1)
    wuk_t = jnp.transpose(w_uk, (1, 2, 0))
    eye = jnp.eye(N_HEADS, dtype=F32)
    wuv_blk = jnp.einsum('khd,hg->hkgd', w_uv, eye).reshape(N_HEADS, kv_lora, N_HEADS * V_DIM)

    row = lambda g: g[i].reshape(1, -1)
    return {
        'f1g': row(ffn1_norm), 'f1in': ffn1_w_in[i].astype(BF16), 'f1out': ffn1_w_out[i].astype(BF16),
        'mixg': row(mix_norm), 'wpre': wpre.astype(BF16), 'wgates': wi[:, s4:].astype(BF16),
        'qag': row(qa_norm), 'kvag': row(kva_norm), 'wuq': wuq.astype(BF16), 'wukv': wukv.astype(BF16),
        'qg': _head_gain(q_norm[i]), 'kg': _head_gain(k_norm[i]),
        'wuk_t': wuk_t.astype(BF16), 'wuv_blk': wuv_blk.astype(BF16),
        'lng': row(sgu_ln_g), 'lnb': row(sgu_ln_b), 'w_s': sgu_w_s[i], 'b_s': sgu_b_s[i],
        'wa': w_a_out[i].astype(BF16), 'wb': w_b_out[i].astype(BF16), 'wo': w_o[i].astype(BF16),
        'f2g': row(ffn2_norm), 'f2in': ffn2_w_in[i].astype(BF16), 'f2out': ffn2_w_out[i].astype(BF16),
        'pleg': row(ple_norm), 'plewg': ple_w_gate[i].astype(BF16), 'plewp': ple_w_proj[i].astype(BF16),
    }


def _mix_tables(w_s, b_s, period, group_dim):
    r = jnp.arange(CHUNK)
    same = (r[:, None] // period) == (r[None, :] // period)
    causal = (r[None, :] % period) <= (r[:, None] % period)
    reps = CHUNK // period
    mixw = jnp.where((same & causal)[None], jnp.tile(w_s[:, :period, :period], (1, reps, reps)), 0.0)
    bias = jnp.tile(b_s[:, :period], (1, reps))
    mixb = jnp.repeat(bias.T, group_dim, axis=1)
    return mixw.astype(BF16), mixb


def kernel(x_prompt, x_sample, cache_ckv, cache_krope, page_table, p_prompt, p_sample, ffn1_norm, ffn1_w_in, ffn1_w_out, mix_norm, w_in, qa_norm, kva_norm, w_uq, w_ukv, q_norm, k_norm, sgu_ln_g, sgu_ln_b, sgu_w_s, sgu_b_s, w_a_out, w_b_out, w_o, ffn2_norm, ffn2_w_in, ffn2_w_out, ple_norm, ple_w_gate, ple_w_proj):
    depth = w_in.shape[0]
    batch, seq, d_model = x_prompt.shape
    dec_batch, dec_seq, _ = x_sample.shape
    past_len = page_table.shape[1] * cache_ckv.shape[2]
    assert seq % TOKEN_TILE == 0 and TOKEN_TILE % dec_seq == 0 and CHUNK % dec_seq == 0
    assert cache_ckv.shape[2] == PAGE

    yp = x_prompt.reshape(batch * seq, d_model)
    ys = x_sample.reshape(dec_batch * dec_seq, d_model)
    cos_p, sin_p = _rope_tables(seq, 0)
    cos_s, sin_s = _rope_tables(dec_seq, past_len)
    reps = TOKEN_TILE // dec_seq
    cos_s, sin_s = jnp.tile(cos_s, (reps, 1)), jnp.tile(sin_s, (reps, 1))

    outs = [[] for _ in range(5)]
    for i in range(depth):
        w = _layer_weights(i, ffn1_norm, ffn1_w_in, ffn1_w_out, mix_norm, w_in, qa_norm, kva_norm, w_uq,
                           w_ukv, q_norm, k_norm, sgu_ln_g, sgu_ln_b, sgu_w_s, sgu_b_s, w_a_out, w_b_out,
                           w_o, ffn2_norm, ffn2_w_in, ffn2_w_out, ple_norm, ple_w_gate, ple_w_proj)

        group_dim = w['lng'].shape[1] // A_GROUPS
        w['mixw'], w['mixb'] = _mix_tables(w['w_s'], w['b_s'], CHUNK, group_dim)
        h1, ain, q, ckv_p, kr_p, k, v = _pre_call(yp, cos_p, sin_p, w, True, False, BF16)
        o = _prompt_attn(q, k, v, batch, seq)
        yp = _post_call(h1, ain, o, p_prompt[i].reshape(batch * seq, -1), w)

        w['mixw'], w['mixb'] = _mix_tables(w['w_s'], w['b_s'], dec_seq, group_dim)
        h1, ain, q, ckv_s, kr_s, vn_s = _pre_call(ys, cos_s, sin_s, w, False, True, F32)
        qlat, qr = _qabs_call(q, w['kg'], w['wuk_t'])
        kv_lora = ckv_s.shape[1]
        olat = _sample_attn(i, page_table, qlat, qr,
                            ckv_s.reshape(dec_batch, dec_seq, kv_lora),
                            kr_s.reshape(dec_batch, dec_seq, QK_ROPE),
                            w['wuk_t'].reshape(N_HEADS * QK_NOPE, kv_lora),
                            cache_ckv, jnp.swapaxes(cache_krope, 2, 3))
        o = _oproj_call(olat, w['wuv_blk'])
        ys = _post_call(h1, ain, o, p_sample[i].reshape(dec_batch * dec_seq, -1), w)

        outs[0].append(ckv_p.reshape(batch, seq, -1))
        outs[1].append(kr_p.reshape(batch, seq, -1))
        outs[2].append(ckv_s.reshape(dec_batch, dec_seq, -1))
        outs[3].append(kr_s.reshape(dec_batch, dec_seq, -1))
        outs[4].append(vn_s.reshape(dec_batch, dec_seq, -1))

    return (yp.reshape(batch, seq, d_model), ys.reshape(dec_batch, dec_seq, d_model),
            jnp.stack(outs[0]), jnp.stack(outs[1]), jnp.stack(outs[2]), jnp.stack(outs[3]),
            jnp.stack(outs[4]))
```

```python
import functools

import jax
import jax.numpy as jnp
from jax import lax
from jax.experimental import pallas as pl
from jax.experimental.pallas import tpu as pltpu

F32 = jnp.float32
BF16 = jnp.bfloat16

EPS = 1e-6
N_HEADS = 8
QK_NOPE = 64
QK_ROPE = 32
QK_HEAD = QK_NOPE + QK_ROPE
V_DIM = 64
HEAD_PAD = 128
A_GROUPS = 4
CHUNK = 128
ROPE_THETA = 10000.0
PAGE = 128
LOG2_E = 1.4426950408889634

FF_CHUNK = 256
TOKEN_TILE = 512
VMEM_LIMIT = 56 * 1024 * 1024

_NT = (((1,), (1,)), ((), ()))


def _dot(a, b):
    return jnp.dot(a, b, preferred_element_type=F32)


def _dot_nt(a, b):
    return lax.dot_general(a, b, _NT, preferred_element_type=F32)


def _rms(x, g):
    return x * lax.rsqrt(jnp.mean(x * x, axis=-1, keepdims=True) + EPS) * g


def _ffn_half(x, g_ref, win_ref, wout_ref, act_ref):
    d_ff = wout_ref.shape[0]
    xn = _rms(x, g_ref[...]).astype(BF16)
    for c in range(d_ff // FF_CHUNK):
        lo = c * FF_CHUNK
        a = _dot(xn, win_ref[:, lo:lo + FF_CHUNK])
        b = _dot(xn, win_ref[:, d_ff + lo:d_ff + lo + FF_CHUNK])
        act_ref[:, lo:lo + FF_CHUNK] = (a * jax.nn.sigmoid(a) * b).astype(BF16)
    return x + 0.5 * _dot(act_ref[...], wout_ref[...])


def _head_norm(xh, gain):
    ss = jnp.sum(xh * xh, axis=-1, keepdims=True)
    return xh * lax.rsqrt(ss * (1.0 / QK_HEAD) + EPS) * gain


def _pre_kernel(expand_kv, write_vn, *refs):
    (x_ref, cos_ref, sin_ref, f1g_ref, f1in_ref, f1out_ref, mixg_ref, wpre_ref, qag_ref, kvag_ref,
     wuq_ref, qg_ref, lng_ref, lnb_ref, mixw_ref, mixb_ref) = refs[:16]
    pos = 16
    if expand_kv:
        wukv_ref, kg_ref = refs[pos:pos + 2]
        pos += 2
    h1_ref, ain_ref, q_ref, ckv_ref, kr_ref = refs[pos:pos + 5]
    pos += 5
    if expand_kv:
        k_ref, v_ref = refs[pos:pos + 2]
        pos += 2
    if write_vn:
        vn_ref = refs[pos]
        pos += 1
    act_ref = refs[pos]

    a_width = lng_ref.shape[1]
    q_lora = qag_ref.shape[1]
    kv_lora = kvag_ref.shape[1]
    rows = x_ref.shape[0]

    h1 = _ffn_half(x_ref[...], f1g_ref, f1in_ref, f1out_ref, act_ref)
    h1_ref[...] = h1
    n = _rms(h1, mixg_ref[...]).astype(BF16)

    c0 = 2 * a_width
    c1 = c0 + q_lora
    c2 = c1 + kv_lora
    nq = N_HEADS * HEAD_PAD
    p_cq = _dot(n, wpre_ref[:, c0:c1])
    p_ckv = _dot(n, wpre_ref[:, c1:c2])
    p_kr = _dot(n, wpre_ref[:, c2:c2 + 2 * HEAD_PAD])
    p_uv = _dot(n, wpre_ref[:, :c0])

    uv = jax.nn.gelu(p_uv)
    u, v = uv[:, :a_width], uv[:, a_width:]
    vc = v - jnp.mean(v, axis=-1, keepdims=True)
    vn = vc * lax.rsqrt(jnp.mean(vc * vc, axis=-1, keepdims=True) + EPS) * lng_ref[...] + lnb_ref[...]
    if write_vn:
        vn_ref[...] = vn
    vnb = vn.astype(BF16)

    cos = cos_ref[...]
    sin = sin_ref[...]
    cq = _rms(p_cq, qag_ref[...]).astype(BF16)
    scale = QK_HEAD ** -0.5 * LOG2_E
    for h in range(N_HEADS):
        t = _dot(cq, wuq_ref[:, 2 * h * HEAD_PAD:2 * (h + 1) * HEAD_PAD])
        qh = t[:, :HEAD_PAD] * cos + t[:, HEAD_PAD:] * sin
        q_ref[:, h * HEAD_PAD:(h + 1) * HEAD_PAD] = (_head_norm(qh, qg_ref[...]) * scale).astype(q_ref.dtype)
    c_kv = _rms(p_ckv, kvag_ref[...])
    ckv_ref[...] = c_kv
    kr_blk = p_kr[:, :HEAD_PAD] * cos + p_kr[:, HEAD_PAD:] * sin
    kr_ref[...] = kr_blk[:, QK_NOPE:QK_HEAD]
    if expand_kv:
        ckvb = c_kv.astype(BF16)
        for hp in range(N_HEADS // 2):
            t = _dot(ckvb, wukv_ref[:, 2 * hp * HEAD_PAD:2 * (hp + 1) * HEAD_PAD])
            for hh in range(2):
                hs = slice((2 * hp + hh) * HEAD_PAD, (2 * hp + hh + 1) * HEAD_PAD)
                kh = t[:, hh * HEAD_PAD:(hh + 1) * HEAD_PAD] + kr_blk
                k_ref[:, hs] = _head_norm(kh, kg_ref[...]).astype(k_ref.dtype)
        v_ref[...] = _dot(ckvb, wukv_ref[:, nq:]).astype(v_ref.dtype)

    gd = a_width // A_GROUPS
    for r in range(rows // CHUNK):
        rs = slice(r * CHUNK, (r + 1) * CHUNK)
        for g in range(A_GROUPS):
            gs = slice(g * gd, (g + 1) * gd)
            z = _dot(mixw_ref[g], vnb[rs, gs]) + mixb_ref[:, gs]
            ain_ref[rs, gs] = (u[rs, gs] * z).astype(ain_ref.dtype)


def _const_spec(shape):
    nd = len(shape)
    return pl.BlockSpec(shape, lambda i, _nd=nd: (0,) * _nd, pipeline_mode=pl.Buffered(1))


def _pre_call(x, cos_tab, sin_tab, w, expand_kv, write_vn, q_dtype):
    n_tok, d_model = x.shape
    tm = TOKEN_TILE
    n_tab = cos_tab.shape[0] // tm
    a_width = w['lng'].shape[1]
    kv_lora = w['kvag'].shape[1]
    d_ff = w['f1out'].shape[0]
    nq = N_HEADS * HEAD_PAD

    def row_spec(cols):
        return pl.BlockSpec((tm, cols), lambda i: (i, 0))

    tab_spec = pl.BlockSpec((tm, HEAD_PAD), lambda i: (i % n_tab, 0))
    consts = [w['f1g'], w['f1in'], w['f1out'], w['mixg'], w['wpre'], w['qag'], w['kvag'], w['wuq'],
              w['qg'], w['lng'], w['lnb'], w['mixw'], w['mixb']]
    if expand_kv:
        consts += [w['wukv'], w['kg']]
    in_specs = [row_spec(d_model), tab_spec, tab_spec] + [_const_spec(c.shape) for c in consts]
    out_shape = [jax.ShapeDtypeStruct((n_tok, d_model), F32),
                 jax.ShapeDtypeStruct((n_tok, a_width), BF16),
                 jax.ShapeDtypeStruct((n_tok, nq), q_dtype),
                 jax.ShapeDtypeStruct((n_tok, kv_lora), F32),
                 jax.ShapeDtypeStruct((n_tok, QK_ROPE), F32)]
    out_specs = [row_spec(d_model), row_spec(a_width), row_spec(nq), row_spec(kv_lora), row_spec(QK_ROPE)]
    if expand_kv:
        out_shape += [jax.ShapeDtypeStruct((n_tok, nq), BF16),
                      jax.ShapeDtypeStruct((n_tok, N_HEADS * V_DIM), BF16)]
        out_specs += [row_spec(nq), row_spec(N_HEADS * V_DIM)]
    if write_vn:
        out_shape.append(jax.ShapeDtypeStruct((n_tok, a_width), F32))
        out_specs.append(row_spec(a_width))
    return pl.pallas_call(
        functools.partial(_pre_kernel, expand_kv, write_vn),
        grid=(n_tok // tm,),
        in_specs=in_specs,
        out_specs=out_specs,
        out_shape=out_shape,
        scratch_shapes=[pltpu.VMEM((tm, d_ff), BF16)],
        compiler_params=pltpu.CompilerParams(dimension_semantics=("arbitrary",),
                                             vmem_limit_bytes=VMEM_LIMIT),
        name="pre_expand" if expand_kv else "pre_latent",
    )(x, cos_tab, sin_tab, *consts)


def _post_kernel(h1_ref, ain_ref, o_ref, p_ref, mixg_ref, wg_ref, wa_ref, wb_ref, wo_ref,
                 f2g_ref, f2in_ref, f2out_ref, pleg_ref, plewg_ref, plewp_ref, y_ref, act_ref):
    d_model = h1_ref.shape[1]
    h1 = h1_ref[...]
    n = _rms(h1, mixg_ref[...]).astype(BF16)
    ga = jax.nn.sigmoid(_dot(n, wg_ref[:, :d_model]))
    gb = jax.nn.sigmoid(_dot(n, wg_ref[:, d_model:]))
    m = ga * _dot(ain_ref[...], wa_ref[...]) + gb * _dot(o_ref[...], wb_ref[...])
    h2 = h1 + _dot(m.astype(BF16), wo_ref[...])
    h3 = _ffn_half(h2, f2g_ref, f2in_ref, f2out_ref, act_ref)
    hn = _rms(h3, pleg_ref[...]).astype(BF16)
    gate = jax.nn.sigmoid(_dot(hn, plewg_ref[...]))
    y_ref[...] = h3 + gate * _dot(p_ref[...].astype(BF16), plewp_ref[...])


def _post_call(h1, ain, o, p, w):
    n_tok, d_model = h1.shape
    tm = TOKEN_TILE
    d_ff = w['f2out'].shape[0]

    def row_spec(cols):
        return pl.BlockSpec((tm, cols), lambda i: (i, 0))

    consts = [w['mixg'], w['wgates'], w['wa'], w['wb'], w['wo'], w['f2g'], w['f2in'], w['f2out'],
              w['pleg'], w['plewg'], w['plewp']]
    return pl.pallas_call(
        _post_kernel,
        grid=(n_tok // tm,),
        in_specs=[row_spec(d_model), row_spec(ain.shape[1]), row_spec(o.shape[1]), row_spec(p.shape[1])]
        + [_const_spec(c.shape) for c in consts],
        out_specs=row_spec(d_model),
        out_shape=jax.ShapeDtypeStruct((n_tok, d_model), F32),
        scratch_shapes=[pltpu.VMEM((tm, d_ff), BF16)],
        compiler_params=pltpu.CompilerParams(dimension_semantics=("arbitrary",),
                                             vmem_limit_bytes=VMEM_LIMIT),
        name="post",
    )(h1, ain, o, p, *consts)


ATT_BQ = 512
ATT_BK = 256
ATT_VT_COLS = 512
ATT_ONES_ROWS = 16
ATT_HEADS = 2


def _prompt_attn_kernel(q_ref, k_ref, v_ref, o_ref, vt_ref, st_ref):
    bq = ATT_BQ
    seq = k_ref.shape[0]

    heads = range(ATT_HEADS)
    for r in range(seq // ATT_VT_COLS):
        rs = slice(r * ATT_VT_COLS, (r + 1) * ATT_VT_COLS)
        vt = v_ref[rs, :].astype(F32).T.astype(BF16)
        for hh in heads:
            vt_ref[hh, :V_DIM, rs] = vt[hh * V_DIM:(hh + 1) * V_DIM]
            vt_ref[hh, V_DIM:, rs] = jnp.ones((ATT_ONES_ROWS, ATT_VT_COLS), BF16)

    def query_block(i):
        rows = slice(i * bq, (i + 1) * bq)
        qts = [q_ref[rows, hh * HEAD_PAD:(hh + 1) * HEAD_PAD].astype(F32).T.astype(BF16) for hh in heads]

        def scores(j, slot, lo=0):
            start = j * ATT_BK
            for hh in heads:
                st_ref[slot, hh, :, lo:] = _dot(
                    k_ref[pl.ds(start, ATT_BK), hh * HEAD_PAD:(hh + 1) * HEAD_PAD], qts[hh][:, lo:])

        def update(j, slot, carry, masked, lo=0):
            start = j * ATT_BK
            out = []
            for hh in heads:
                m, acc = carry[hh]
                st = st_ref[slot, hh, :, lo:]
                if masked:
                    key = start + lax.broadcasted_iota(jnp.int32, st.shape, 0)
                    qry = i * bq + lo + lax.broadcasted_iota(jnp.int32, st.shape, 1)
                    st = jnp.where(key <= qry, st, -1e30)
                m_new = jnp.maximum(m[:, lo:], jnp.max(st, axis=0, keepdims=True))
                p = jnp.exp2(st - m_new)
                acc_new = (jnp.exp2(m[:, lo:] - m_new) * acc[:, lo:]
                           + _dot(vt_ref[hh, :, pl.ds(start, ATT_BK)], p.astype(BF16)))
                if lo:
                    m_new = jnp.concatenate([m[:, :lo], m_new], axis=1)
                    acc_new = jnp.concatenate([acc[:, :lo], acc_new], axis=1)
                out.append((m_new, acc_new))
            return tuple(out)

        one = (jnp.full((1, bq), -1e30, F32), jnp.zeros((V_DIM + ATT_ONES_ROWS, bq), F32))

        carry = (one,) * ATT_HEADS
        scores(0, 0)
        for j in range(2 * i):
            scores(j + 1, (j + 1) % 2)
            carry = update(j, j % 2, carry, False)
        scores(2 * i + 1, 1, lo=ATT_BK)
        carry = update(2 * i, 0, carry, True)
        carry = update(2 * i + 1, 1, carry, True, lo=ATT_BK)
        ot = jnp.concatenate([acc[:V_DIM] / acc[V_DIM:V_DIM + 1] for (_, acc) in carry], axis=0)
        o_ref[rows, :] = ot.T.astype(o_ref.dtype)

    for i in range(seq // bq):
        query_block(i)


def _prompt_attn(q, k, v, batch, seq):
    assert ATT_BQ == 2 * ATT_BK and seq % (2 * ATT_BQ) == 0 and seq % ATT_VT_COLS == 0
    n_tok = q.shape[0]
    return pl.pallas_call(
        _prompt_attn_kernel,
        grid=(batch, N_HEADS // ATT_HEADS),
        in_specs=[pl.BlockSpec((seq, ATT_HEADS * HEAD_PAD), lambda b, hp: (b, hp)),
                  pl.BlockSpec((seq, ATT_HEADS * HEAD_PAD), lambda b, hp: (b, hp)),
                  pl.BlockSpec((seq, ATT_HEADS * V_DIM), lambda b, hp: (b, hp))],
        out_specs=pl.BlockSpec((seq, ATT_HEADS * V_DIM), lambda b, hp: (b, hp)),
        out_shape=jax.ShapeDtypeStruct((n_tok, N_HEADS * V_DIM), BF16),
        scratch_shapes=[pltpu.VMEM((ATT_HEADS, V_DIM + ATT_ONES_ROWS, seq), BF16),
                        pltpu.VMEM((2, ATT_HEADS, ATT_BK, ATT_BQ), F32)],
        compiler_params=pltpu.CompilerParams(dimension_semantics=("arbitrary",) * 2,
                                             vmem_limit_bytes=VMEM_LIMIT),
        name="prompt_attn",
    )(q, k, v)


SAMPLE_SUB = 2048
SAMPLE_SEQS = 1
SAMPLE_SLOTS = 3


def _sample_blocks(past):
    return [min(SAMPLE_SUB, past - lo) for lo in range(0, past, SAMPLE_SUB)]


def _qabs_kernel(q_ref, gk_ref, wuk_ref, qlat_ref, qr_ref):
    for h in range(N_HEADS):
        qh = q_ref[:, h * HEAD_PAD:(h + 1) * HEAD_PAD] * gk_ref[...]
        qlat_ref[h] = _dot(qh[:, :QK_NOPE].astype(BF16), wuk_ref[h])
        qr_ref[h] = qh[:, QK_NOPE:QK_HEAD]


def _qabs_call(q, gk, wuk):
    n_tok = q.shape[0]
    kv_lora = wuk.shape[2]
    return pl.pallas_call(
        _qabs_kernel,
        out_shape=[jax.ShapeDtypeStruct((N_HEADS, n_tok, kv_lora), F32),
                   jax.ShapeDtypeStruct((N_HEADS, n_tok, QK_ROPE), F32)],
        compiler_params=pltpu.CompilerParams(vmem_limit_bytes=VMEM_LIMIT),
        name="q_absorb",
    )(q, gk, wuk)


def _sample_attn_kernel(layer, pt_ref, qlat_ref, qr_ref, cnew_ref, krnew_ref, wukt_ref, ckv_hbm, krt_hbm,
                        olat_ref, cbuf, kbuf, csem, ksem, lhs_ref, cb_ref):
    s = pl.program_id(0)
    n_steps = pl.num_programs(0)
    n_par = cnew_ref.shape[0]
    past = cbuf.shape[1] // n_par
    n_pages = past // PAGE
    t_new = cnew_ref.shape[1]
    kv_lora = cnew_ref.shape[2]
    n_rows = N_HEADS * t_new
    n_kn = N_HEADS * QK_NOPE
    n_slots = cbuf.shape[0]
    ahead = n_slots - 1
    slot = s % n_slots

    def copies(step, slot_):
        out = []
        for a in range(n_par):
            for p in range(n_pages):
                page = pt_ref[(step * n_par + a) * n_pages + p]
                keys = pl.ds(a * past + p * PAGE, PAGE)
                out.append(pltpu.make_async_copy(ckv_hbm.at[layer, page], cbuf.at[slot_, keys, :],
                                                 csem.at[slot_]))
                out.append(pltpu.make_async_copy(krt_hbm.at[layer, page], kbuf.at[slot_, :, keys],
                                                 ksem.at[slot_]))
        return out

    def start_all(cps):
        for n, cp in enumerate(cps):
            cp.start(priority=(n // 2) % 2)

    @pl.when(s == 0)
    def _():
        for a in range(n_par):
            lhs_ref[a, :n_kn, :] = wukt_ref[...]
        for d in range(ahead):
            start_all(copies(jnp.minimum(d, n_steps - 1), d))

    for cp in copies(s, slot):
        cp.wait()

    qrs = []
    for a in range(n_par):
        tok = slice(a * t_new, (a + 1) * t_new)
        lhs_ref[a, n_kn:, :] = qlat_ref[:, tok, :].reshape(n_rows, kv_lora).astype(BF16)
        qrs.append(qr_ref[:, tok, :].reshape(n_rows, QK_ROPE).astype(BF16))

    def scaled_scores(big, rope_scores, r2):
        n = big.shape[1]
        kn = big[:n_kn]
        ss = jnp.sum((kn * kn).reshape(N_HEADS, QK_NOPE, n), axis=1)
        rs = lax.rsqrt((ss + r2) * (1.0 / QK_HEAD) + EPS)
        sc = (big[n_kn:] + rope_scores).reshape(N_HEADS, t_new, n) * rs[:, None, :]
        return sc.reshape(n_rows, n)

    blocks = _sample_blocks(past)
    starts = [sum(blocks[:k]) for k in range(len(blocks))]

    def past_scores(a, k):
        keys = slice(a * past + starts[k], a * past + starts[k] + blocks[k])
        cb = cbuf[slot, keys, :].astype(BF16)
        cb_ref[a, k % 2, :blocks[k], :] = cb
        krt = kbuf[slot, :, keys]
        r2 = jnp.sum(krt * krt, axis=0, keepdims=True)
        return scaled_scores(_dot_nt(lhs_ref[a], cb), _dot(qrs[a], krt.astype(BF16)), r2)

    def new_scores(a):
        pad = PAGE - t_new
        cb_new = jnp.concatenate([cnew_ref[a], jnp.zeros((pad, kv_lora), F32)], axis=0).astype(BF16)
        kr_new = jnp.concatenate([krnew_ref[a], jnp.zeros((pad, QK_ROPE), F32)], axis=0)
        kr2 = kr_new * kr_new
        kr2_hi = kr2.astype(BF16)
        kr2_lo = (kr2 - kr2_hi.astype(F32)).astype(BF16)
        ones = jnp.ones((8, QK_ROPE), BF16)
        r2_new = (_dot_nt(ones, kr2_hi) + _dot_nt(ones, kr2_lo))[:1]
        sc = scaled_scores(_dot_nt(lhs_ref[a], cb_new), _dot_nt(qrs[a], kr_new.astype(BF16)), r2_new)
        row = lax.broadcasted_iota(jnp.int32, (n_rows, PAGE), 0)
        col = lax.broadcasted_iota(jnp.int32, (n_rows, PAGE), 1)
        return jnp.where(col <= row % t_new, sc, -1e30), cb_new

    n_sub = len(blocks)
    new = [new_scores(a) for a in range(n_par)]
    sc_next = [past_scores(a, 0) for a in range(n_par)]

    nxt = jnp.minimum(s + ahead, n_steps - 1)
    nxt_slot = (s + ahead) % n_slots
    start_all(copies(nxt, nxt_slot))

    state = []
    for a in range(n_par):
        sc_new, cb_new = new[a]
        m = jnp.max(sc_new, axis=-1, keepdims=True)
        p = jnp.exp2(sc_new - m)
        state.append((m, jnp.sum(p, axis=-1, keepdims=True), _dot(p.astype(BF16), cb_new)))
    for k in range(n_sub):
        sc_cur = sc_next
        if k + 1 < n_sub:
            sc_next = [past_scores(a, k + 1) for a in range(n_par)]
        for a in range(n_par):
            m, l, acc = state[a]
            m_new = jnp.maximum(m, jnp.max(sc_cur[a], axis=-1, keepdims=True))
            alpha = jnp.exp2(m - m_new)
            p = jnp.exp2(sc_cur[a] - m_new)
            l = alpha * l + jnp.sum(p, axis=-1, keepdims=True)
            acc = alpha * acc + _dot(p.astype(BF16), cb_ref[a, k % 2, :blocks[k], :])
            state[a] = (m_new, l, acc)
    for a in range(n_par):
        _, l, acc = state[a]
        olat_ref[a] = acc / l

    @pl.when(s == n_steps - 1)
    def _():
        for d in range(1, n_slots):
            for cp in copies(nxt, (s + d) % n_slots):
                cp.wait()


def _sample_attn(layer, page_table, qlat, qr, c_new, kr_new, wukt, cache_ckv, cache_krt):
    n_seq, n_pages = page_table.shape
    t_new = c_new.shape[1]
    kv_lora = c_new.shape[2]
    n_rows = N_HEADS * t_new
    past = n_pages * PAGE
    n_par = SAMPLE_SEQS
    assert n_seq % n_par == 0
    grid_spec = pltpu.PrefetchScalarGridSpec(
        num_scalar_prefetch=1,
        grid=(n_seq // n_par,),
        in_specs=[pl.BlockSpec((N_HEADS, n_par * t_new, kv_lora), lambda s, pt: (0, s, 0)),
                  pl.BlockSpec((N_HEADS, n_par * t_new, QK_ROPE), lambda s, pt: (0, s, 0)),
                  pl.BlockSpec((n_par, t_new, kv_lora), lambda s, pt: (s, 0, 0)),
                  pl.BlockSpec((n_par, t_new, QK_ROPE), lambda s, pt: (s, 0, 0)),
                  pl.BlockSpec(wukt.shape, lambda s, pt: (0, 0)),
                  pl.BlockSpec(memory_space=pl.ANY),
                  pl.BlockSpec(memory_space=pl.ANY)],
        out_specs=pl.BlockSpec((n_par, n_rows, kv_lora), lambda s, pt: (s, 0, 0)),
        scratch_shapes=[pltpu.VMEM((SAMPLE_SLOTS, n_par * past, kv_lora), F32),
                        pltpu.VMEM((SAMPLE_SLOTS, QK_ROPE, n_par * past), F32),
                        pltpu.SemaphoreType.DMA((SAMPLE_SLOTS,)),
                        pltpu.SemaphoreType.DMA((SAMPLE_SLOTS,)),
                        pltpu.VMEM((n_par, N_HEADS * QK_NOPE + n_rows, kv_lora), BF16),
                        pltpu.VMEM((n_par, 2, max(_sample_blocks(past)), kv_lora), BF16)],
    )
    return pl.pallas_call(
        functools.partial(_sample_attn_kernel, layer),
        grid_spec=grid_spec,
        out_shape=jax.ShapeDtypeStruct((n_seq, n_rows, kv_lora), F32),
        compiler_params=pltpu.CompilerParams(dimension_semantics=("arbitrary",),
                                             vmem_limit_bytes=VMEM_LIMIT),
        name="sample_attn",
    )(page_table.reshape(-1), qlat, qr, c_new, kr_new, wukt, cache_ckv, cache_krt)


def _oproj_kernel(olat_ref, wuv_ref, o_ref):
    n_seq, n_rows, kv_lora = olat_ref.shape
    t_new = n_rows // N_HEADS
    acc = None
    for h in range(N_HEADS):
        x = olat_ref[:, h * t_new:(h + 1) * t_new, :].reshape(n_seq * t_new, kv_lora).astype(BF16)
        part = _dot(x, wuv_ref[h])
        acc = part if acc is None else acc + part
    o_ref[...] = acc.astype(o_ref.dtype)


def _oproj_call(olat, wuv):
    n_seq, n_rows, _ = olat.shape
    return pl.pallas_call(
        _oproj_kernel,
        out_shape=jax.ShapeDtypeStruct((n_seq * n_rows // N_HEADS, N_HEADS * V_DIM), BF16),
        compiler_params=pltpu.CompilerParams(vmem_limit_bytes=VMEM_LIMIT),
        name="o_proj",
    )(olat, wuv)


def _pad_heads(w, width):
    k = w.shape[0]
    w = w.reshape(k, N_HEADS, width)
    return jnp.pad(w, ((0, 0), (0, 0), (0, HEAD_PAD - width))).reshape(k, N_HEADS * HEAD_PAD)


def _rot_cols(w):
    half = w.shape[-1] // 2
    return jnp.concatenate([-w[..., half:], w[..., :half]], axis=-1)


def _rope_block(w):
    return jnp.pad(w, ((0, 0), (QK_NOPE, HEAD_PAD - QK_HEAD)))


def _head_gain(g):
    return jnp.pad(g, (0, HEAD_PAD - QK_HEAD)).reshape(1, HEAD_PAD)


def _rope_tables(n_pos, offset):
    pos = jnp.arange(n_pos, dtype=F32) + offset
    inv_freq = ROPE_THETA ** (-jnp.arange(0, QK_ROPE, 2, dtype=F32) / QK_ROPE)
    ang = pos[:, None] * inv_freq[None, :]
    cos, sin = jnp.cos(ang), jnp.sin(ang)
    ones = jnp.ones((n_pos, QK_NOPE), F32)
    tail = jnp.ones((n_pos, HEAD_PAD - QK_HEAD), F32)
    cos_tab = jnp.concatenate([ones, cos, cos, tail], axis=1)
    sin_tab = jnp.concatenate([0 * ones, sin, sin, 0 * tail], axis=1)
    return cos_tab, sin_tab


def _layer_weights(i, ffn1_norm, ffn1_w_in, ffn1_w_out, mix_norm, w_in, qa_norm, kva_norm, w_uq, w_ukv,
                   q_norm, k_norm, sgu_ln_g, sgu_ln_b, sgu_w_s, sgu_b_s, w_a_out, w_b_out, w_o,
                   ffn2_norm, ffn2_w_in, ffn2_w_out, ple_norm, ple_w_gate, ple_w_proj):
    a_width = sgu_ln_g.shape[1]
    q_lora = qa_norm.shape[1]
    kv_lora = kva_norm.shape[1]
    s1 = 2 * a_width
    s2 = s1 + q_lora
    s3 = s2 + kv_lora
    s4 = s3 + QK_ROPE
    wi = w_in[i].astype(BF16)
    w_kr = wi[:, s3:s4]
    wpre = jnp.concatenate([wi[:, :s3], _rope_block(w_kr), _rope_block(_rot_cols(w_kr))], axis=1)

    wq = w_uq[i].astype(BF16).reshape(q_lora, N_HEADS, QK_HEAD)
    wq_rot = jnp.concatenate([jnp.zeros_like(wq[..., :QK_NOPE]), _rot_cols(wq[..., QK_NOPE:])], axis=-1)
    wuq = jnp.stack([_pad_heads(wq.reshape(q_lora, -1), QK_HEAD).reshape(q_lora, N_HEADS, HEAD_PAD),
                     _pad_heads(wq_rot.reshape(q_lora, -1), QK_HEAD).reshape(q_lora, N_HEADS, HEAD_PAD)],
                    axis=2).reshape(q_lora, 2 * N_HEADS * HEAD_PAD)

    wkv = w_ukv[i].astype(BF16).reshape(kv_lora, N_HEADS, QK_NOPE + V_DIM)
    w_uk = wkv[..., :QK_NOPE]
    w_uv = wkv[..., QK_NOPE:]
    wukv = jnp.concatenate([_pad_heads(w_uk.reshape(kv_lora, -1), QK_NOPE),
                            w_uv.reshape(kv_lora, -1)], axis=1)
    wuk_t = jnp.transpose(w_uk, (1, 2, 0))
    head_cols = (jnp.arange(N_HEADS)[:, None] == jnp.arange(N_HEADS)[None, :])[:, None, :, None]
    wuv_blk = jnp.where(head_cols, jnp.transpose(w_uv, (1, 0, 2))[:, :, None, :], 0)
    wuv_blk = wuv_blk.reshape(N_HEADS, kv_lora, N_HEADS * V_DIM)

    row = lambda g: g[i].reshape(1, -1)
    return {
        'f1g': row(ffn1_norm), 'f1in': ffn1_w_in[i].astype(BF16), 'f1out': ffn1_w_out[i].astype(BF16),
        'mixg': row(mix_norm), 'wpre': wpre, 'wgates': wi[:, s4:],
        'qag': row(qa_norm), 'kvag': row(kva_norm), 'wuq': wuq, 'wukv': wukv,
        'qg': _head_gain(q_norm[i]), 'kg': _head_gain(k_norm[i]),
        'wuk_t': wuk_t, 'wuv_blk': wuv_blk,
        'lng': row(sgu_ln_g), 'lnb': row(sgu_ln_b), 'w_s': sgu_w_s[i], 'b_s': sgu_b_s[i],
        'wa': w_a_out[i].astype(BF16), 'wb': w_b_out[i].astype(BF16), 'wo': w_o[i].astype(BF16),
        'f2g': row(ffn2_norm), 'f2in': ffn2_w_in[i].astype(BF16), 'f2out': ffn2_w_out[i].astype(BF16),
        'pleg': row(ple_norm), 'plewg': ple_w_gate[i].astype(BF16), 'plewp': ple_w_proj[i].astype(BF16),
    }


def _mix_tables(w_s, b_s, period, group_dim):
    r = jnp.arange(CHUNK)
    same = (r[:, None] // period) == (r[None, :] // period)
    causal = (r[None, :] % period) <= (r[:, None] % period)
    reps = CHUNK // period
    mixw = jnp.where((same & causal)[None], jnp.tile(w_s[:, :period, :period], (1, reps, reps)), 0.0)
    bias = jnp.tile(b_s[:, :period], (1, reps))
    mixb = jnp.repeat(bias.T, group_dim, axis=1)
    return mixw.astype(BF16), mixb


def kernel(x_prompt, x_sample, cache_ckv, cache_krope, page_table, p_prompt, p_sample, ffn1_norm, ffn1_w_in, ffn1_w_out, mix_norm, w_in, qa_norm, kva_norm, w_uq, w_ukv, q_norm, k_norm, sgu_ln_g, sgu_ln_b, sgu_w_s, sgu_b_s, w_a_out, w_b_out, w_o, ffn2_norm, ffn2_w_in, ffn2_w_out, ple_norm, ple_w_gate, ple_w_proj):
    depth = w_in.shape[0]
    batch, seq, d_model = x_prompt.shape
    dec_batch, dec_seq, _ = x_sample.shape
    past_len = page_table.shape[1] * cache_ckv.shape[2]
    assert seq % TOKEN_TILE == 0 and TOKEN_TILE % dec_seq == 0 and CHUNK % dec_seq == 0
    assert cache_ckv.shape[2] == PAGE

    yp = x_prompt.reshape(batch * seq, d_model)
    ys = x_sample.reshape(dec_batch * dec_seq, d_model)
    cos_p, sin_p = _rope_tables(seq, 0)
    cos_s, sin_s = _rope_tables(dec_seq, past_len)
    reps = TOKEN_TILE // dec_seq
    cos_s, sin_s = jnp.tile(cos_s, (reps, 1)), jnp.tile(sin_s, (reps, 1))

    outs = [[] for _ in range(5)]
    for i in range(depth):
        w = _layer_weights(i, ffn1_norm, ffn1_w_in, ffn1_w_out, mix_norm, w_in, qa_norm, kva_norm, w_uq,
                           w_ukv, q_norm, k_norm, sgu_ln_g, sgu_ln_b, sgu_w_s, sgu_b_s, w_a_out, w_b_out,
                           w_o, ffn2_norm, ffn2_w_in, ffn2_w_out, ple_norm, ple_w_gate, ple_w_proj)

        group_dim = w['lng'].shape[1] // A_GROUPS
        w['mixw'], w['mixb'] = _mix_tables(w['w_s'], w['b_s'], CHUNK, group_dim)
        h1, ain, q, ckv_p, kr_p, k, v = _pre_call(yp, cos_p, sin_p, w, True, False, BF16)
        o = _prompt_attn(q, k, v, batch, seq)
        yp = _post_call(h1, ain, o, p_prompt[i].reshape(batch * seq, -1), w)

        w['mixw'], w['mixb'] = _mix_tables(w['w_s'], w['b_s'], dec_seq, group_dim)
        h1, ain, q, ckv_s, kr_s, vn_s = _pre_call(ys, cos_s, sin_s, w, False, True, F32)
        qlat, qr = _qabs_call(q, w['kg'], w['wuk_t'])
        kv_lora = ckv_s.shape[1]
        olat = _sample_attn(i, page_table, qlat, qr,
                            ckv_s.reshape(dec_batch, dec_seq, kv_lora),
                            kr_s.reshape(dec_batch, dec_seq, QK_ROPE),
                            w['wuk_t'].reshape(N_HEADS * QK_NOPE, kv_lora),
                            cache_ckv, jnp.swapaxes(cache_krope, 2, 3))
        o = _oproj_call(olat, w['wuv_blk'])
        ys = _post_call(h1, ain, o, p_sample[i].reshape(dec_batch * dec_seq, -1), w)

        outs[0].append(ckv_p.reshape(batch, seq, -1))
        outs[1].append(kr_p.reshape(batch, seq, -1))
        outs[2].append(ckv_s.reshape(dec_batch, dec_seq, -1))
        outs[3].append(kr_s.reshape(dec_batch, dec_seq, -1))
        outs[4].append(vn_s.reshape(dec_batch, dec_seq, -1))

    return (yp.reshape(batch, seq, d_model), ys.reshape(dec_batch, dec_seq, d_model),
            jnp.stack(outs[0]), jnp.stack(outs[1]), jnp.stack(outs[2]), jnp.stack(outs[3]),
            jnp.stack(outs[4]))
```

```python
import functools

import jax
import jax.numpy as jnp
from jax import lax
from jax.experimental import pallas as pl
from jax.experimental.pallas import tpu as pltpu

F32 = jnp.float32
BF16 = jnp.bfloat16

EPS = 1e-6
N_HEADS = 8
QK_NOPE = 64
QK_ROPE = 32
QK_HEAD = QK_NOPE + QK_ROPE
V_DIM = 64
HEAD_PAD = 128
A_GROUPS = 4
CHUNK = 128
ROPE_THETA = 10000.0
PAGE = 128
LOG2_E = 1.4426950408889634

FF_CHUNK = 256
TOKEN_TILE = 512
VMEM_LIMIT = 56 * 1024 * 1024

_NT = (((1,), (1,)), ((), ()))


def _dot(a, b):
    return jnp.dot(a, b, preferred_element_type=F32)


def _dot_nt(a, b):
    return lax.dot_general(a, b, _NT, preferred_element_type=F32)


def _rms(x, g):
    return x * lax.rsqrt(jnp.mean(x * x, axis=-1, keepdims=True) + EPS) * g


def _ffn_half(x, g_ref, win_ref, wout_ref, act_ref):
    d_ff = wout_ref.shape[0]
    xn = _rms(x, g_ref[...]).astype(BF16)
    for c in range(d_ff // FF_CHUNK):
        lo = c * FF_CHUNK
        a = _dot(xn, win_ref[:, lo:lo + FF_CHUNK])
        b = _dot(xn, win_ref[:, d_ff + lo:d_ff + lo + FF_CHUNK])
        act_ref[:, lo:lo + FF_CHUNK] = (a * jax.nn.sigmoid(a) * b).astype(BF16)
    return x + 0.5 * _dot(act_ref[...], wout_ref[...])


def _head_norm(xh, gain):
    ss = jnp.sum(xh * xh, axis=-1, keepdims=True)
    return xh * lax.rsqrt(ss * (1.0 / QK_HEAD) + EPS) * gain


def _pre_kernel(latent, *refs):
    (x_ref, cos_ref, sin_ref, f1g_ref, f1in_ref, f1out_ref, mixg_ref, wpre_ref, qag_ref, kvag_ref,
     wuq_ref, qg_ref, lng_ref, lnb_ref, mixw_ref, mixb_ref, wk_ref, kg_ref,
     h1_ref, ain_ref, ckv_ref, kr_ref) = refs[:22]
    if latent:
        qlat_ref, qr_ref, vn_ref, act_ref = refs[22:]
    else:
        q_ref, k_ref, v_ref, act_ref = refs[22:]

    a_width = lng_ref.shape[1]
    q_lora = qag_ref.shape[1]
    kv_lora = kvag_ref.shape[1]
    rows = x_ref.shape[0]

    h1 = _ffn_half(x_ref[...], f1g_ref, f1in_ref, f1out_ref, act_ref)
    h1_ref[...] = h1
    n = _rms(h1, mixg_ref[...]).astype(BF16)

    c0 = 2 * a_width
    c1 = c0 + q_lora
    c2 = c1 + kv_lora
    nq = N_HEADS * HEAD_PAD
    p_cq = _dot(n, wpre_ref[:, c0:c1])
    p_ckv = _dot(n, wpre_ref[:, c1:c2])
    p_kr = _dot(n, wpre_ref[:, c2:c2 + 2 * HEAD_PAD])
    p_uv = _dot(n, wpre_ref[:, :c0])

    uv = jax.nn.gelu(p_uv)
    u, v = uv[:, :a_width], uv[:, a_width:]
    vc = v - jnp.mean(v, axis=-1, keepdims=True)
    vn = vc * lax.rsqrt(jnp.mean(vc * vc, axis=-1, keepdims=True) + EPS) * lng_ref[...] + lnb_ref[...]
    if latent:
        vn_ref[...] = vn
    vnb = vn.astype(BF16)

    cos = cos_ref[...]
    sin = sin_ref[...]
    cq = _rms(p_cq, qag_ref[...]).astype(BF16)
    scale = QK_HEAD ** -0.5 * LOG2_E
    for h in range(N_HEADS):
        t = _dot(cq, wuq_ref[:, 2 * h * HEAD_PAD:2 * (h + 1) * HEAD_PAD])
        qh = t[:, :HEAD_PAD] * cos + t[:, HEAD_PAD:] * sin
        qh = _head_norm(qh, qg_ref[...]) * scale
        if latent:
            qa = qh * kg_ref[...]
            qlat_ref[h] = _dot(qa[:, :QK_NOPE].astype(BF16), wk_ref[h])
            qr_ref[h] = qa[:, QK_NOPE:QK_HEAD]
        else:
            q_ref[:, h * HEAD_PAD:(h + 1) * HEAD_PAD] = qh.astype(q_ref.dtype)
    c_kv = _rms(p_ckv, kvag_ref[...])
    ckv_ref[...] = c_kv
    kr_blk = p_kr[:, :HEAD_PAD] * cos + p_kr[:, HEAD_PAD:] * sin
    kr_ref[...] = kr_blk[:, QK_NOPE:QK_HEAD]
    if not latent:
        ckvb = c_kv.astype(BF16)
        for hp in range(N_HEADS // 2):
            t = _dot(ckvb, wk_ref[:, 2 * hp * HEAD_PAD:2 * (hp + 1) * HEAD_PAD])
            for hh in range(2):
                hs = slice((2 * hp + hh) * HEAD_PAD, (2 * hp + hh + 1) * HEAD_PAD)
                kh = t[:, hh * HEAD_PAD:(hh + 1) * HEAD_PAD] + kr_blk
                k_ref[:, hs] = _head_norm(kh, kg_ref[...]).astype(k_ref.dtype)
        v_ref[...] = _dot(ckvb, wk_ref[:, nq:]).astype(v_ref.dtype)

    gd = a_width // A_GROUPS
    for r in range(rows // CHUNK):
        rs = slice(r * CHUNK, (r + 1) * CHUNK)
        for g in range(A_GROUPS):
            gs = slice(g * gd, (g + 1) * gd)
            z = _dot(mixw_ref[g], vnb[rs, gs]) + mixb_ref[:, gs]
            ain_ref[rs, gs] = (u[rs, gs] * z).astype(ain_ref.dtype)


def _const_spec(shape):
    nd = len(shape)
    return pl.BlockSpec(shape, lambda i, _nd=nd: (0,) * _nd, pipeline_mode=pl.Buffered(1))


def _pre_call(x, cos_tab, sin_tab, w, latent):
    n_tok, d_model = x.shape
    tm = TOKEN_TILE
    n_tab = cos_tab.shape[0] // tm
    a_width = w['lng'].shape[1]
    kv_lora = w['kvag'].shape[1]
    d_ff = w['f1out'].shape[0]
    nq = N_HEADS * HEAD_PAD

    def row_spec(cols):
        return pl.BlockSpec((tm, cols), lambda i: (i, 0))

    tab_spec = pl.BlockSpec((tm, HEAD_PAD), lambda i: (i % n_tab, 0))
    consts = [w['f1g'], w['f1in'], w['f1out'], w['mixg'], w['wpre'], w['qag'], w['kvag'], w['wuq'],
              w['qg'], w['lng'], w['lnb'], w['mixw'], w['mixb'],
              w['wuk_t'] if latent else w['wukv'], w['kg']]
    in_specs = [row_spec(d_model), tab_spec, tab_spec] + [_const_spec(c.shape) for c in consts]
    out_shape = [jax.ShapeDtypeStruct((n_tok, d_model), F32),
                 jax.ShapeDtypeStruct((n_tok, a_width), BF16),
                 jax.ShapeDtypeStruct((n_tok, kv_lora), F32),
                 jax.ShapeDtypeStruct((n_tok, QK_ROPE), F32)]
    out_specs = [row_spec(d_model), row_spec(a_width), row_spec(kv_lora), row_spec(QK_ROPE)]
    if latent:
        out_shape += [jax.ShapeDtypeStruct((N_HEADS, n_tok, kv_lora), F32),
                      jax.ShapeDtypeStruct((N_HEADS, n_tok, QK_ROPE), F32),
                      jax.ShapeDtypeStruct((n_tok, a_width), F32)]
        out_specs += [pl.BlockSpec((N_HEADS, tm, kv_lora), lambda i: (0, i, 0)),
                      pl.BlockSpec((N_HEADS, tm, QK_ROPE), lambda i: (0, i, 0)),
                      row_spec(a_width)]
    else:
        out_shape += [jax.ShapeDtypeStruct((n_tok, nq), BF16),
                      jax.ShapeDtypeStruct((n_tok, nq), BF16),
                      jax.ShapeDtypeStruct((n_tok, N_HEADS * V_DIM), BF16)]
        out_specs += [row_spec(nq), row_spec(nq), row_spec(N_HEADS * V_DIM)]
    return pl.pallas_call(
        functools.partial(_pre_kernel, latent),
        grid=(n_tok // tm,),
        in_specs=in_specs,
        out_specs=out_specs,
        out_shape=out_shape,
        scratch_shapes=[pltpu.VMEM((tm, d_ff), BF16)],
        compiler_params=pltpu.CompilerParams(dimension_semantics=("arbitrary",),
                                             vmem_limit_bytes=VMEM_LIMIT),
        name="pre_latent" if latent else "pre_expand",
    )(x, cos_tab, sin_tab, *consts)


def _post_kernel(latent, h1_ref, ain_ref, o_ref, p_ref, mixg_ref, wg_ref, wa_ref, wb_ref, wo_ref,
                 f2g_ref, f2in_ref, f2out_ref, pleg_ref, plewg_ref, plewp_ref, *rest):
    if latent:
        wuv_ref, y_ref, act_ref = rest
    else:
        y_ref, act_ref = rest
    d_model = h1_ref.shape[1]
    h1 = h1_ref[...]
    n = _rms(h1, mixg_ref[...]).astype(BF16)
    ga = jax.nn.sigmoid(_dot(n, wg_ref[:, :d_model]))
    gb = jax.nn.sigmoid(_dot(n, wg_ref[:, d_model:]))
    if latent:
        n_seq, n_rows, kv_lora = o_ref.shape
        t_new = n_rows // N_HEADS
        o = None
        for h in range(N_HEADS):
            xh = o_ref[:, h * t_new:(h + 1) * t_new, :].reshape(n_seq * t_new, kv_lora).astype(BF16)
            part = _dot(xh, wuv_ref[h])
            o = part if o is None else o + part
        o = o.astype(BF16)
    else:
        o = o_ref[...]
    m = ga * _dot(ain_ref[...], wa_ref[...]) + gb * _dot(o, wb_ref[...])
    h2 = h1 + _dot(m.astype(BF16), wo_ref[...])
    h3 = _ffn_half(h2, f2g_ref, f2in_ref, f2out_ref, act_ref)
    hn = _rms(h3, pleg_ref[...]).astype(BF16)
    gate = jax.nn.sigmoid(_dot(hn, plewg_ref[...]))
    y_ref[...] = h3 + gate * _dot(p_ref[...].astype(BF16), plewp_ref[...])


def _post_call(h1, ain, o, p, w, latent):
    n_tok, d_model = h1.shape
    tm = TOKEN_TILE
    d_ff = w['f2out'].shape[0]

    def row_spec(cols):
        return pl.BlockSpec((tm, cols), lambda i: (i, 0))

    if latent:
        seqs_per_tile = tm * N_HEADS // o.shape[1]
        o_spec = pl.BlockSpec((seqs_per_tile,) + o.shape[1:], lambda i: (i, 0, 0))
    else:
        o_spec = row_spec(o.shape[1])
    consts = [w['mixg'], w['wgates'], w['wa'], w['wb'], w['wo'], w['f2g'], w['f2in'], w['f2out'],
              w['pleg'], w['plewg'], w['plewp']] + ([w['wuv_blk']] if latent else [])
    return pl.pallas_call(
        functools.partial(_post_kernel, latent),
        grid=(n_tok // tm,),
        in_specs=[row_spec(d_model), row_spec(ain.shape[1]), o_spec, row_spec(p.shape[1])]
        + [_const_spec(c.shape) for c in consts],
        out_specs=row_spec(d_model),
        out_shape=jax.ShapeDtypeStruct((n_tok, d_model), F32),
        scratch_shapes=[pltpu.VMEM((tm, d_ff), BF16)],
        compiler_params=pltpu.CompilerParams(dimension_semantics=("arbitrary",),
                                             vmem_limit_bytes=VMEM_LIMIT),
        name="post",
    )(h1, ain, o, p, *consts)


ATT_BQ = 512
ATT_BK = 256
ATT_VT_COLS = 512
ATT_ONES_ROWS = 16
ATT_HEADS = 2


def _prompt_attn_kernel(q_ref, k_ref, v_ref, o_ref, vt_ref, st_ref):
    bq = ATT_BQ
    seq = k_ref.shape[0]

    heads = range(ATT_HEADS)
    for r in range(seq // ATT_VT_COLS):
        rs = slice(r * ATT_VT_COLS, (r + 1) * ATT_VT_COLS)
        vt = v_ref[rs, :].astype(F32).T.astype(BF16)
        for hh in heads:
            vt_ref[hh, :V_DIM, rs] = vt[hh * V_DIM:(hh + 1) * V_DIM]
            vt_ref[hh, V_DIM:, rs] = jnp.ones((ATT_ONES_ROWS, ATT_VT_COLS), BF16)

    def query_block(i):
        rows = slice(i * bq, (i + 1) * bq)
        qts = [q_ref[rows, hh * HEAD_PAD:(hh + 1) * HEAD_PAD].astype(F32).T.astype(BF16) for hh in heads]

        def scores(j, slot, lo=0):
            start = j * ATT_BK
            for hh in heads:
                st_ref[slot, hh, :, lo:] = _dot(
                    k_ref[pl.ds(start, ATT_BK), hh * HEAD_PAD:(hh + 1) * HEAD_PAD], qts[hh][:, lo:])

        def update(j, slot, carry, masked, lo=0):
            start = j * ATT_BK
            out = []
            for hh in heads:
                m, acc = carry[hh]
                st = st_ref[slot, hh, :, lo:]
                if masked:
                    key = start + lax.broadcasted_iota(jnp.int32, st.shape, 0)
                    qry = i * bq + lo + lax.broadcasted_iota(jnp.int32, st.shape, 1)
                    st = jnp.where(key <= qry, st, -1e30)
                m_new = jnp.maximum(m[:, lo:], jnp.max(st, axis=0, keepdims=True))
                p = jnp.exp2(st - m_new)
                acc_new = (jnp.exp2(m[:, lo:] - m_new) * acc[:, lo:]
                           + _dot(vt_ref[hh, :, pl.ds(start, ATT_BK)], p.astype(BF16)))
                if lo:
                    m_new = jnp.concatenate([m[:, :lo], m_new], axis=1)
                    acc_new = jnp.concatenate([acc[:, :lo], acc_new], axis=1)
                out.append((m_new, acc_new))
            return tuple(out)

        one = (jnp.full((1, bq), -1e30, F32), jnp.zeros((V_DIM + ATT_ONES_ROWS, bq), F32))

        carry = (one,) * ATT_HEADS
        scores(0, 0)
        for j in range(2 * i):
            scores(j + 1, (j + 1) % 2)
            carry = update(j, j % 2, carry, False)
        scores(2 * i + 1, 1, lo=ATT_BK)
        carry = update(2 * i, 0, carry, True)
        carry = update(2 * i + 1, 1, carry, True, lo=ATT_BK)
        ot = jnp.concatenate([acc[:V_DIM] / acc[V_DIM:V_DIM + 1] for (_, acc) in carry], axis=0)
        o_ref[rows, :] = ot.T.astype(o_ref.dtype)

    for i in range(seq // bq):
        query_block(i)


def _prompt_attn(q, k, v, batch, seq):
    assert ATT_BQ == 2 * ATT_BK and seq % (2 * ATT_BQ) == 0 and seq % ATT_VT_COLS == 0
    n_tok = q.shape[0]
    return pl.pallas_call(
        _prompt_attn_kernel,
        grid=(batch, N_HEADS // ATT_HEADS),
        in_specs=[pl.BlockSpec((seq, ATT_HEADS * HEAD_PAD), lambda b, hp: (b, hp)),
                  pl.BlockSpec((seq, ATT_HEADS * HEAD_PAD), lambda b, hp: (b, hp)),
                  pl.BlockSpec((seq, ATT_HEADS * V_DIM), lambda b, hp: (b, hp))],
        out_specs=pl.BlockSpec((seq, ATT_HEADS * V_DIM), lambda b, hp: (b, hp)),
        out_shape=jax.ShapeDtypeStruct((n_tok, N_HEADS * V_DIM), BF16),
        scratch_shapes=[pltpu.VMEM((ATT_HEADS, V_DIM + ATT_ONES_ROWS, seq), BF16),
                        pltpu.VMEM((2, ATT_HEADS, ATT_BK, ATT_BQ), F32)],
        compiler_params=pltpu.CompilerParams(dimension_semantics=("arbitrary",) * 2,
                                             vmem_limit_bytes=VMEM_LIMIT),
        name="prompt_attn",
    )(q, k, v)


SAMPLE_SUB = 2048
SAMPLE_SEQS = 1
SAMPLE_SLOTS = 3


def _sample_blocks(past):
    return [min(SAMPLE_SUB, past - lo) for lo in range(0, past, SAMPLE_SUB)]


def _sample_attn_kernel(layer, pt_ref, qlat_ref, qr_ref, cnew_ref, krnew_ref, wukt_ref, ckv_hbm, krt_hbm,
                        olat_ref, cbuf, kbuf, csem, ksem, lhs_ref, cb_ref):
    s = pl.program_id(0)
    n_steps = pl.num_programs(0)
    n_par = cnew_ref.shape[0]
    past = cbuf.shape[1] // n_par
    n_pages = past // PAGE
    t_new = cnew_ref.shape[1]
    kv_lora = cnew_ref.shape[2]
    n_rows = N_HEADS * t_new
    n_kn = N_HEADS * QK_NOPE
    n_slots = cbuf.shape[0]
    ahead = n_slots - 1
    slot = s % n_slots

    def copies(step, slot_):
        out = []
        for a in range(n_par):
            for p in range(n_pages):
                page = pt_ref[(step * n_par + a) * n_pages + p]
                keys = pl.ds(a * past + p * PAGE, PAGE)
                out.append(pltpu.make_async_copy(ckv_hbm.at[layer, page], cbuf.at[slot_, keys, :],
                                                 csem.at[slot_]))
                out.append(pltpu.make_async_copy(krt_hbm.at[layer, page], kbuf.at[slot_, :, keys],
                                                 ksem.at[slot_]))
        return out

    def start_all(cps):
        for n, cp in enumerate(cps):
            cp.start(priority=(n // 2) % 2)

    @pl.when(s == 0)
    def _():
        for a in range(n_par):
            lhs_ref[a, :n_kn, :] = wukt_ref[...]
        for d in range(ahead):
            start_all(copies(jnp.minimum(d, n_steps - 1), d))

    for cp in copies(s, slot):
        cp.wait()

    qrs = []
    for a in range(n_par):
        tok = slice(a * t_new, (a + 1) * t_new)
        lhs_ref[a, n_kn:, :] = qlat_ref[:, tok, :].reshape(n_rows, kv_lora).astype(BF16)
        qrs.append(qr_ref[:, tok, :].reshape(n_rows, QK_ROPE).astype(BF16))

    def scaled_scores(big, rope_scores, r2):
        n = big.shape[1]
        kn = big[:n_kn]
        ss = jnp.sum((kn * kn).reshape(N_HEADS, QK_NOPE, n), axis=1)
        rs = lax.rsqrt((ss + r2) * (1.0 / QK_HEAD) + EPS)
        sc = (big[n_kn:] + rope_scores).reshape(N_HEADS, t_new, n) * rs[:, None, :]
        return sc.reshape(n_rows, n)

    blocks = _sample_blocks(past)
    starts = [sum(blocks[:k]) for k in range(len(blocks))]

    def past_scores(a, k):
        keys = slice(a * past + starts[k], a * past + starts[k] + blocks[k])
        cb = cbuf[slot, keys, :].astype(BF16)
        cb_ref[a, k % 2, :blocks[k], :] = cb
        krt = kbuf[slot, :, keys]
        r2 = jnp.sum(krt * krt, axis=0, keepdims=True)
        return scaled_scores(_dot_nt(lhs_ref[a], cb), _dot(qrs[a], krt.astype(BF16)), r2)

    def new_scores(a):
        pad = PAGE - t_new
        cb_new = jnp.concatenate([cnew_ref[a], jnp.zeros((pad, kv_lora), F32)], axis=0).astype(BF16)
        kr_new = jnp.concatenate([krnew_ref[a], jnp.zeros((pad, QK_ROPE), F32)], axis=0)
        kr2 = kr_new * kr_new
        kr2_hi = kr2.astype(BF16)
        kr2_lo = (kr2 - kr2_hi.astype(F32)).astype(BF16)
        ones = jnp.ones((8, QK_ROPE), BF16)
        r2_new = (_dot_nt(ones, kr2_hi) + _dot_nt(ones, kr2_lo))[:1]
        sc = scaled_scores(_dot_nt(lhs_ref[a], cb_new), _dot_nt(qrs[a], kr_new.astype(BF16)), r2_new)
        row = lax.broadcasted_iota(jnp.int32, (n_rows, PAGE), 0)
        col = lax.broadcasted_iota(jnp.int32, (n_rows, PAGE), 1)
        return jnp.where(col <= row % t_new, sc, -1e30), cb_new

    n_sub = len(blocks)
    new = [new_scores(a) for a in range(n_par)]
    sc_next = [past_scores(a, 0) for a in range(n_par)]

    nxt = jnp.minimum(s + ahead, n_steps - 1)
    nxt_slot = (s + ahead) % n_slots
    start_all(copies(nxt, nxt_slot))

    state = []
    for a in range(n_par):
        sc_new, cb_new = new[a]
        m = jnp.max(sc_new, axis=-1, keepdims=True)
        p = jnp.exp2(sc_new - m)
        state.append((m, jnp.sum(p, axis=-1, keepdims=True), _dot(p.astype(BF16), cb_new)))
    for k in range(n_sub):
        sc_cur = sc_next
        if k + 1 < n_sub:
            sc_next = [past_scores(a, k + 1) for a in range(n_par)]
        for a in range(n_par):
            m, l, acc = state[a]
            m_new = jnp.maximum(m, jnp.max(sc_cur[a], axis=-1, keepdims=True))
            alpha = jnp.exp2(m - m_new)
            p = jnp.exp2(sc_cur[a] - m_new)
            l = alpha * l + jnp.sum(p, axis=-1, keepdims=True)
            acc = alpha * acc + _dot(p.astype(BF16), cb_ref[a, k % 2, :blocks[k], :])
            state[a] = (m_new, l, acc)
    for a in range(n_par):
        _, l, acc = state[a]
        olat_ref[a] = acc / l

    @pl.when(s == n_steps - 1)
    def _():
        for d in range(1, n_slots):
            for cp in copies(nxt, (s + d) % n_slots):
                cp.wait()


def _sample_attn(layer, page_table, qlat, qr, c_new, kr_new, wukt, cache_ckv, cache_krt):
    n_seq, n_pages = page_table.shape
    t_new = c_new.shape[1]
    kv_lora = c_new.shape[2]
    n_rows = N_HEADS * t_new
    past = n_pages * PAGE
    n_par = SAMPLE_SEQS
    assert n_seq % n_par == 0
    grid_spec = pltpu.PrefetchScalarGridSpec(
        num_scalar_prefetch=1,
        grid=(n_seq // n_par,),
        in_specs=[pl.BlockSpec((N_HEADS, n_par * t_new, kv_lora), lambda s, pt: (0, s, 0)),
                  pl.BlockSpec((N_HEADS, n_par * t_new, QK_ROPE), lambda s, pt: (0, s, 0)),
                  pl.BlockSpec((n_par, t_new, kv_lora), lambda s, pt: (s, 0, 0)),
                  pl.BlockSpec((n_par, t_new, QK_ROPE), lambda s, pt: (s, 0, 0)),
                  pl.BlockSpec(wukt.shape, lambda s, pt: (0, 0)),
                  pl.BlockSpec(memory_space=pl.ANY),
                  pl.BlockSpec(memory_space=pl.ANY)],
        out_specs=pl.BlockSpec((n_par, n_rows, kv_lora), lambda s, pt: (s, 0, 0)),
        scratch_shapes=[pltpu.VMEM((SAMPLE_SLOTS, n_par * past, kv_lora), F32),
                        pltpu.VMEM((SAMPLE_SLOTS, QK_ROPE, n_par * past), F32),
                        pltpu.SemaphoreType.DMA((SAMPLE_SLOTS,)),
                        pltpu.SemaphoreType.DMA((SAMPLE_SLOTS,)),
                        pltpu.VMEM((n_par, N_HEADS * QK_NOPE + n_rows, kv_lora), BF16),
                        pltpu.VMEM((n_par, 2, max(_sample_blocks(past)), kv_lora), BF16)],
    )
    return pl.pallas_call(
        functools.partial(_sample_attn_kernel, layer),
        grid_spec=grid_spec,
        out_shape=jax.ShapeDtypeStruct((n_seq, n_rows, kv_lora), F32),
        compiler_params=pltpu.CompilerParams(dimension_semantics=("arbitrary",),
                                             vmem_limit_bytes=VMEM_LIMIT),
        name="sample_attn",
    )(page_table.reshape(-1), qlat, qr, c_new, kr_new, wukt, cache_ckv, cache_krt)


def _pad_heads(w, width):
    k = w.shape[0]
    w = w.reshape(k, N_HEADS, width)
    return jnp.pad(w, ((0, 0), (0, 0), (0, HEAD_PAD - width))).reshape(k, N_HEADS * HEAD_PAD)


def _rot_cols(w):
    half = w.shape[-1] // 2
    return jnp.concatenate([-w[..., half:], w[..., :half]], axis=-1)


def _rope_block(w):
    return jnp.pad(w, ((0, 0), (QK_NOPE, HEAD_PAD - QK_HEAD)))


def _head_gain(g):
    return jnp.pad(g, (0, HEAD_PAD - QK_HEAD)).reshape(1, HEAD_PAD)


def _rope_tables(n_pos, offset):
    pos = jnp.arange(n_pos, dtype=F32) + offset
    inv_freq = ROPE_THETA ** (-jnp.arange(0, QK_ROPE, 2, dtype=F32) / QK_ROPE)
    ang = pos[:, None] * inv_freq[None, :]
    cos, sin = jnp.cos(ang), jnp.sin(ang)
    ones = jnp.ones((n_pos, QK_NOPE), F32)
    tail = jnp.ones((n_pos, HEAD_PAD - QK_HEAD), F32)
    cos_tab = jnp.concatenate([ones, cos, cos, tail], axis=1)
    sin_tab = jnp.concatenate([0 * ones, sin, sin, 0 * tail], axis=1)
    return cos_tab, sin_tab


def _layer_weights(i, ffn1_norm, ffn1_w_in, ffn1_w_out, mix_norm, w_in, qa_norm, kva_norm, w_uq, w_ukv,
                   q_norm, k_norm, sgu_ln_g, sgu_ln_b, sgu_w_s, sgu_b_s, w_a_out, w_b_out, w_o,
                   ffn2_norm, ffn2_w_in, ffn2_w_out, ple_norm, ple_w_gate, ple_w_proj):
    a_width = sgu_ln_g.shape[1]
    q_lora = qa_norm.shape[1]
    kv_lora = kva_norm.shape[1]
    s1 = 2 * a_width
    s2 = s1 + q_lora
    s3 = s2 + kv_lora
    s4 = s3 + QK_ROPE
    wi = w_in[i].astype(BF16)
    w_kr = wi[:, s3:s4]
    wpre = jnp.concatenate([wi[:, :s3], _rope_block(w_kr), _rope_block(_rot_cols(w_kr))], axis=1)

    wq = w_uq[i].astype(BF16).reshape(q_lora, N_HEADS, QK_HEAD)
    wq_rot = jnp.concatenate([jnp.zeros_like(wq[..., :QK_NOPE]), _rot_cols(wq[..., QK_NOPE:])], axis=-1)
    wuq = jnp.stack([_pad_heads(wq.reshape(q_lora, -1), QK_HEAD).reshape(q_lora, N_HEADS, HEAD_PAD),
                     _pad_heads(wq_rot.reshape(q_lora, -1), QK_HEAD).reshape(q_lora, N_HEADS, HEAD_PAD)],
                    axis=2).reshape(q_lora, 2 * N_HEADS * HEAD_PAD)

    wkv = w_ukv[i].astype(BF16).reshape(kv_lora, N_HEADS, QK_NOPE + V_DIM)
    w_uk = wkv[..., :QK_NOPE]
    w_uv = wkv[..., QK_NOPE:]
    wukv = jnp.concatenate([_pad_heads(w_uk.reshape(kv_lora, -1), QK_NOPE),
                            w_uv.reshape(kv_lora, -1)], axis=1)
    wuk_t = jnp.transpose(w_uk, (1, 2, 0))
    head_cols = (jnp.arange(N_HEADS)[:, None] == jnp.arange(N_HEADS)[None, :])[:, None, :, None]
    wuv_blk = jnp.where(head_cols, jnp.transpose(w_uv, (1, 0, 2))[:, :, None, :], 0)
    wuv_blk = wuv_blk.reshape(N_HEADS, kv_lora, N_HEADS * V_DIM)

    row = lambda g: g[i].reshape(1, -1)
    return {
        'f1g': row(ffn1_norm), 'f1in': ffn1_w_in[i].astype(BF16), 'f1out': ffn1_w_out[i].astype(BF16),
        'mixg': row(mix_norm), 'wpre': wpre, 'wgates': wi[:, s4:],
        'qag': row(qa_norm), 'kvag': row(kva_norm), 'wuq': wuq, 'wukv': wukv,
        'qg': _head_gain(q_norm[i]), 'kg': _head_gain(k_norm[i]),
        'wuk_t': wuk_t, 'wuv_blk': wuv_blk,
        'lng': row(sgu_ln_g), 'lnb': row(sgu_ln_b), 'w_s': sgu_w_s[i], 'b_s': sgu_b_s[i],
        'wa': w_a_out[i].astype(BF16), 'wb': w_b_out[i].astype(BF16), 'wo': w_o[i].astype(BF16),
        'f2g': row(ffn2_norm), 'f2in': ffn2_w_in[i].astype(BF16), 'f2out': ffn2_w_out[i].astype(BF16),
        'pleg': row(ple_norm), 'plewg': ple_w_gate[i].astype(BF16), 'plewp': ple_w_proj[i].astype(BF16),
    }


def _mix_tables(w_s, b_s, period, group_dim):
    r = jnp.arange(CHUNK)
    same = (r[:, None] // period) == (r[None, :] // period)
    causal = (r[None, :] % period) <= (r[:, None] % period)
    reps = CHUNK // period
    mixw = jnp.where((same & causal)[None], jnp.tile(w_s[:, :period, :period], (1, reps, reps)), 0.0)
    bias = jnp.tile(b_s[:, :period], (1, reps))
    mixb = jnp.repeat(bias.T, group_dim, axis=1)
    return mixw.astype(BF16), mixb


def kernel(x_prompt, x_sample, cache_ckv, cache_krope, page_table, p_prompt, p_sample, ffn1_norm, ffn1_w_in, ffn1_w_out, mix_norm, w_in, qa_norm, kva_norm, w_uq, w_ukv, q_norm, k_norm, sgu_ln_g, sgu_ln_b, sgu_w_s, sgu_b_s, w_a_out, w_b_out, w_o, ffn2_norm, ffn2_w_in, ffn2_w_out, ple_norm, ple_w_gate, ple_w_proj):
    depth = w_in.shape[0]
    batch, seq, d_model = x_prompt.shape
    dec_batch, dec_seq, _ = x_sample.shape
    past_len = page_table.shape[1] * cache_ckv.shape[2]
    assert seq % TOKEN_TILE == 0 and TOKEN_TILE % dec_seq == 0 and CHUNK % dec_seq == 0
    assert cache_ckv.shape[2] == PAGE

    yp = x_prompt.reshape(batch * seq, d_model)
    ys = x_sample.reshape(dec_batch * dec_seq, d_model)
    cos_p, sin_p = _rope_tables(seq, 0)
    cos_s, sin_s = _rope_tables(dec_seq, past_len)
    reps = TOKEN_TILE // dec_seq
    cos_s, sin_s = jnp.tile(cos_s, (reps, 1)), jnp.tile(sin_s, (reps, 1))

    outs = [[] for _ in range(5)]
    for i in range(depth):
        w = _layer_weights(i, ffn1_norm, ffn1_w_in, ffn1_w_out, mix_norm, w_in, qa_norm, kva_norm, w_uq,
                           w_ukv, q_norm, k_norm, sgu_ln_g, sgu_ln_b, sgu_w_s, sgu_b_s, w_a_out, w_b_out,
                           w_o, ffn2_norm, ffn2_w_in, ffn2_w_out, ple_norm, ple_w_gate, ple_w_proj)

        group_dim = w['lng'].shape[1] // A_GROUPS
        w['mixw'], w['mixb'] = _mix_tables(w['w_s'], w['b_s'], CHUNK, group_dim)
        h1, ain, ckv_p, kr_p, q, k, v = _pre_call(yp, cos_p, sin_p, w, False)
        o = _prompt_attn(q, k, v, batch, seq)
        yp = _post_call(h1, ain, o, p_prompt[i].reshape(batch * seq, -1), w, False)

        w['mixw'], w['mixb'] = _mix_tables(w['w_s'], w['b_s'], dec_seq, group_dim)
        h1, ain, ckv_s, kr_s, qlat, qr, vn_s = _pre_call(ys, cos_s, sin_s, w, True)
        kv_lora = ckv_s.shape[1]
        olat = _sample_attn(i, page_table, qlat, qr,
                            ckv_s.reshape(dec_batch, dec_seq, kv_lora),
                            kr_s.reshape(dec_batch, dec_seq, QK_ROPE),
                            w['wuk_t'].reshape(N_HEADS * QK_NOPE, kv_lora),
                            cache_ckv, jnp.swapaxes(cache_krope, 2, 3))
        ys = _post_call(h1, ain, olat, p_sample[i].reshape(dec_batch * dec_seq, -1), w, True)

        outs[0].append(ckv_p.reshape(batch, seq, -1))
        outs[1].append(kr_p.reshape(batch, seq, -1))
        outs[2].append(ckv_s.reshape(dec_batch, dec_seq, -1))
        outs[3].append(kr_s.reshape(dec_batch, dec_seq, -1))
        outs[4].append(vn_s.reshape(dec_batch, dec_seq, -1))

    return (yp.reshape(batch, seq, d_model), ys.reshape(dec_batch, dec_seq, d_model),
            jnp.stack(outs[0]), jnp.stack(outs[1]), jnp.stack(outs[2]), jnp.stack(outs[3]),
            jnp.stack(outs[4]))
```

```python
import functools

import jax
import jax.numpy as jnp
from jax import lax
from jax.experimental import pallas as pl
from jax.experimental.pallas import tpu as pltpu

F32 = jnp.float32
BF16 = jnp.bfloat16

EPS = 1e-6
N_HEADS = 8
QK_NOPE = 64
QK_ROPE = 32
QK_HEAD = QK_NOPE + QK_ROPE
V_DIM = 64
HEAD_PAD = 128
A_GROUPS = 4
CHUNK = 128
ROPE_THETA = 10000.0
PAGE = 128
LOG2_E = 1.4426950408889634

FF_CHUNK = 256
TOKEN_TILE = 512
VMEM_LIMIT = 56 * 1024 * 1024

_NT = (((1,), (1,)), ((), ()))


def _dot(a, b):
    return jnp.dot(a, b, preferred_element_type=F32)


def _dot_nt(a, b):
    return lax.dot_general(a, b, _NT, preferred_element_type=F32)


def _rms(x, g):
    return x * lax.rsqrt(jnp.mean(x * x, axis=-1, keepdims=True) + EPS) * g


def _ffn_half(x, g_ref, win_ref, wout_ref, act_ref):
    d_ff = wout_ref.shape[0]
    xn = _rms(x, g_ref[...]).astype(BF16)
    for c in range(d_ff // FF_CHUNK):
        lo = c * FF_CHUNK
        a = _dot(xn, win_ref[:, lo:lo + FF_CHUNK])
        b = _dot(xn, win_ref[:, d_ff + lo:d_ff + lo + FF_CHUNK])
        act_ref[:, lo:lo + FF_CHUNK] = (a * jax.nn.sigmoid(a) * b).astype(BF16)
    return x + 0.5 * _dot(act_ref[...], wout_ref[...])


def _head_norm(xh, gain):
    ss = jnp.sum(xh * xh, axis=-1, keepdims=True)
    return xh * lax.rsqrt(ss * (1.0 / QK_HEAD) + EPS) * gain


def _pre_kernel(latent, *refs):
    (x_ref, cos_ref, sin_ref, f1g_ref, f1in_ref, f1out_ref, mixg_ref, wpre_ref, qag_ref, kvag_ref,
     wuq_ref, qg_ref, lng_ref, lnb_ref, mixw_ref, mixb_ref, wk_ref, kg_ref,
     h1_ref, ain_ref, ckv_ref, kr_ref) = refs[:22]
    if latent:
        qlat_ref, qr_ref, vn_ref, act_ref = refs[22:]
    else:
        q_ref, k_ref, v_ref, act_ref = refs[22:]

    a_width = lng_ref.shape[1]
    q_lora = qag_ref.shape[1]
    kv_lora = kvag_ref.shape[1]
    rows = x_ref.shape[0]

    h1 = _ffn_half(x_ref[...], f1g_ref, f1in_ref, f1out_ref, act_ref)
    h1_ref[...] = h1
    n = _rms(h1, mixg_ref[...]).astype(BF16)

    c0 = 2 * a_width
    c1 = c0 + q_lora
    c2 = c1 + kv_lora
    nq = N_HEADS * HEAD_PAD
    p_cq = _dot(n, wpre_ref[:, c0:c1])
    p_ckv = _dot(n, wpre_ref[:, c1:c2])
    p_kr = _dot(n, wpre_ref[:, c2:c2 + 2 * HEAD_PAD])
    p_uv = _dot(n, wpre_ref[:, :c0])

    uv = jax.nn.gelu(p_uv)
    u, v = uv[:, :a_width], uv[:, a_width:]
    vc = v - jnp.mean(v, axis=-1, keepdims=True)
    vn = vc * lax.rsqrt(jnp.mean(vc * vc, axis=-1, keepdims=True) + EPS) * lng_ref[...] + lnb_ref[...]
    if latent:
        vn_ref[...] = vn
    vnb = vn.astype(BF16)

    cos = cos_ref[...]
    sin = sin_ref[...]
    cq = _rms(p_cq, qag_ref[...]).astype(BF16)
    scale = QK_HEAD ** -0.5 * LOG2_E
    for h in range(N_HEADS):
        t = _dot(cq, wuq_ref[:, 2 * h * HEAD_PAD:2 * (h + 1) * HEAD_PAD])
        qh = t[:, :HEAD_PAD] * cos + t[:, HEAD_PAD:] * sin
        qh = _head_norm(qh, qg_ref[...]) * scale
        if latent:
            qa = qh * kg_ref[...]
            qlat_ref[h] = _dot(qa[:, :QK_NOPE].astype(BF16), wk_ref[h])
            qr_ref[h] = qa[:, QK_NOPE:QK_HEAD]
        else:
            q_ref[:, h * HEAD_PAD:(h + 1) * HEAD_PAD] = qh.astype(q_ref.dtype)
    c_kv = _rms(p_ckv, kvag_ref[...])
    ckv_ref[...] = c_kv
    kr_blk = p_kr[:, :HEAD_PAD] * cos + p_kr[:, HEAD_PAD:] * sin
    kr_ref[...] = kr_blk[:, QK_NOPE:QK_HEAD]
    if not latent:
        ckvb = c_kv.astype(BF16)
        for hp in range(N_HEADS // 2):
            t = _dot(ckvb, wk_ref[:, 2 * hp * HEAD_PAD:2 * (hp + 1) * HEAD_PAD])
            for hh in range(2):
                hs = slice((2 * hp + hh) * HEAD_PAD, (2 * hp + hh + 1) * HEAD_PAD)
                kh = t[:, hh * HEAD_PAD:(hh + 1) * HEAD_PAD] + kr_blk
                k_ref[:, hs] = _head_norm(kh, kg_ref[...]).astype(k_ref.dtype)
        v_ref[...] = _dot(ckvb, wk_ref[:, nq:]).astype(v_ref.dtype)

    gd = a_width // A_GROUPS
    for r in range(rows // CHUNK):
        rs = slice(r * CHUNK, (r + 1) * CHUNK)
        for g in range(A_GROUPS):
            gs = slice(g * gd, (g + 1) * gd)
            z = _dot(mixw_ref[g], vnb[rs, gs]) + mixb_ref[:, gs]
            ain_ref[rs, gs] = (u[rs, gs] * z).astype(ain_ref.dtype)


def _const_spec(shape):
    nd = len(shape)
    return pl.BlockSpec(shape, lambda i, _nd=nd: (0,) * _nd, pipeline_mode=pl.Buffered(1))


def _pre_call(x, cos_tab, sin_tab, w, latent):
    n_tok, d_model = x.shape
    tm = TOKEN_TILE
    n_tab = cos_tab.shape[0] // tm
    a_width = w['lng'].shape[1]
    kv_lora = w['kvag'].shape[1]
    d_ff = w['f1out'].shape[0]
    nq = N_HEADS * HEAD_PAD

    def row_spec(cols):
        return pl.BlockSpec((tm, cols), lambda i: (i, 0))

    tab_spec = pl.BlockSpec((tm, HEAD_PAD), lambda i: (i % n_tab, 0))
    consts = [w['f1g'], w['f1in'], w['f1out'], w['mixg'], w['wpre'], w['qag'], w['kvag'], w['wuq'],
              w['qg'], w['lng'], w['lnb'], w['mixw'], w['mixb'],
              w['wuk_t'] if latent else w['wukv'], w['kg']]
    in_specs = [row_spec(d_model), tab_spec, tab_spec] + [_const_spec(c.shape) for c in consts]
    out_shape = [jax.ShapeDtypeStruct((n_tok, d_model), F32),
                 jax.ShapeDtypeStruct((n_tok, a_width), BF16),
                 jax.ShapeDtypeStruct((n_tok, kv_lora), F32),
                 jax.ShapeDtypeStruct((n_tok, QK_ROPE), F32)]
    out_specs = [row_spec(d_model), row_spec(a_width), row_spec(kv_lora), row_spec(QK_ROPE)]
    if latent:
        out_shape += [jax.ShapeDtypeStruct((N_HEADS, n_tok, kv_lora), F32),
                      jax.ShapeDtypeStruct((N_HEADS, n_tok, QK_ROPE), F32),
                      jax.ShapeDtypeStruct((n_tok, a_width), F32)]
        out_specs += [pl.BlockSpec((N_HEADS, tm, kv_lora), lambda i: (0, i, 0)),
                      pl.BlockSpec((N_HEADS, tm, QK_ROPE), lambda i: (0, i, 0)),
                      row_spec(a_width)]
    else:
        out_shape += [jax.ShapeDtypeStruct((n_tok, nq), BF16),
                      jax.ShapeDtypeStruct((n_tok, nq), BF16),
                      jax.ShapeDtypeStruct((n_tok, N_HEADS * V_DIM), BF16)]
        out_specs += [row_spec(nq), row_spec(nq), row_spec(N_HEADS * V_DIM)]
    return pl.pallas_call(
        functools.partial(_pre_kernel, latent),
        grid=(n_tok // tm,),
        in_specs=in_specs,
        out_specs=out_specs,
        out_shape=out_shape,
        scratch_shapes=[pltpu.VMEM((tm, d_ff), BF16)],
        compiler_params=pltpu.CompilerParams(dimension_semantics=("arbitrary",),
                                             vmem_limit_bytes=VMEM_LIMIT),
        name="pre_latent" if latent else "pre_expand",
    )(x, cos_tab, sin_tab, *consts)


def _post_kernel(latent, h1_ref, ain_ref, o_ref, p_ref, mixg_ref, wg_ref, wa_ref, wb_ref, wo_ref,
                 f2g_ref, f2in_ref, f2out_ref, pleg_ref, plewg_ref, plewp_ref, *rest):
    if latent:
        wuv_ref, y_ref, act_ref = rest
    else:
        y_ref, act_ref = rest
    d_model = h1_ref.shape[1]
    h1 = h1_ref[...]
    n = _rms(h1, mixg_ref[...]).astype(BF16)
    ga = jax.nn.sigmoid(_dot(n, wg_ref[:, :d_model]))
    gb = jax.nn.sigmoid(_dot(n, wg_ref[:, d_model:]))
    if latent:
        n_seq, n_rows, kv_lora = o_ref.shape
        t_new = n_rows // N_HEADS
        o = None
        for h in range(N_HEADS):
            xh = o_ref[:, h * t_new:(h + 1) * t_new, :].reshape(n_seq * t_new, kv_lora).astype(BF16)
            part = _dot(xh, wuv_ref[h])
            o = part if o is None else o + part
        o = o.astype(BF16)
    else:
        o = o_ref[...]
    m = ga * _dot(ain_ref[...], wa_ref[...]) + gb * _dot(o, wb_ref[...])
    h2 = h1 + _dot(m.astype(BF16), wo_ref[...])
    h3 = _ffn_half(h2, f2g_ref, f2in_ref, f2out_ref, act_ref)
    hn = _rms(h3, pleg_ref[...]).astype(BF16)
    gate = jax.nn.sigmoid(_dot(hn, plewg_ref[...]))
    y_ref[...] = h3 + gate * _dot(p_ref[...].astype(BF16), plewp_ref[...])


def _post_call(h1, ain, o, p, w, latent):
    n_tok, d_model = h1.shape
    tm = TOKEN_TILE
    d_ff = w['f2out'].shape[0]

    def row_spec(cols):
        return pl.BlockSpec((tm, cols), lambda i: (i, 0))

    if latent:
        seqs_per_tile = tm * N_HEADS // o.shape[1]
        o_spec = pl.BlockSpec((seqs_per_tile,) + o.shape[1:], lambda i: (i, 0, 0))
    else:
        o_spec = row_spec(o.shape[1])
    consts = [w['mixg'], w['wgates'], w['wa'], w['wb'], w['wo'], w['f2g'], w['f2in'], w['f2out'],
              w['pleg'], w['plewg'], w['plewp']] + ([w['wuv_blk']] if latent else [])
    return pl.pallas_call(
        functools.partial(_post_kernel, latent),
        grid=(n_tok // tm,),
        in_specs=[row_spec(d_model), row_spec(ain.shape[1]), o_spec, row_spec(p.shape[1])]
        + [_const_spec(c.shape) for c in consts],
        out_specs=row_spec(d_model),
        out_shape=jax.ShapeDtypeStruct((n_tok, d_model), F32),
        scratch_shapes=[pltpu.VMEM((tm, d_ff), BF16)],
        compiler_params=pltpu.CompilerParams(dimension_semantics=("arbitrary",),
                                             vmem_limit_bytes=VMEM_LIMIT),
        name="post",
    )(h1, ain, o, p, *consts)


ATT_BQ = 512
ATT_BK = 256
ATT_VT_COLS = 512
ATT_ONES_ROWS = 16
ATT_HEADS = 2


def _prompt_attn_kernel(q_ref, k_ref, v_ref, o_ref, vt_ref, st_ref):
    bq = ATT_BQ
    seq = k_ref.shape[0]

    heads = range(ATT_HEADS)
    for r in range(seq // ATT_VT_COLS):
        rs = slice(r * ATT_VT_COLS, (r + 1) * ATT_VT_COLS)
        vt = v_ref[rs, :].astype(F32).T.astype(BF16)
        for hh in heads:
            vt_ref[hh, :V_DIM, rs] = vt[hh * V_DIM:(hh + 1) * V_DIM]
            vt_ref[hh, V_DIM:, rs] = jnp.ones((ATT_ONES_ROWS, ATT_VT_COLS), BF16)

    def query_block(i):
        rows = slice(i * bq, (i + 1) * bq)
        qts = [q_ref[rows, hh * HEAD_PAD:(hh + 1) * HEAD_PAD].astype(F32).T.astype(BF16) for hh in heads]

        def scores(j, slot, lo=0):
            start = j * ATT_BK
            for hh in heads:
                st_ref[slot, hh, :, lo:] = _dot(
                    k_ref[pl.ds(start, ATT_BK), hh * HEAD_PAD:(hh + 1) * HEAD_PAD], qts[hh][:, lo:])

        def update(j, slot, carry, masked, lo=0):
            start = j * ATT_BK
            out = []
            for hh in heads:
                m, acc = carry[hh]
                st = st_ref[slot, hh, :, lo:]
                if masked:
                    key = start + lax.broadcasted_iota(jnp.int32, st.shape, 0)
                    qry = i * bq + lo + lax.broadcasted_iota(jnp.int32, st.shape, 1)
                    st = jnp.where(key <= qry, st, -1e30)
                m_new = jnp.maximum(m[:, lo:], jnp.max(st, axis=0, keepdims=True))
                p = jnp.exp2(st - m_new)
                acc_new = (jnp.exp2(m[:, lo:] - m_new) * acc[:, lo:]
                           + _dot(vt_ref[hh, :, pl.ds(start, ATT_BK)], p.astype(BF16)))
                if lo:
                    m_new = jnp.concatenate([m[:, :lo], m_new], axis=1)
                    acc_new = jnp.concatenate([acc[:, :lo], acc_new], axis=1)
                out.append((m_new, acc_new))
            return tuple(out)

        one = (jnp.full((1, bq), -1e30, F32), jnp.zeros((V_DIM + ATT_ONES_ROWS, bq), F32))

        carry = (one,) * ATT_HEADS
        scores(0, 0)
        for j in range(2 * i):
            scores(j + 1, (j + 1) % 2)
            carry = update(j, j % 2, carry, False)
        scores(2 * i + 1, 1, lo=ATT_BK)
        carry = update(2 * i, 0, carry, True)
        carry = update(2 * i + 1, 1, carry, True, lo=ATT_BK)
        ot = jnp.concatenate([acc[:V_DIM] / acc[V_DIM:V_DIM + 1] for (_, acc) in carry], axis=0)
        o_ref[rows, :] = ot.T.astype(o_ref.dtype)

    for i in range(seq // bq):
        query_block(i)


def _prompt_attn(q, k, v, batch, seq):
    assert ATT_BQ == 2 * ATT_BK and seq % (2 * ATT_BQ) == 0 and seq % ATT_VT_COLS == 0
    n_tok = q.shape[0]
    return pl.pallas_call(
        _prompt_attn_kernel,
        grid=(batch, N_HEADS // ATT_HEADS),
        in_specs=[pl.BlockSpec((seq, ATT_HEADS * HEAD_PAD), lambda b, hp: (b, hp)),
                  pl.BlockSpec((seq, ATT_HEADS * HEAD_PAD), lambda b, hp: (b, hp)),
                  pl.BlockSpec((seq, ATT_HEADS * V_DIM), lambda b, hp: (b, hp))],
        out_specs=pl.BlockSpec((seq, ATT_HEADS * V_DIM), lambda b, hp: (b, hp)),
        out_shape=jax.ShapeDtypeStruct((n_tok, N_HEADS * V_DIM), BF16),
        scratch_shapes=[pltpu.VMEM((ATT_HEADS, V_DIM + ATT_ONES_ROWS, seq), BF16),
                        pltpu.VMEM((2, ATT_HEADS, ATT_BK, ATT_BQ), F32)],
        compiler_params=pltpu.CompilerParams(dimension_semantics=("arbitrary",) * 2,
                                             vmem_limit_bytes=VMEM_LIMIT),
        name="prompt_attn",
    )(q, k, v)


SAMPLE_SUB = 3584
SAMPLE_SEQS = 1
SAMPLE_SLOTS = 3


def _sample_blocks(past):
    return [min(SAMPLE_SUB, past - lo) for lo in range(0, past, SAMPLE_SUB)]


def _sample_attn_kernel(layer, pt_ref, qlat_ref, qr_ref, cnew_ref, krnew_ref, wukt_ref, ckv_hbm, krt_hbm,
                        olat_ref, cbuf, kbuf, csem, ksem, lhs_ref, cb_ref):
    s = pl.program_id(0)
    n_steps = pl.num_programs(0)
    n_par = cnew_ref.shape[0]
    past = cbuf.shape[1] // n_par
    n_pages = past // PAGE
    t_new = cnew_ref.shape[1]
    kv_lora = cnew_ref.shape[2]
    n_rows = N_HEADS * t_new
    n_kn = N_HEADS * QK_NOPE
    n_slots = cbuf.shape[0]
    ahead = n_slots - 1
    slot = s % n_slots

    def copies(step, slot_):
        out = []
        for a in range(n_par):
            for p in range(n_pages):
                page = pt_ref[(step * n_par + a) * n_pages + p]
                keys = pl.ds(a * past + p * PAGE, PAGE)
                out.append(pltpu.make_async_copy(ckv_hbm.at[layer, page], cbuf.at[slot_, keys, :],
                                                 csem.at[slot_]))
                out.append(pltpu.make_async_copy(krt_hbm.at[layer, page], kbuf.at[slot_, :, keys],
                                                 ksem.at[slot_]))
        return out

    def start_all(cps):
        for n, cp in enumerate(cps):
            cp.start(priority=(n // 2) % 2)

    @pl.when(s == 0)
    def _():
        for a in range(n_par):
            lhs_ref[a, :n_kn, :] = wukt_ref[...]
        for d in range(ahead):
            start_all(copies(jnp.minimum(d, n_steps - 1), d))

    for cp in copies(s, slot):
        cp.wait()

    qrs = []
    for a in range(n_par):
        tok = slice(a * t_new, (a + 1) * t_new)
        lhs_ref[a, n_kn:, :] = qlat_ref[:, tok, :].reshape(n_rows, kv_lora).astype(BF16)
        qrs.append(qr_ref[:, tok, :].reshape(n_rows, QK_ROPE).astype(BF16))

    def scaled_scores(big, rope_scores, r2):
        n = big.shape[1]
        kn = big[:n_kn]
        ss = jnp.sum((kn * kn).reshape(N_HEADS, QK_NOPE, n), axis=1)
        rs = lax.rsqrt((ss + r2) * (1.0 / QK_HEAD) + EPS)
        sc = (big[n_kn:] + rope_scores).reshape(N_HEADS, t_new, n) * rs[:, None, :]
        return sc.reshape(n_rows, n)

    blocks = _sample_blocks(past)
    starts = [sum(blocks[:k]) for k in range(len(blocks))]

    def past_scores(a, k):
        keys = slice(a * past + starts[k], a * past + starts[k] + blocks[k])
        cb = cbuf[slot, keys, :].astype(BF16)
        cb_ref[a, k % 2, :blocks[k], :] = cb
        krt = kbuf[slot, :, keys]
        r2 = jnp.sum(krt * krt, axis=0, keepdims=True)
        return scaled_scores(_dot_nt(lhs_ref[a], cb), _dot(qrs[a], krt.astype(BF16)), r2)

    def new_scores(a):
        pad = PAGE - t_new
        cb_new = jnp.concatenate([cnew_ref[a], jnp.zeros((pad, kv_lora), F32)], axis=0).astype(BF16)
        kr_new = jnp.concatenate([krnew_ref[a], jnp.zeros((pad, QK_ROPE), F32)], axis=0)
        kr2 = kr_new * kr_new
        kr2_hi = kr2.astype(BF16)
        kr2_lo = (kr2 - kr2_hi.astype(F32)).astype(BF16)
        ones = jnp.ones((8, QK_ROPE), BF16)
        r2_new = (_dot_nt(ones, kr2_hi) + _dot_nt(ones, kr2_lo))[:1]
        sc = scaled_scores(_dot_nt(lhs_ref[a], cb_new), _dot_nt(qrs[a], kr_new.astype(BF16)), r2_new)
        row = lax.broadcasted_iota(jnp.int32, (n_rows, PAGE), 0)
        col = lax.broadcasted_iota(jnp.int32, (n_rows, PAGE), 1)
        return jnp.where(col <= row % t_new, sc, -1e30), cb_new

    n_sub = len(blocks)
    new = [new_scores(a) for a in range(n_par)]
    sc_next = [past_scores(a, 0) for a in range(n_par)]

    nxt = jnp.minimum(s + ahead, n_steps - 1)
    nxt_slot = (s + ahead) % n_slots
    start_all(copies(nxt, nxt_slot))

    state = []
    for a in range(n_par):
        sc_new, cb_new = new[a]
        m = jnp.max(sc_new, axis=-1, keepdims=True)
        p = jnp.exp2(sc_new - m)
        state.append((m, jnp.sum(p, axis=-1, keepdims=True), _dot(p.astype(BF16), cb_new)))
    for k in range(n_sub):
        sc_cur = sc_next
        if k + 1 < n_sub:
            sc_next = [past_scores(a, k + 1) for a in range(n_par)]
        for a in range(n_par):
            m, l, acc = state[a]
            m_new = jnp.maximum(m, jnp.max(sc_cur[a], axis=-1, keepdims=True))
            alpha = jnp.exp2(m - m_new)
            p = jnp.exp2(sc_cur[a] - m_new)
            l = alpha * l + jnp.sum(p, axis=-1, keepdims=True)
            acc = alpha * acc + _dot(p.astype(BF16), cb_ref[a, k % 2, :blocks[k], :])
            state[a] = (m_new, l, acc)
    for a in range(n_par):
        _, l, acc = state[a]
        olat_ref[a] = acc / l

    @pl.when(s == n_steps - 1)
    def _():
        for d in range(1, n_slots):
            for cp in copies(nxt, (s + d) % n_slots):
                cp.wait()


def _sample_attn(layer, page_table, qlat, qr, c_new, kr_new, wukt, cache_ckv, cache_krt):
    n_seq, n_pages = page_table.shape
    t_new = c_new.shape[1]
    kv_lora = c_new.shape[2]
    n_rows = N_HEADS * t_new
    past = n_pages * PAGE
    n_par = SAMPLE_SEQS
    assert n_seq % n_par == 0
    grid_spec = pltpu.PrefetchScalarGridSpec(
        num_scalar_prefetch=1,
        grid=(n_seq // n_par,),
        in_specs=[pl.BlockSpec((N_HEADS, n_par * t_new, kv_lora), lambda s, pt: (0, s, 0)),
                  pl.BlockSpec((N_HEADS, n_par * t_new, QK_ROPE), lambda s, pt: (0, s, 0)),
                  pl.BlockSpec((n_par, t_new, kv_lora), lambda s, pt: (s, 0, 0)),
                  pl.BlockSpec((n_par, t_new, QK_ROPE), lambda s, pt: (s, 0, 0)),
                  pl.BlockSpec(wukt.shape, lambda s, pt: (0, 0)),
                  pl.BlockSpec(memory_space=pl.ANY),
                  pl.BlockSpec(memory_space=pl.ANY)],
        out_specs=pl.BlockSpec((n_par, n_rows, kv_lora), lambda s, pt: (s, 0, 0)),
        scratch_shapes=[pltpu.VMEM((SAMPLE_SLOTS, n_par * past, kv_lora), F32),
                        pltpu.VMEM((SAMPLE_SLOTS, QK_ROPE, n_par * past), F32),
                        pltpu.SemaphoreType.DMA((SAMPLE_SLOTS,)),
                        pltpu.SemaphoreType.DMA((SAMPLE_SLOTS,)),
                        pltpu.VMEM((n_par, N_HEADS * QK_NOPE + n_rows, kv_lora), BF16),
                        pltpu.VMEM((n_par, 2, max(_sample_blocks(past)), kv_lora), BF16)],
    )
    return pl.pallas_call(
        functools.partial(_sample_attn_kernel, layer),
        grid_spec=grid_spec,
        out_shape=jax.ShapeDtypeStruct((n_seq, n_rows, kv_lora), F32),
        compiler_params=pltpu.CompilerParams(dimension_semantics=("arbitrary",),
                                             vmem_limit_bytes=VMEM_LIMIT),
        name="sample_attn",
    )(page_table.reshape(-1), qlat, qr, c_new, kr_new, wukt, cache_ckv, cache_krt)


def _pad_heads(w, width):
    k = w.shape[0]
    w = w.reshape(k, N_HEADS, width)
    return jnp.pad(w, ((0, 0), (0, 0), (0, HEAD_PAD - width))).reshape(k, N_HEADS * HEAD_PAD)


def _rot_cols(w):
    half = w.shape[-1] // 2
    return jnp.concatenate([-w[..., half:], w[..., :half]], axis=-1)


def _rope_block(w):
    return jnp.pad(w, ((0, 0), (QK_NOPE, HEAD_PAD - QK_HEAD)))


def _head_gain(g):
    return jnp.pad(g, (0, HEAD_PAD - QK_HEAD)).reshape(1, HEAD_PAD)


def _rope_tables(n_pos, offset):
    pos = jnp.arange(n_pos, dtype=F32) + offset
    inv_freq = ROPE_THETA ** (-jnp.arange(0, QK_ROPE, 2, dtype=F32) / QK_ROPE)
    ang = pos[:, None] * inv_freq[None, :]
    cos, sin = jnp.cos(ang), jnp.sin(ang)
    ones = jnp.ones((n_pos, QK_NOPE), F32)
    tail = jnp.ones((n_pos, HEAD_PAD - QK_HEAD), F32)
    cos_tab = jnp.concatenate([ones, cos, cos, tail], axis=1)
    sin_tab = jnp.concatenate([0 * ones, sin, sin, 0 * tail], axis=1)
    return cos_tab, sin_tab


def _layer_weights(i, ffn1_norm, ffn1_w_in, ffn1_w_out, mix_norm, w_in, qa_norm, kva_norm, w_uq, w_ukv,
                   q_norm, k_norm, sgu_ln_g, sgu_ln_b, sgu_w_s, sgu_b_s, w_a_out, w_b_out, w_o,
                   ffn2_norm, ffn2_w_in, ffn2_w_out, ple_norm, ple_w_gate, ple_w_proj):
    a_width = sgu_ln_g.shape[1]
    q_lora = qa_norm.shape[1]
    kv_lora = kva_norm.shape[1]
    s1 = 2 * a_width
    s2 = s1 + q_lora
    s3 = s2 + kv_lora
    s4 = s3 + QK_ROPE
    wi = w_in[i].astype(BF16)
    w_kr = wi[:, s3:s4]
    wpre = jnp.concatenate([wi[:, :s3], _rope_block(w_kr), _rope_block(_rot_cols(w_kr))], axis=1)

    wq = w_uq[i].astype(BF16).reshape(q_lora, N_HEADS, QK_HEAD)
    wq_rot = jnp.concatenate([jnp.zeros_like(wq[..., :QK_NOPE]), _rot_cols(wq[..., QK_NOPE:])], axis=-1)
    wuq = jnp.stack([_pad_heads(wq.reshape(q_lora, -1), QK_HEAD).reshape(q_lora, N_HEADS, HEAD_PAD),
                     _pad_heads(wq_rot.reshape(q_lora, -1), QK_HEAD).reshape(q_lora, N_HEADS, HEAD_PAD)],
                    axis=2).reshape(q_lora, 2 * N_HEADS * HEAD_PAD)

    wkv = w_ukv[i].astype(BF16).reshape(kv_lora, N_HEADS, QK_NOPE + V_DIM)
    w_uk = wkv[..., :QK_NOPE]
    w_uv = wkv[..., QK_NOPE:]
    wukv = jnp.concatenate([_pad_heads(w_uk.reshape(kv_lora, -1), QK_NOPE),
                            w_uv.reshape(kv_lora, -1)], axis=1)
    wuk_t = jnp.transpose(w_uk, (1, 2, 0))
    head_cols = (jnp.arange(N_HEADS)[:, None] == jnp.arange(N_HEADS)[None, :])[:, None, :, None]
    wuv_blk = jnp.where(head_cols, jnp.transpose(w_uv, (1, 0, 2))[:, :, None, :], 0)
    wuv_blk = wuv_blk.reshape(N_HEADS, kv_lora, N_HEADS * V_DIM)

    row = lambda g: g[i].reshape(1, -1)
    return {
        'f1g': row(ffn1_norm), 'f1in': ffn1_w_in[i].astype(BF16), 'f1out': ffn1_w_out[i].astype(BF16),
        'mixg': row(mix_norm), 'wpre': wpre, 'wgates': wi[:, s4:],
        'qag': row(qa_norm), 'kvag': row(kva_norm), 'wuq': wuq, 'wukv': wukv,
        'qg': _head_gain(q_norm[i]), 'kg': _head_gain(k_norm[i]),
        'wuk_t': wuk_t, 'wuv_blk': wuv_blk,
        'lng': row(sgu_ln_g), 'lnb': row(sgu_ln_b), 'w_s': sgu_w_s[i], 'b_s': sgu_b_s[i],
        'wa': w_a_out[i].astype(BF16), 'wb': w_b_out[i].astype(BF16), 'wo': w_o[i].astype(BF16),
        'f2g': row(ffn2_norm), 'f2in': ffn2_w_in[i].astype(BF16), 'f2out': ffn2_w_out[i].astype(BF16),
        'pleg': row(ple_norm), 'plewg': ple_w_gate[i].astype(BF16), 'plewp': ple_w_proj[i].astype(BF16),
    }


def _mix_tables(w_s, b_s, period, group_dim):
    r = jnp.arange(CHUNK)
    same = (r[:, None] // period) == (r[None, :] // period)
    causal = (r[None, :] % period) <= (r[:, None] % period)
    reps = CHUNK // period
    mixw = jnp.where((same & causal)[None], jnp.tile(w_s[:, :period, :period], (1, reps, reps)), 0.0)
    bias = jnp.tile(b_s[:, :period], (1, reps))
    mixb = jnp.repeat(bias.T, group_dim, axis=1)
    return mixw.astype(BF16), mixb


def kernel(x_prompt, x_sample, cache_ckv, cache_krope, page_table, p_prompt, p_sample, ffn1_norm, ffn1_w_in, ffn1_w_out, mix_norm, w_in, qa_norm, kva_norm, w_uq, w_ukv, q_norm, k_norm, sgu_ln_g, sgu_ln_b, sgu_w_s, sgu_b_s, w_a_out, w_b_out, w_o, ffn2_norm, ffn2_w_in, ffn2_w_out, ple_norm, ple_w_gate, ple_w_proj):
    depth = w_in.shape[0]
    batch, seq, d_model = x_prompt.shape
    dec_batch, dec_seq, _ = x_sample.shape
    past_len = page_table.shape[1] * cache_ckv.shape[2]
    assert seq % TOKEN_TILE == 0 and TOKEN_TILE % dec_seq == 0 and CHUNK % dec_seq == 0
    assert cache_ckv.shape[2] == PAGE

    yp = x_prompt.reshape(batch * seq, d_model)
    ys = x_sample.reshape(dec_batch * dec_seq, d_model)
    cos_p, sin_p = _rope_tables(seq, 0)
    cos_s, sin_s = _rope_tables(dec_seq, past_len)
    reps = TOKEN_TILE // dec_seq
    cos_s, sin_s = jnp.tile(cos_s, (reps, 1)), jnp.tile(sin_s, (reps, 1))

    outs = [[] for _ in range(5)]
    for i in range(depth):
        w = _layer_weights(i, ffn1_norm, ffn1_w_in, ffn1_w_out, mix_norm, w_in, qa_norm, kva_norm, w_uq,
                           w_ukv, q_norm, k_norm, sgu_ln_g, sgu_ln_b, sgu_w_s, sgu_b_s, w_a_out, w_b_out,
                           w_o, ffn2_norm, ffn2_w_in, ffn2_w_out, ple_norm, ple_w_gate, ple_w_proj)

        group_dim = w['lng'].shape[1] // A_GROUPS
        w['mixw'], w['mixb'] = _mix_tables(w['w_s'], w['b_s'], CHUNK, group_dim)
        h1, ain, ckv_p, kr_p, q, k, v = _pre_call(yp, cos_p, sin_p, w, False)
        o = _prompt_attn(q, k, v, batch, seq)
        yp = _post_call(h1, ain, o, p_prompt[i].reshape(batch * seq, -1), w, False)

        w['mixw'], w['mixb'] = _mix_tables(w['w_s'], w['b_s'], dec_seq, group_dim)
        h1, ain, ckv_s, kr_s, qlat, qr, vn_s = _pre_call(ys, cos_s, sin_s, w, True)
        kv_lora = ckv_s.shape[1]
        olat = _sample_attn(i, page_table, qlat, qr,
                            ckv_s.reshape(dec_batch, dec_seq, kv_lora),
                            kr_s.reshape(dec_batch, dec_seq, QK_ROPE),
                            w['wuk_t'].reshape(N_HEADS * QK_NOPE, kv_lora),
                            cache_ckv, jnp.swapaxes(cache_krope, 2, 3))
        ys = _post_call(h1, ain, olat, p_sample[i].reshape(dec_batch * dec_seq, -1), w, True)

        outs[0].append(ckv_p.reshape(batch, seq, -1))
        outs[1].append(kr_p.reshape(batch, seq, -1))
        outs[2].append(ckv_s.reshape(dec_batch, dec_seq, -1))
        outs[3].append(kr_s.reshape(dec_batch, dec_seq, -1))
        outs[4].append(vn_s.reshape(dec_batch, dec_seq, -1))

    return (yp.reshape(batch, seq, d_model), ys.reshape(dec_batch, dec_seq, d_model),
            jnp.stack(outs[0]), jnp.stack(outs[1]), jnp.stack(outs[2]), jnp.stack(outs[3]),
            jnp.stack(outs[4]))
```
